```python
import math
import jax, jax.numpy as jnp
from jax import lax
import numpy as np

D_MODEL = 1024
BATCH = 16
SEQ = 4096
DEPTH = 1
DEC_BATCH = 16
DEC_SEQ = 2048
PAST_LEN = 128

DA_HEADS = 8
DA_HEAD_DIM = 64
DA_V_DIM = 2 * DA_HEAD_DIM
DA_QK = DA_HEADS * 2 * DA_HEAD_DIM
DA_V = DA_HEADS * DA_V_DIM
Q_BLOCK = 128
DL_PAIRS = ((128, 1), (512, 4), (2048, 16))
N_GROUPS = len(DL_PAIRS)
DL_HEADS = 8
DL_HEAD_DIM = 64
DL_W = DL_HEADS * DL_HEAD_DIM
DL_BLOCK = 64
ROPE_THETA = 500000.0
ROPE_FRACTION = 4
FFN_HIDDEN = -(-8 * D_MODEL // (3 * 256)) * 256
EPS = 1e-6
NEG = -1e30
N_IN = 2 * DA_QK + DA_V + 3 * N_GROUPS * DL_W + 2 * D_MODEL

kernel_name = "hybrid_diff_dilated_encoder"


def _rmsnorm(x, g):
    xf = x.astype(jnp.float32)
    y = xf * lax.rsqrt(jnp.mean(xf * xf, axis=-1, keepdims=True) + EPS)
    return (y * g.astype(jnp.float32)).astype(x.dtype)


def _rope_tables(seq, head_dim):
    rot = head_dim // ROPE_FRACTION
    inv = 1.0 / (ROPE_THETA ** (jnp.arange(0, rot, 2, dtype=jnp.float32) / rot))
    ang = jnp.arange(seq, dtype=jnp.float32)[:, None] * inv[None, :]
    return jnp.cos(ang), jnp.sin(ang)


def _rope(t, cos, sin):
    half = cos.shape[-1]
    tf = t.astype(jnp.float32)
    t1, t2, tp = tf[..., :half], tf[..., half:2 * half], tf[..., 2 * half:]
    c = cos[None, :, None, :]
    s = sin[None, :, None, :]
    return jnp.concatenate([t1 * c - t2 * s, t2 * c + t1 * s, tp], axis=-1).astype(t.dtype)


def _diff_attention(q, k, v, lam):
    B, S, H, _, E = q.shape
    nq = S // Q_BLOCK
    qb = q.reshape(B, nq, Q_BLOCK, H, 2, E).transpose(1, 0, 2, 3, 4, 5)
    scale = E ** -0.5

    def one_block(qblk):
        s = jnp.einsum('bqhce,bkhce->bhcqk', qblk, k).astype(jnp.float32) * scale
        p = jax.nn.softmax(s, axis=-1)
        a = p[:, :, 0] - lam * p[:, :, 1]
        return jnp.einsum('bhqk,bkhe->bqhe', a.astype(v.dtype), v)

    o = lax.map(one_block, qb)
    return o.transpose(1, 0, 2, 3, 4).reshape(B, S, H, v.shape[-1])


def _neighbour_blocks(t, blk):
    nb = t.shape[2] // blk
    pad = [(0, 0), (0, 0), (blk, blk)] + [(0, 0)] * (t.ndim - 3)
    tb = jnp.pad(t, pad).reshape(t.shape[:2] + (nb + 2, blk) + t.shape[3:])
    return jnp.concatenate([tb[:, :, :-2], tb[:, :, 1:-1], tb[:, :, 2:]], axis=3)


def _dilated_group(q, k, v, dil, radius):
    B, S, H, E = q.shape
    blk = DL_BLOCK
    span = dil * blk
    Sp = -(-S // span) * span
    T = Sp // dil
    nb = T // blk
    padw = [(0, 0), (0, Sp - S), (0, 0), (0, 0)]

    def strided(t):
        return jnp.pad(t, padw).reshape(B, T, dil, H, E).transpose(0, 2, 1, 3, 4)

    qs, ks, vs = strided(q), strided(k), strided(v)
    qb = qs.reshape(B, dil, nb, blk, H, E)
    kn = _neighbour_blocks(ks, blk)
    vn = _neighbour_blocks(vs, blk)
    valid = (jnp.arange(Sp) < S).reshape(T, dil).T[None]
    kmask = _neighbour_blocks(valid, blk)
    qi = jnp.arange(blk)[:, None]
    kj = jnp.arange(3 * blk)[None, :]
    band = jnp.abs(kj - blk - qi) <= radius
    mask = band[None, None, None, None] & kmask[:, :, :, None, None, :]

    s = jnp.einsum('brnqhe,brnkhe->brnhqk', qb, kn).astype(jnp.float32) * (E ** -0.5)
    s = jnp.where(mask, s, NEG)
    m = jnp.max(s, axis=-1, keepdims=True)
    p = jnp.exp(s - m)
    den = jnp.sum(p, axis=-1)
    o = jnp.einsum('brnhqk,brnkhe->brnqhe', (p / den[..., None]).astype(v.dtype), vn)
    lse = m[..., 0] + jnp.log(den)
    o = o.reshape(B, dil, T, H, E).transpose(0, 2, 1, 3, 4).reshape(B, Sp, H, E)[:, :S]
    lse = lse.transpose(0, 1, 2, 4, 3).reshape(B, dil, T, H).transpose(0, 2, 1, 3).reshape(B, Sp, H)[:, :S]
    return o, lse


def _mixer(h, lambda_init, w_in, lam_q1, lam_k1, lam_q2, lam_k2, g_subln, w_a, w_b, w_out, cos, sin):
    B, S, _ = h.shape
    z = h @ w_in
    sizes = [DA_QK, DA_QK, DA_V] + [DL_W] * (3 * N_GROUPS) + [D_MODEL, D_MODEL]
    parts = jnp.split(z, np.cumsum(sizes)[:-1].tolist(), axis=-1)

    qa = _rope(parts[0].reshape(B, S, DA_HEADS * 2, DA_HEAD_DIM), cos, sin).reshape(B, S, DA_HEADS, 2, DA_HEAD_DIM)
    ka = _rope(parts[1].reshape(B, S, DA_HEADS * 2, DA_HEAD_DIM), cos, sin).reshape(B, S, DA_HEADS, 2, DA_HEAD_DIM)
    va = parts[2].reshape(B, S, DA_HEADS, DA_V_DIM)
    lam = (jnp.exp(jnp.sum(lam_q1.astype(jnp.float32) * lam_k1.astype(jnp.float32)))
           - jnp.exp(jnp.sum(lam_q2.astype(jnp.float32) * lam_k2.astype(jnp.float32))) + lambda_init)
    oa = _diff_attention(qa, ka, va, lam)
    oa = (_rmsnorm(oa, g_subln) * (1.0 - lambda_init)).reshape(B, S, DA_V)

    outs, lses = [], []
    for g, (window, dil) in enumerate(DL_PAIRS):
        base = 3 + 3 * g
        qg = _rope(parts[base].reshape(B, S, DL_HEADS, DL_HEAD_DIM), cos, sin)
        kg = _rope(parts[base + 1].reshape(B, S, DL_HEADS, DL_HEAD_DIM), cos, sin)
        vg = parts[base + 2].reshape(B, S, DL_HEADS, DL_HEAD_DIM)
        o, l = _dilated_group(qg, kg, vg, dil, window // (2 * dil))
        outs.append(o)
        lses.append(l)
    wgt = jax.nn.softmax(jnp.stack(lses), axis=0)
    ob = jnp.einsum('gbsh,gbshe->bshe', wgt, jnp.stack(outs).astype(jnp.float32)).astype(h.dtype).reshape(B, S, DL_W)

    merged = jax.nn.sigmoid(parts[-2]) * (oa @ w_a) + jax.nn.sigmoid(parts[-1]) * (ob @ w_b)
    return merged @ w_out


def _trunk(x, c, w_ada, b_ada, g_mix, w_in, lam_q1, lam_k1, lam_q2, lam_k2, g_subln,
           w_a, w_b, w_out, g_ffn, w_gu, w_down, g_final):
    cos, sin = _rope_tables(x.shape[1], DL_HEAD_DIM)
    for l in range(DEPTH):
        lambda_init = 0.8 - 0.6 * math.exp(-0.3 * l)
        mod = (jax.nn.silu(c) @ w_ada[l] + b_ada[l])[:, None, :]
        sh1, sc1, gt1, sh2, sc2, gt2 = jnp.split(mod, 6, axis=-1)
        h = _rmsnorm(x, g_mix[l]) * (1 + sc1) + sh1
        x = x + gt1 * _mixer(h, lambda_init, w_in[l], lam_q1[l], lam_k1[l], lam_q2[l], lam_k2[l],
                             g_subln[l], w_a[l], w_b[l], w_out[l], cos, sin)
        h = _rmsnorm(x, g_ffn[l]) * (1 + sc2) + sh2
        gate, up = jnp.split(h @ w_gu[l], 2, axis=-1)
        x = x + gt2 * ((jax.nn.silu(gate) * up) @ w_down[l])
    return _rmsnorm(x, g_final)


def setup_inputs(seed: int = 0) -> dict:
    key = jax.random.key(seed)
    ks = jax.random.split(key, 24)
    f32 = jnp.float32

    def nrm(k, shape, scale):
        return jax.random.normal(k, shape, f32) * scale

    def gain(k, shape):
        return 1.0 + 0.02 * jax.random.normal(k, shape, f32)

    return {
        'x_prompt': nrm(ks[0], (BATCH, SEQ, D_MODEL), 1.0),
        'x_sample': nrm(ks[1], (DEC_BATCH, DEC_SEQ, D_MODEL), 1.0),
        'c_prompt': nrm(ks[2], (BATCH, D_MODEL), 1.0),
        'c_sample': nrm(ks[3], (DEC_BATCH, D_MODEL), 1.0),
        'w_ada': nrm(ks[4], (DEPTH, D_MODEL, 6 * D_MODEL), D_MODEL ** -0.5),
        'b_ada': nrm(ks[5], (DEPTH, 6 * D_MODEL), 0.01),
        'g_mix': gain(ks[6], (DEPTH, D_MODEL)),
        'w_in': nrm(ks[7], (DEPTH, D_MODEL, N_IN), D_MODEL ** -0.5),
        'lam_q1': nrm(ks[8], (DEPTH, DA_HEAD_DIM), 0.1),
        'lam_k1': nrm(ks[9], (DEPTH, DA_HEAD_DIM), 0.1),
        'lam_q2': nrm(ks[10], (DEPTH, DA_HEAD_DIM), 0.1),
        'lam_k2': nrm(ks[11], (DEPTH, DA_HEAD_DIM), 0.1),
        'g_subln': gain(ks[12], (DEPTH, DA_V_DIM)),
        'w_a': nrm(ks[13], (DEPTH, DA_V, D_MODEL), DA_V ** -0.5),
        'w_b': nrm(ks[14], (DEPTH, DL_W, D_MODEL), DL_W ** -0.5),
        'w_out': nrm(ks[15], (DEPTH, D_MODEL, D_MODEL), D_MODEL ** -0.5),
        'g_ffn': gain(ks[16], (DEPTH, D_MODEL)),
        'w_gu': nrm(ks[17], (DEPTH, D_MODEL, 2 * FFN_HIDDEN), D_MODEL ** -0.5),
        'w_down': nrm(ks[18], (DEPTH, FFN_HIDDEN, D_MODEL), FFN_HIDDEN ** -0.5),
        'g_final': gain(ks[19], (D_MODEL,)),
    }


def reference(x_prompt, x_sample, c_prompt, c_sample, w_ada, b_ada, g_mix, w_in,
              lam_q1, lam_k1, lam_q2, lam_k2, g_subln, w_a, w_b, w_out,
              g_ffn, w_gu, w_down, g_final):
    y_prompt = _trunk(x_prompt, c_prompt, w_ada, b_ada, g_mix, w_in, lam_q1, lam_k1, lam_q2, lam_k2,
                      g_subln, w_a, w_b, w_out, g_ffn, w_gu, w_down, g_final)
    y_sample = _trunk(x_sample, c_sample, w_ada, b_ada, g_mix, w_in, lam_q1, lam_k1, lam_q2, lam_k2,
                      g_subln, w_a, w_b, w_out, g_ffn, w_gu, w_down, g_final)
    return (y_prompt, y_sample)
```

```python
import functools
import math

import jax
import jax.numpy as jnp
from jax import lax
from jax.experimental import pallas as pl
from jax.experimental.pallas import tpu as pltpu

F32 = jnp.float32
BF16 = jnp.bfloat16

D_MODEL = 1024
DA_HEADS = 8
DA_HEAD_DIM = 64
DA_PAIR = 2 * DA_HEAD_DIM
DL_PAIRS = ((128, 1), (512, 4), (2048, 16))
DL_HEADS = 8
DL_HEAD_DIM = 64
DL_W = DL_HEADS * DL_HEAD_DIM
DL_BLOCK = 64
ROPE_THETA = 500000.0
ROPE_ROT = DL_HEAD_DIM // 4
ROPE_HALF = ROPE_ROT // 2
EPS = 1e-6
NEG = -1e30

LANES = 128
COL_TILE = 512
N_IN = 2 * DA_HEADS * DA_PAIR + DA_HEADS * DA_PAIR + 9 * DL_W + 2 * D_MODEL
N_COL_TILES = N_IN // COL_TILE
CT_QA, CT_KA, CT_VA = 0, 2, 4
CT_DL = 6
CT_GA, CT_GB = 15, 17
VMEM_LIMIT = 56 * 1024 * 1024


def _cparams(sem):
    return pltpu.CompilerParams(dimension_semantics=sem, vmem_limit_bytes=VMEM_LIMIT)


def _nt_dot(a, b):
    return lax.dot_general(a, b, (((1,), (1,)), ((), ())), preferred_element_type=F32)


def _dot(a, b):
    return jnp.dot(a, b, preferred_element_type=F32)


def _rms(x):
    return x * lax.rsqrt(jnp.mean(x * x, axis=-1, keepdims=True) + EPS)


def _ada_kernel(c_ref, w_ref, b_ref, o_ref):
    c = c_ref[...]
    a = c * (1.0 / (1.0 + jnp.exp(-c)))
    w = w_ref[...]
    a_hi = a.astype(BF16)
    a_lo = (a - a_hi.astype(F32)).astype(BF16)
    w_hi = w.astype(BF16)
    w_lo = (w - w_hi.astype(F32)).astype(BF16)
    o_ref[...] = _dot(a_hi, w_hi) + _dot(a_hi, w_lo) + _dot(a_lo, w_hi) + b_ref[...]


def _ada(c, w_ada, b_ada):
    nb, d = c.shape
    n = w_ada.shape[1]
    tn = COL_TILE
    return pl.pallas_call(
        _ada_kernel,
        grid=(n // tn,),
        in_specs=[pl.BlockSpec((nb, d), lambda j: (0, 0)),
                  pl.BlockSpec((d, tn), lambda j: (0, j)),
                  pl.BlockSpec((1, tn), lambda j: (0, j))],
        out_specs=pl.BlockSpec((nb, tn), lambda j: (0, j)),
        out_shape=jax.ShapeDtypeStruct((nb, n), F32),
        compiler_params=_cparams(("arbitrary",)),
        name="ada",
    )(c, w_ada, b_ada.reshape(1, n))


def _is_rope_tile(j):
    in_a = j < CT_VA
    r = j - CT_DL
    in_b = (r >= 0) & (j < CT_GA) & (lax.rem(r, 3) != 2)
    return in_a | in_b


def _inproj_kernel(x_ref, mod_ref, g_ref, w_ref, cs_ref, sn_ref, z_ref, h_scr, *, row_chunk):
    j = pl.program_id(1)
    tm = x_ref.shape[0]

    @pl.when(j == 0)
    def _():
        y = _rms(x_ref[...]) * g_ref[...]
        h = y * (1.0 + mod_ref[1:2, :]) + mod_ref[0:1, :]
        h_scr[...] = h.astype(BF16)

    rope = _is_rope_tile(j)

    @pl.when(rope)
    def _():
        lane = lax.broadcasted_iota(jnp.int32, (row_chunk, LANES), 1)
        first_half = lax.rem(lane, DL_HEAD_DIM) < ROPE_HALF
        for r in range(tm // row_chunk):
            rows = pl.ds(r * row_chunk, row_chunk)
            z = _dot(h_scr[rows, :], w_ref[...])
            cs = cs_ref[rows, :]
            sn = sn_ref[rows, :]
            for cb in range(COL_TILE // LANES):
                zc = z[:, cb * LANES:(cb + 1) * LANES]
                partner = jnp.where(first_half, pltpu.roll(zc, LANES - ROPE_HALF, 1),
                                    pltpu.roll(zc, ROPE_HALF, 1))
                z_ref[rows, cb * LANES:(cb + 1) * LANES] = (zc * cs + partner * sn).astype(BF16)

    @pl.when(jnp.logical_not(rope))
    def _():
        for r in range(tm // row_chunk):
            rows = pl.ds(r * row_chunk, row_chunk)
            z_ref[rows, :] = _dot(h_scr[rows, :], w_ref[...]).astype(BF16)


def _rope_tables(seq):
    inv = 1.0 / (ROPE_THETA ** (jnp.arange(0, ROPE_ROT, 2, dtype=F32) / ROPE_ROT))
    ang = jnp.arange(seq, dtype=F32)[:, None] * inv[None, :]
    cos, sin = jnp.cos(ang), jnp.sin(ang)
    ones = jnp.ones((seq, DL_HEAD_DIM - ROPE_ROT), F32)
    zeros = jnp.zeros((seq, DL_HEAD_DIM - ROPE_ROT), F32)
    cs = jnp.concatenate([cos, cos, ones], axis=1)
    sn = jnp.concatenate([-sin, sin, zeros], axis=1)
    reps = LANES // DL_HEAD_DIM
    return jnp.tile(cs, (1, reps)), jnp.tile(sn, (1, reps))


def _inproj(x, mod, g_mix, w_in_bf, cs, sn):
    b, s, d = x.shape
    tm = min(1024, s)
    nsb = s // tm
    xm = x.reshape(b * s, d)
    return pl.pallas_call(
        functools.partial(_inproj_kernel, row_chunk=min(256, tm)),
        grid=(b * nsb, N_COL_TILES),
        in_specs=[pl.BlockSpec((tm, d), lambda i, j: (i, 0)),
                  pl.BlockSpec((None, 6, d), lambda i, j: (i // nsb, 0, 0)),
                  pl.BlockSpec((1, d), lambda i, j: (0, 0)),
                  pl.BlockSpec((d, COL_TILE), lambda i, j: (0, j)),
                  pl.BlockSpec((tm, LANES), lambda i, j: (i % nsb, 0)),
                  pl.BlockSpec((tm, LANES), lambda i, j: (i % nsb, 0))],
        out_specs=pl.BlockSpec((tm, COL_TILE), lambda i, j: (i, j)),
        out_shape=jax.ShapeDtypeStruct((b * s, N_IN), BF16),
        scratch_shapes=[pltpu.VMEM((tm, d), BF16)],
        compiler_params=_cparams(("parallel", "arbitrary")),
        name="inproj",
    )(xm, mod, g_mix.reshape(1, d), w_in_bf, cs, sn).reshape(b, s, N_IN)


def _diffattn_kernel(q_ref, k_ref, v_ref, lq1_ref, lk1_ref, lq2_ref, lk2_ref, gs_ref, o_ref,
                     v2_scr, acc_scr, m_scr, *, tk, lambda_init):
    qi = pl.program_id(2)
    tq = q_ref.shape[0]
    seq = k_ref.shape[0]

    @pl.when(qi == 0)
    def _():
        v2_scr[:, :DA_PAIR] = v_ref[...]
        v2_scr[:, DA_PAIR:] = jnp.ones((seq, DA_PAIR), BF16)

    lane = lax.broadcasted_iota(jnp.int32, (tq, DA_PAIR), 1)
    q = q_ref[...] * jnp.asarray(DA_HEAD_DIM ** -0.5, BF16)
    zero = jnp.zeros_like(q)
    qmaps = (jnp.where(lane < DA_HEAD_DIM, q, zero), jnp.where(lane >= DA_HEAD_DIM, q, zero))
    m_scr[...] = jnp.full(m_scr.shape, NEG, F32)
    acc_scr[...] = jnp.zeros(acc_scr.shape, F32)

    def body(kk, carry):
        rows = pl.ds(pl.multiple_of(kk * tk, tk), tk)
        kb = k_ref[rows, :]
        vb = v2_scr[rows, :]
        for mp in range(2):
            s = _nt_dot(qmaps[mp], kb)
            m_prev = m_scr[mp]
            m_new = jnp.maximum(m_prev, jnp.max(s, axis=1, keepdims=True))
            alpha = jnp.exp(m_prev - m_new)
            p = jnp.exp(s - jnp.tile(m_new, (1, tk // LANES)))
            acc_scr[mp] = acc_scr[mp] * jnp.tile(alpha, (1, 2)) + _dot(p.astype(BF16), vb)
            m_scr[mp] = m_new
        return carry

    lax.fori_loop(0, seq // tk, body, 0)

    o1 = acc_scr[0, :, :DA_PAIR] / acc_scr[0, :, DA_PAIR:]
    o2 = acc_scr[1, :, :DA_PAIR] / acc_scr[1, :, DA_PAIR:]
    lam = (jnp.exp(jnp.sum(lq1_ref[...] * lk1_ref[...], axis=1, keepdims=True))
           - jnp.exp(jnp.sum(lq2_ref[...] * lk2_ref[...], axis=1, keepdims=True)) + lambda_init)
    o = o1 - lam * o2
    o_ref[...] = ((_rms(o) * gs_ref[...]) * (1.0 - lambda_init)).astype(BF16)


def _diffattn(z, lam_q1, lam_k1, lam_q2, lam_k2, g_subln, lambda_init):
    b, s, _ = z.shape
    tq = min(512, s)
    tk = min(512, s)
    ct = COL_TILE // DA_PAIR
    lam_spec = pl.BlockSpec((1, DA_HEAD_DIM), lambda bi, h, qi: (0, 0))
    return pl.pallas_call(
        functools.partial(_diffattn_kernel, tk=tk, lambda_init=lambda_init),
        grid=(b, DA_HEADS, s // tq),
        in_specs=[pl.BlockSpec((None, tq, DA_PAIR), lambda bi, h, qi: (bi, qi, CT_QA * ct + h)),
                  pl.BlockSpec((None, s, DA_PAIR), lambda bi, h, qi: (bi, 0, CT_KA * ct + h)),
                  pl.BlockSpec((None, s, DA_PAIR), lambda bi, h, qi: (bi, 0, CT_VA * ct + h)),
                  lam_spec, lam_spec, lam_spec, lam_spec,
                  pl.BlockSpec((1, DA_PAIR), lambda bi, h, qi: (0, 0))],
        out_specs=pl.BlockSpec((None, tq, DA_PAIR), lambda bi, h, qi: (bi, qi, h)),
        out_shape=jax.ShapeDtypeStruct((b, s, DA_HEADS * DA_PAIR), BF16),
        scratch_shapes=[pltpu.VMEM((s, 2 * DA_PAIR), BF16),
                        pltpu.VMEM((2, tq, 2 * DA_PAIR), F32),
                        pltpu.VMEM((2, tq, LANES), F32)],
        compiler_params=_cparams(("parallel", "parallel", "arbitrary")),
        name="diffattn",
    )(z, z, z, lam_q1.reshape(1, -1), lam_k1.reshape(1, -1), lam_q2.reshape(1, -1),
      lam_k2.reshape(1, -1), g_subln.reshape(1, -1))


def _dilated_kernel(q_ref, k_ref, v_ref, o_ref, lse_ref, *, kw, radius):
    n = pl.program_id(2)
    tq = q_ref.shape[0]
    t = k_ref.shape[0]
    q0 = n * tq
    ks = pl.multiple_of(jnp.clip(q0 - radius, 0, t - kw), DL_BLOCK)
    qpos = lax.broadcasted_iota(jnp.int32, (tq, kw), 0) + q0
    kpos = lax.broadcasted_iota(jnp.int32, (tq, kw), 1) + ks
    band = jnp.abs(kpos - qpos) <= radius
    lane = lax.broadcasted_iota(jnp.int32, (tq, LANES), 1)
    low = lane < DL_HEAD_DIM
    q = q_ref[...] * jnp.asarray(DL_HEAD_DIM ** -0.5, BF16)
    zero = jnp.zeros((tq, LANES), BF16)
    ones = jnp.ones((kw, LANES), BF16)
    lse_tile = jnp.zeros((tq, LANES), F32)
    for j in range(DL_W // LANES):
        cols = slice(j * LANES, (j + 1) * LANES)
        qp = q[:, cols]
        kp = k_ref[pl.ds(ks, kw), cols]
        vp = jnp.concatenate([v_ref[pl.ds(ks, kw), cols], ones], axis=1)
        outs = []
        for par in range(2):
            qm = jnp.where(low, qp, zero) if par == 0 else jnp.where(low, zero, qp)
            s = jnp.where(band, _nt_dot(qm, kp), NEG)
            m = jnp.max(s, axis=1, keepdims=True)
            p = jnp.exp(s - m)
            r = _dot(p.astype(BF16), vp)
            den = r[:, LANES:]
            outs.append(r[:, :LANES] / den)
            lse_tile = jnp.where(lane == 2 * j + par, m + jnp.log(den), lse_tile)
        o_ref[:, cols] = jnp.where(low, outs[0], outs[1]).astype(BF16)
    lse_ref[...] = lse_tile


def _dilated(z, g, window, dil):
    b, s, _ = z.shape
    assert s % (dil * DL_BLOCK) == 0, "sequence must tile into dilated blocks without padding"
    radius = window // (2 * dil)
    t = s // dil
    tq = min(256, t)
    kw = min(tq + 2 * radius, t)
    zr = z.reshape(b, t, dil * N_IN)
    cq = CT_DL + 3 * g
    o, lse = pl.pallas_call(
        functools.partial(_dilated_kernel, kw=kw, radius=radius),
        grid=(b, dil, t // tq),
        in_specs=[pl.BlockSpec((None, tq, DL_W), lambda bi, c, n: (bi, n, c * N_COL_TILES + cq)),
                  pl.BlockSpec((None, t, DL_W), lambda bi, c, n: (bi, 0, c * N_COL_TILES + cq + 1)),
                  pl.BlockSpec((None, t, DL_W), lambda bi, c, n: (bi, 0, c * N_COL_TILES + cq + 2))],
        out_specs=[pl.BlockSpec((None, tq, DL_W), lambda bi, c, n: (bi, n, c)),
                   pl.BlockSpec((None, tq, LANES), lambda bi, c, n: (bi, n, c))],
        out_shape=[jax.ShapeDtypeStruct((b, t, dil * DL_W), BF16),
                   jax.ShapeDtypeStruct((b, t, dil * LANES), F32)],
        compiler_params=_cparams(("parallel", "parallel", "arbitrary")),
        name=f"dilated{g}",
    )(zr, zr, zr)
    return o.reshape(b, s, DL_W), lse.reshape(b, s, LANES)


def _merge_kernel(x_ref, mod_ref, oa_ref, o0_ref, o1_ref, o2_ref, l0_ref, l1_ref, l2_ref,
                  ga0_ref, ga1_ref, gb0_ref, gb1_ref, wa_ref, wb_ref, wo_ref, ex_ref, g_ref,
                  x1_ref, h2_ref):
    l0, l1, l2 = l0_ref[...], l1_ref[...], l2_ref[...]
    mx = jnp.maximum(jnp.maximum(l0, l1), l2)
    e0, e1, e2 = jnp.exp(l0 - mx), jnp.exp(l1 - mx), jnp.exp(l2 - mx)
    inv = 1.0 / (e0 + e1 + e2)
    ob = None
    for e, o_ref in ((e0, o0_ref), (e1, o1_ref), (e2, o2_ref)):
        w = e * inv
        w_hi = w.astype(BF16)
        w_lo = (w - w_hi.astype(F32)).astype(BF16)
        wx = _dot(w_hi, ex_ref[...]) + _dot(w_lo, ex_ref[...])
        term = wx * o_ref[...].astype(F32)
        ob = term if ob is None else ob + term
    ya = _dot(oa_ref[...], wa_ref[...])
    yb = _dot(ob.astype(BF16), wb_ref[...])
    ga = jnp.concatenate([ga0_ref[...], ga1_ref[...]], axis=1).astype(F32)
    gb = jnp.concatenate([gb0_ref[...], gb1_ref[...]], axis=1).astype(F32)
    merged = ya / (1.0 + jnp.exp(-ga)) + yb / (1.0 + jnp.exp(-gb))
    y = _dot(merged.astype(BF16), wo_ref[...])
    x1 = x_ref[...] + mod_ref[2:3, :] * y
    x1_ref[...] = x1
    h2 = (_rms(x1) * g_ref[...]) * (1.0 + mod_ref[4:5, :]) + mod_ref[3:4, :]
    h2_ref[...] = h2.astype(BF16)


def _merge(x, mod, z, oa, obs, lses, wa_bf, wb_bf, wo_bf, g_ffn):
    b, s, d = x.shape
    m = b * s
    tm = min(512, s)
    nsb = s // tm
    row = lambda w: pl.BlockSpec((tm, w), lambda i: (i, 0))
    zcol = lambda ct: pl.BlockSpec((tm, COL_TILE), lambda i: (i, ct))
    full = lambda shape: pl.BlockSpec(shape, lambda i: (0, 0))
    head = jnp.arange(LANES)[:, None]
    col = jnp.arange(DL_W)[None, :]
    expand = ((col // DL_HEAD_DIM) == head).astype(BF16)
    zm = z.reshape(m, N_IN)
    x1, h2 = pl.pallas_call(
        _merge_kernel,
        grid=(m // tm,),
        in_specs=[row(d), pl.BlockSpec((None, 6, d), lambda i: (i // nsb, 0, 0)),
                  row(DA_HEADS * DA_PAIR), row(DL_W), row(DL_W), row(DL_W),
                  row(LANES), row(LANES), row(LANES),
                  zcol(CT_GA), zcol(CT_GA + 1), zcol(CT_GB), zcol(CT_GB + 1),
                  full(wa_bf.shape), full(wb_bf.shape), full(wo_bf.shape), full(expand.shape),
                  full((1, d))],
        out_specs=[row(d), row(d)],
        out_shape=[jax.ShapeDtypeStruct((m, d), F32), jax.ShapeDtypeStruct((m, d), BF16)],
        compiler_params=_cparams(("parallel",)),
        name="merge",
    )(x.reshape(m, d), mod, oa.reshape(m, -1), *[o.reshape(m, DL_W) for o in obs],
      *[l.reshape(m, LANES) for l in lses], zm, zm, zm, zm, wa_bf, wb_bf, wo_bf, expand,
      g_ffn.reshape(1, d))
    return x1, h2


def _ffn_kernel(h_ref, x1_ref, mod_ref, wg_ref, wu_ref, wd_ref, gf_ref, o_ref, *, chunk, final_norm):
    h = h_ref[...]
    hidden = wd_ref.shape[0]
    acc = None
    for c in range(hidden // chunk):
        cols = slice(c * chunk, (c + 1) * chunk)
        gate = _dot(h, wg_ref[:, cols])
        up = _dot(h, wu_ref[:, cols])
        f = (gate / (1.0 + jnp.exp(-gate))) * up
        part = _dot(f.astype(BF16), wd_ref[cols, :])
        acc = part if acc is None else acc + part
    x2 = x1_ref[...] + mod_ref[5:6, :] * acc
    o_ref[...] = _rms(x2) * gf_ref[...] if final_norm else x2


def _ffn(h2, x1, mod, s, wg_bf, wu_bf, wd_bf, g_final, final_norm):
    m, d = x1.shape
    tm = min(512, s)
    nsb = s // tm
    hidden = wd_bf.shape[0]
    chunk = hidden // 2
    assert chunk % LANES == 0
    row = pl.BlockSpec((tm, d), lambda i: (i, 0))
    full = lambda shape: pl.BlockSpec(shape, lambda i: (0, 0))
    return pl.pallas_call(
        functools.partial(_ffn_kernel, chunk=chunk, final_norm=final_norm),
        grid=(m // tm,),
        in_specs=[row, row, pl.BlockSpec((None, 6, d), lambda i: (i // nsb, 0, 0)),
                  full(wg_bf.shape), full(wu_bf.shape), full(wd_bf.shape), full((1, d))],
        out_specs=row,
        out_shape=jax.ShapeDtypeStruct((m, d), F32),
        compiler_params=_cparams(("parallel",)),
        name="ffn",
    )(h2, x1, mod, wg_bf, wu_bf, wd_bf, g_final.reshape(1, d))


def _trunk(x, mod_all, weights):
    (g_mix, w_in, lam_q1, lam_k1, lam_q2, lam_k2, g_subln, w_a, w_b, w_out, g_ffn,
     w_gu, w_down, g_final) = weights
    b, s, d = x.shape
    depth = w_in.shape[0]
    cs, sn = _rope_tables(s)
    for l in range(depth):
        lambda_init = 0.8 - 0.6 * math.exp(-0.3 * l)
        mod = mod_all[l]
        z = _inproj(x, mod, g_mix[l], w_in[l].astype(BF16), cs, sn)
        oa = _diffattn(z, lam_q1[l], lam_k1[l], lam_q2[l], lam_k2[l], g_subln[l], lambda_init)
        obs, lses = [], []
        for g, (window, dil) in enumerate(DL_PAIRS):
            o, lse = _dilated(z, g, window, dil)
            obs.append(o)
            lses.append(lse)
        x1, h2 = _merge(x, mod, z, oa, obs, lses, w_a[l].astype(BF16), w_b[l].astype(BF16),
                        w_out[l].astype(BF16), g_ffn[l])
        hidden = w_down.shape[1]
        wgu = w_gu[l].astype(BF16)
        last = l == depth - 1
        x = _ffn(h2, x1, mod, s, wgu[:, :hidden], wgu[:, hidden:], w_down[l].astype(BF16),
                 g_final, last).reshape(b, s, d)
        if not last:
            continue
    return x


def kernel(x_prompt, x_sample, c_prompt, c_sample, w_ada, b_ada, g_mix, w_in, lam_q1, lam_k1,
           lam_q2, lam_k2, g_subln, w_a, w_b, w_out, g_ffn, w_gu, w_down, g_final):
    depth = w_in.shape[0]
    bp = c_prompt.shape[0]
    c_all = jnp.concatenate([c_prompt, c_sample], axis=0)
    mods = [_ada(c_all, w_ada[l], b_ada[l]).reshape(c_all.shape[0], 6, D_MODEL) for l in range(depth)]
    weights = (g_mix, w_in, lam_q1, lam_k1, lam_q2, lam_k2, g_subln, w_a, w_b, w_out, g_ffn,
               w_gu, w_down, g_final)
    y_prompt = _trunk(x_prompt, [m[:bp] for m in mods], weights)
    y_sample = _trunk(x_sample, [m[bp:] for m in mods], weights)
    return (y_prompt, y_sample)
```

```python
import functools
import math

import jax
import jax.numpy as jnp
from jax import lax
from jax.experimental import pallas as pl
from jax.experimental.pallas import tpu as pltpu

F32 = jnp.float32
BF16 = jnp.bfloat16

D_MODEL = 1024
DA_HEADS = 8
DA_HEAD_DIM = 64
DA_PAIR = 2 * DA_HEAD_DIM
DL_PAIRS = ((128, 1), (512, 4), (2048, 16))
DL_HEADS = 8
DL_HEAD_DIM = 64
DL_W = DL_HEADS * DL_HEAD_DIM
DL_BLOCK = 64
ROPE_THETA = 500000.0
ROPE_ROT = DL_HEAD_DIM // 4
ROPE_HALF = ROPE_ROT // 2
EPS = 1e-6
NEG = -1e30
LOG2E = math.log2(math.e)
LN2 = math.log(2.0)
Q_SCALE = DA_HEAD_DIM ** -0.5 * LOG2E

LANES = 128
COL_TILE = 512
N_IN = 2 * DA_HEADS * DA_PAIR + DA_HEADS * DA_PAIR + 9 * DL_W + 2 * D_MODEL
N_COL_TILES = N_IN // COL_TILE
CT_QA, CT_KA, CT_VA = 0, 2, 4
CT_DL = 6
CT_GA = 15
VMEM_LIMIT = 56 * 1024 * 1024


def _cparams(sem):
    return pltpu.CompilerParams(dimension_semantics=sem, vmem_limit_bytes=VMEM_LIMIT)


def _nt_dot(a, b):
    return lax.dot_general(a, b, (((1,), (1,)), ((), ())), preferred_element_type=F32)


def _dot(a, b):
    return jnp.dot(a, b, preferred_element_type=F32)


def _rms(x):
    return x * lax.rsqrt(jnp.mean(x * x, axis=-1, keepdims=True) + EPS)


def _ada_kernel(c_ref, w_ref, b_ref, o_ref):
    c = c_ref[...]
    a = c * (1.0 / (1.0 + jnp.exp(-c)))
    w = w_ref[...]
    a_hi = a.astype(BF16)
    a_lo = (a - a_hi.astype(F32)).astype(BF16)
    w_hi = w.astype(BF16)
    w_lo = (w - w_hi.astype(F32)).astype(BF16)
    o_ref[...] = _dot(a_hi, w_hi) + _dot(a_hi, w_lo) + _dot(a_lo, w_hi) + b_ref[...]


def _ada(c, w_ada, b_ada):
    nb, d = c.shape
    n = w_ada.shape[1]
    tn = COL_TILE
    return pl.pallas_call(
        _ada_kernel,
        grid=(n // tn,),
        in_specs=[pl.BlockSpec((nb, d), lambda j: (0, 0)),
                  pl.BlockSpec((d, tn), lambda j: (0, j)),
                  pl.BlockSpec((1, tn), lambda j: (0, j))],
        out_specs=pl.BlockSpec((nb, tn), lambda j: (0, j)),
        out_shape=jax.ShapeDtypeStruct((nb, n), F32),
        compiler_params=_cparams(("arbitrary",)),
        name="ada",
    )(c, w_ada, b_ada.reshape(1, n))


def _is_rope_tile(j):
    r = j - CT_DL
    return (j < CT_VA) | ((r >= 0) & (j < CT_GA) & (lax.rem(r, 3) != 2))


def _is_query_tile(j):
    r = j - CT_DL
    return (j < CT_KA) | ((r >= 0) & (j < CT_GA) & (lax.rem(r, 3) == 0))


def _inproj_kernel(x_ref, mod_ref, g_ref, w_ref, cs_ref, sn_ref, za_ref, zg_ref, zb0_ref, zb1_ref,
                   zb2_ref, h_scr, zs_scr, *, row_chunk):
    j = pl.program_id(1)
    tm = x_ref.shape[0]

    @pl.when(j == 0)
    def _():
        y = _rms(x_ref[...]) * g_ref[...]
        h = y * (1.0 + mod_ref[1:2, :]) + mod_ref[0:1, :]
        h_scr[...] = h.astype(BF16)

    rope = _is_rope_tile(j)

    @pl.when(rope)
    def _():
        lane = lax.broadcasted_iota(jnp.int32, (row_chunk, LANES), 1)
        first_half = lax.rem(lane, DL_HEAD_DIM) < ROPE_HALF
        qscale = jnp.where(_is_query_tile(j), Q_SCALE, 1.0).astype(F32)
        for r in range(tm // row_chunk):
            rows = pl.ds(r * row_chunk, row_chunk)
            z = _dot(h_scr[rows, :], w_ref[...])
            cs = cs_ref[rows, :] * qscale
            sn = sn_ref[rows, :] * qscale
            for cb in range(COL_TILE // LANES):
                zc = z[:, cb * LANES:(cb + 1) * LANES]
                partner = jnp.where(first_half, pltpu.roll(zc, LANES - ROPE_HALF, 1),
                                    pltpu.roll(zc, ROPE_HALF, 1))
                zs_scr[cb, rows, :] = zc * cs + partner * sn

    @pl.when(jnp.logical_not(rope))
    def _():
        for r in range(tm // row_chunk):
            rows = pl.ds(r * row_chunk, row_chunk)
            z = _dot(h_scr[rows, :], w_ref[...])
            for cb in range(COL_TILE // LANES):
                zs_scr[cb, rows, :] = z[:, cb * LANES:(cb + 1) * LANES]

    def emit(dst_ref, src_rows):
        for cb in range(COL_TILE // LANES):
            dst_ref[..., cb * LANES:(cb + 1) * LANES] = zs_scr[cb, src_rows, :].astype(BF16)

    @pl.when(j < CT_DL)
    def _():
        emit(za_ref, slice(None))

    @pl.when(j >= CT_GA)
    def _():
        emit(zg_ref, slice(None))

    for g, zb_ref in enumerate((zb0_ref, zb1_ref, zb2_ref)):
        dil = DL_PAIRS[g][1]

        @pl.when((j >= CT_DL + 3 * g) & (j < CT_DL + 3 * g + 3))
        def _(zb_ref=zb_ref, dil=dil):
            for c in range(dil):
                emit(zb_ref.at[c], pl.ds(c, tm // dil, stride=dil))


def _rope_tables(seq):
    inv = 1.0 / (ROPE_THETA ** (jnp.arange(0, ROPE_ROT, 2, dtype=F32) / ROPE_ROT))
    ang = jnp.arange(seq, dtype=F32)[:, None] * inv[None, :]
    cos, sin = jnp.cos(ang), jnp.sin(ang)
    ones = jnp.ones((seq, DL_HEAD_DIM - ROPE_ROT), F32)
    zeros = jnp.zeros((seq, DL_HEAD_DIM - ROPE_ROT), F32)
    cs = jnp.concatenate([cos, cos, ones], axis=1)
    sn = jnp.concatenate([-sin, sin, zeros], axis=1)
    reps = LANES // DL_HEAD_DIM
    return jnp.tile(cs, (1, reps)), jnp.tile(sn, (1, reps))


def _inproj(x, mod, g_mix, w_in_bf, cs, sn):
    b, s, d = x.shape
    tm = min(1024, s)
    nsb = s // tm
    m = b * s
    out_shape = [jax.ShapeDtypeStruct((m, CT_DL * COL_TILE), BF16),
                 jax.ShapeDtypeStruct((m, (N_COL_TILES - CT_GA) * COL_TILE), BF16)]
    out_specs = [pl.BlockSpec((tm, COL_TILE), lambda i, j: (i, jnp.minimum(j, CT_DL - 1))),
                 pl.BlockSpec((tm, COL_TILE), lambda i, j: (i, jnp.maximum(j - CT_GA, 0)))]
    for g, (_, dil) in enumerate(DL_PAIRS):
        assert s % (dil * DL_BLOCK) == 0, "sequence must tile into dilated blocks without padding"
        out_shape.append(jax.ShapeDtypeStruct((b, dil, s // dil, 3 * DL_W), BF16))
        out_specs.append(pl.BlockSpec(
            (None, dil, tm // dil, COL_TILE),
            lambda i, j, g=g: (i // nsb, 0, i % nsb, jnp.clip(j - (CT_DL + 3 * g), 0, 2))))
    return pl.pallas_call(
        functools.partial(_inproj_kernel, row_chunk=min(256, tm)),
        grid=(b * nsb, N_COL_TILES),
        in_specs=[pl.BlockSpec((tm, d), lambda i, j: (i, 0)),
                  pl.BlockSpec((None, 6, d), lambda i, j: (i // nsb, 0, 0)),
                  pl.BlockSpec((1, d), lambda i, j: (0, 0)),
                  pl.BlockSpec((d, COL_TILE), lambda i, j: (0, j)),
                  pl.BlockSpec((tm, LANES), lambda i, j: (i % nsb, 0)),
                  pl.BlockSpec((tm, LANES), lambda i, j: (i % nsb, 0))],
        out_specs=out_specs,
        out_shape=out_shape,
        scratch_shapes=[pltpu.VMEM((tm, d), BF16), pltpu.VMEM((COL_TILE // LANES, tm, LANES), F32)],
        compiler_params=_cparams(("parallel", "arbitrary")),
        name="inproj",
    )(x.reshape(m, d), mod, g_mix.reshape(1, d), w_in_bf, cs, sn)


def _diffattn_kernel(q_ref, k_ref, v_ref, lq1_ref, lk1_ref, lq2_ref, lk2_ref, gs_ref, o_ref,
                     v2_scr, acc_scr, m_scr, *, tk, lambda_init):
    qi = pl.program_id(2)
    tq = q_ref.shape[0]
    seq = k_ref.shape[0]

    @pl.when(qi == 0)
    def _():
        v2_scr[:, :DA_PAIR] = v_ref[...]
        v2_scr[:, DA_PAIR:] = jnp.ones((seq, DA_PAIR), BF16)

    lane = lax.broadcasted_iota(jnp.int32, (tq, DA_PAIR), 1)
    q = q_ref[...]
    zero = jnp.zeros_like(q)
    qmaps = (jnp.where(lane < DA_HEAD_DIM, q, zero), jnp.where(lane >= DA_HEAD_DIM, q, zero))
    m_scr[...] = jnp.full(m_scr.shape, NEG, F32)
    acc_scr[...] = jnp.zeros(acc_scr.shape, F32)

    def body(kk, carry):
        rows = pl.ds(pl.multiple_of(kk * tk, tk), tk)
        kb = k_ref[rows, :]
        vb = v2_scr[rows, :]
        for mp in range(2):
            s = _nt_dot(qmaps[mp], kb)
            m_prev = m_scr[mp]
            m_new = jnp.maximum(m_prev, jnp.max(s, axis=1, keepdims=True))
            alpha = jnp.exp2(m_prev - m_new)
            p = jnp.exp2(s - jnp.tile(m_new, (1, tk // LANES)))
            acc_scr[mp] = acc_scr[mp] * jnp.tile(alpha, (1, 2)) + _dot(p.astype(BF16), vb)
            m_scr[mp] = m_new
        return carry

    lax.fori_loop(0, seq // tk, body, 0, unroll=True)

    o1 = acc_scr[0, :, :DA_PAIR] / acc_scr[0, :, DA_PAIR:]
    o2 = acc_scr[1, :, :DA_PAIR] / acc_scr[1, :, DA_PAIR:]
    lam = (jnp.exp(jnp.sum(lq1_ref[...] * lk1_ref[...], axis=1, keepdims=True))
           - jnp.exp(jnp.sum(lq2_ref[...] * lk2_ref[...], axis=1, keepdims=True)) + lambda_init)
    o = o1 - lam * o2
    o_ref[...] = ((_rms(o) * gs_ref[...]) * (1.0 - lambda_init)).astype(BF16)


def _diffattn(za, seq, lam_q1, lam_k1, lam_q2, lam_k2, g_subln, lambda_init):
    m = za.shape[0]
    b = m // seq
    zr = za.reshape(b, seq, za.shape[1])
    tq = min(512, seq)
    tk = min(512, seq)
    ct = COL_TILE // DA_PAIR
    lam_spec = pl.BlockSpec((1, DA_HEAD_DIM), lambda bi, h, qi: (0, 0))
    return pl.pallas_call(
        functools.partial(_diffattn_kernel, tk=tk, lambda_init=lambda_init),
        grid=(b, DA_HEADS, seq // tq),
        in_specs=[pl.BlockSpec((None, tq, DA_PAIR), lambda bi, h, qi: (bi, qi, CT_QA * ct + h)),
                  pl.BlockSpec((None, seq, DA_PAIR), lambda bi, h, qi: (bi, 0, CT_KA * ct + h)),
                  pl.BlockSpec((None, seq, DA_PAIR), lambda bi, h, qi: (bi, 0, CT_VA * ct + h)),
                  lam_spec, lam_spec, lam_spec, lam_spec,
                  pl.BlockSpec((1, DA_PAIR), lambda bi, h, qi: (0, 0))],
        out_specs=pl.BlockSpec((None, tq, DA_PAIR), lambda bi, h, qi: (bi, qi, h)),
        out_shape=jax.ShapeDtypeStruct((b, seq, DA_HEADS * DA_PAIR), BF16),
        scratch_shapes=[pltpu.VMEM((seq, 2 * DA_PAIR), BF16),
                        pltpu.VMEM((2, tq, 2 * DA_PAIR), F32),
                        pltpu.VMEM((2, tq, LANES), F32)],
        compiler_params=_cparams(("parallel", "parallel", "arbitrary")),
        name="diffattn",
    )(zr, zr, zr, lam_q1.reshape(1, -1), lam_k1.reshape(1, -1), lam_q2.reshape(1, -1),
      lam_k2.reshape(1, -1), g_subln.reshape(1, -1))


def _dilated_kernel(q_ref, k_ref, v_ref, o_ref, lse_ref, *, kw, radius):
    n = pl.program_id(2)
    tq = q_ref.shape[0]
    t = k_ref.shape[0]
    q0 = n * tq
    ks = pl.multiple_of(jnp.clip(q0 - radius, 0, t - kw), DL_BLOCK)
    qpos = lax.broadcasted_iota(jnp.int32, (tq, kw), 0) + q0
    kpos = lax.broadcasted_iota(jnp.int32, (tq, kw), 1) + ks
    band = jnp.abs(kpos - qpos) <= radius
    lane = lax.broadcasted_iota(jnp.int32, (tq, LANES), 1)
    low = lane < DL_HEAD_DIM
    zero = jnp.zeros((tq, LANES), BF16)
    ones = jnp.ones((kw, LANES), BF16)
    lse_tile = jnp.zeros((tq, LANES), F32)
    for j in range(DL_W // LANES):
        cols = slice(j * LANES, (j + 1) * LANES)
        qp = q_ref[:, cols]
        kp = k_ref[pl.ds(ks, kw), cols]
        vp = jnp.concatenate([v_ref[pl.ds(ks, kw), cols], ones], axis=1)
        outs = []
        for par in range(2):
            qm = jnp.where(low, qp, zero) if par == 0 else jnp.where(low, zero, qp)
            s = jnp.where(band, _nt_dot(qm, kp), NEG)
            m = jnp.max(s, axis=1, keepdims=True)
            p = jnp.exp2(s - m)
            r = _dot(p.astype(BF16), vp)
            den = r[:, LANES:]
            outs.append(r[:, :LANES] / den)
            lse_tile = jnp.where(lane == 2 * j + par, m * LN2 + jnp.log(den), lse_tile)
        o_ref[:, cols] = jnp.where(low, outs[0], outs[1]).astype(BF16)
    lse_ref[...] = lse_tile


def _dilated(zb, g, window, dil):
    b, _, t, _ = zb.shape
    radius = window // (2 * dil)
    tq = min(256, t)
    kw = min(tq + 2 * radius, t)
    return pl.pallas_call(
        functools.partial(_dilated_kernel, kw=kw, radius=radius),
        grid=(b, dil, t // tq),
        in_specs=[pl.BlockSpec((None, None, tq, DL_W), lambda bi, c, n: (bi, c, n, 0)),
                  pl.BlockSpec((None, None, t, DL_W), lambda bi, c, n: (bi, c, 0, 1)),
                  pl.BlockSpec((None, None, t, DL_W), lambda bi, c, n: (bi, c, 0, 2))],
        out_specs=[pl.BlockSpec((None, None, tq, DL_W), lambda bi, c, n: (bi, c, n, 0)),
                   pl.BlockSpec((None, None, tq, LANES), lambda bi, c, n: (bi, c, n, 0))],
        out_shape=[jax.ShapeDtypeStruct((b, dil, t, DL_W), BF16),
                   jax.ShapeDtypeStruct((b, dil, t, LANES), F32)],
        compiler_params=_cparams(("parallel", "parallel", "arbitrary")),
        name=f"dilated{g}",
    )(zb, zb, zb)


def _natural_order(ref, scr):
    dil, n, w = ref.shape
    slabs = []
    for cb in range(w // LANES):
        cols = slice(cb * LANES, (cb + 1) * LANES)
        if dil == 1:
            slabs.append(ref[0, :, cols].astype(F32))
            continue
        for c in range(dil):
            scr[cb, pl.ds(c, n, stride=dil), :] = ref[c, :, cols].astype(F32)
        slabs.append(scr[cb])
    return slabs


def _merge_kernel(x_ref, mod_ref, oa_ref, o0_ref, o1_ref, o2_ref, l0_ref, l1_ref, l2_ref,
                  ga_ref, gb_ref, wa_ref, wb_ref, wo_ref, ex_ref, g_ref, x1_ref, h2_ref,
                  os1_scr, os2_scr, ls1_scr, ls2_scr):
    l0, = _natural_order(l0_ref, None)
    l1, = _natural_order(l1_ref, ls1_scr)
    l2, = _natural_order(l2_ref, ls2_scr)
    o0 = _natural_order(o0_ref, None)
    o1 = _natural_order(o1_ref, os1_scr)
    o2 = _natural_order(o2_ref, os2_scr)
    mx = jnp.maximum(jnp.maximum(l0, l1), l2)
    e0, e1, e2 = jnp.exp(l0 - mx), jnp.exp(l1 - mx), jnp.exp(l2 - mx)
    inv = 1.0 / (e0 + e1 + e2)
    ob = None
    for e, o in ((e0, o0), (e1, o1), (e2, o2)):
        w = e * inv
        w_hi = w.astype(BF16)
        w_lo = (w - w_hi.astype(F32)).astype(BF16)
        wx = _dot(w_hi, ex_ref[...]) + _dot(w_lo, ex_ref[...])
        term = [wx[:, cb * LANES:(cb + 1) * LANES] * o[cb] for cb in range(len(o))]
        ob = term if ob is None else [a + t for a, t in zip(ob, term)]
    ob = jnp.concatenate(ob, axis=1)
    ya = _dot(oa_ref[...], wa_ref[...])
    yb = _dot(ob.astype(BF16), wb_ref[...])
    ga = ga_ref[...].astype(F32)
    gb = gb_ref[...].astype(F32)
    merged = ya / (1.0 + jnp.exp(-ga)) + yb / (1.0 + jnp.exp(-gb))
    y = _dot(merged.astype(BF16), wo_ref[...])
    x1 = x_ref[...] + mod_ref[2:3, :] * y
    x1_ref[...] = x1
    h2 = (_rms(x1) * g_ref[...]) * (1.0 + mod_ref[4:5, :]) + mod_ref[3:4, :]
    h2_ref[...] = h2.astype(BF16)


def _merge(x, mod, zg, oa, obs, lses, wa_bf, wb_bf, wo_bf, g_ffn):
    b, s, d = x.shape
    m = b * s
    tm = min(512, s)
    nsb = s // tm
    row = lambda w: pl.BlockSpec((tm, w), lambda i: (i, 0))
    full = lambda shape: pl.BlockSpec(shape, lambda i: (0, 0))

    def residue_major(dil, w):
        return pl.BlockSpec((None, dil, tm // dil, w), lambda i: (i // nsb, 0, i % nsb, 0))

    dils = [dil for _, dil in DL_PAIRS]
    head = jnp.arange(LANES)[:, None]
    col = jnp.arange(DL_W)[None, :]
    expand = ((col // DL_HEAD_DIM) == head).astype(BF16)
    x1, h2 = pl.pallas_call(
        _merge_kernel,
        grid=(m // tm,),
        in_specs=[row(d), pl.BlockSpec((None, 6, d), lambda i: (i // nsb, 0, 0)),
                  row(DA_HEADS * DA_PAIR)]
                 + [residue_major(dil, DL_W) for dil in dils]
                 + [residue_major(dil, LANES) for dil in dils]
                 + [pl.BlockSpec((tm, d), lambda i: (i, 0)), pl.BlockSpec((tm, d), lambda i: (i, 1)),
                    full(wa_bf.shape), full(wb_bf.shape), full(wo_bf.shape), full(expand.shape),
                    full((1, d))],
        out_specs=[row(d), row(d)],
        out_shape=[jax.ShapeDtypeStruct((m, d), F32), jax.ShapeDtypeStruct((m, d), BF16)],
        scratch_shapes=[pltpu.VMEM((DL_W // LANES, tm, LANES), F32),
                        pltpu.VMEM((DL_W // LANES, tm, LANES), F32),
                        pltpu.VMEM((1, tm, LANES), F32), pltpu.VMEM((1, tm, LANES), F32)],
        compiler_params=_cparams(("parallel",)),
        name="merge",
    )(x.reshape(m, d), mod, oa.reshape(m, -1), *obs, *lses, zg, zg, wa_bf, wb_bf, wo_bf, expand,
      g_ffn.reshape(1, d))
    return x1, h2


def _ffn_kernel(h_ref, x1_ref, mod_ref, wg_ref, wu_ref, wd_ref, gf_ref, o_ref, *, chunk, final_norm):
    h = h_ref[...]
    hidden = wd_ref.shape[0]
    acc = None
    for c in range(hidden // chunk):
        cols = slice(c * chunk, (c + 1) * chunk)
        gate = _dot(h, wg_ref[:, cols])
        up = _dot(h, wu_ref[:, cols])
        f = (gate / (1.0 + jnp.exp(-gate))) * up
        part = _dot(f.astype(BF16), wd_ref[cols, :])
        acc = part if acc is None else acc + part
    x2 = x1_ref[...] + mod_ref[5:6, :] * acc
    o_ref[...] = _rms(x2) * gf_ref[...] if final_norm else x2


def _ffn(h2, x1, mod, s, wg_bf, wu_bf, wd_bf, g_final, final_norm):
    m, d = x1.shape
    tm = min(512, s)
    nsb = s // tm
    hidden = wd_bf.shape[0]
    chunk = hidden // 2
    assert chunk % LANES == 0
    row = pl.BlockSpec((tm, d), lambda i: (i, 0))
    full = lambda shape: pl.BlockSpec(shape, lambda i: (0, 0))
    return pl.pallas_call(
        functools.partial(_ffn_kernel, chunk=chunk, final_norm=final_norm),
        grid=(m // tm,),
        in_specs=[row, row, pl.BlockSpec((None, 6, d), lambda i: (i // nsb, 0, 0)),
                  full(wg_bf.shape), full(wu_bf.shape), full(wd_bf.shape), full((1, d))],
        out_specs=row,
        out_shape=jax.ShapeDtypeStruct((m, d), F32),
        compiler_params=_cparams(("parallel",)),
        name="ffn",
    )(h2, x1, mod, wg_bf, wu_bf, wd_bf, g_final.reshape(1, d))


def _trunk(x, mod_all, weights):
    (g_mix, w_in, lam_q1, lam_k1, lam_q2, lam_k2, g_subln, w_a, w_b, w_out, g_ffn,
     w_gu, w_down, g_final) = weights
    b, s, d = x.shape
    depth = w_in.shape[0]
    hidden = w_down.shape[1]
    cs, sn = _rope_tables(s)
    for l in range(depth):
        lambda_init = 0.8 - 0.6 * math.exp(-0.3 * l)
        mod = mod_all[l]
        za, zg, *zbs = _inproj(x, mod, g_mix[l], w_in[l].astype(BF16), cs, sn)
        oa = _diffattn(za, s, lam_q1[l], lam_k1[l], lam_q2[l], lam_k2[l], g_subln[l], lambda_init)
        obs, lses = [], []
        for g, (window, dil) in enumerate(DL_PAIRS):
            o, lse = _dilated(zbs[g], g, window, dil)
            obs.append(o)
            lses.append(lse)
        x1, h2 = _merge(x, mod, zg, oa, obs, lses, w_a[l].astype(BF16), w_b[l].astype(BF16),
                        w_out[l].astype(BF16), g_ffn[l])
        wgu = w_gu[l].astype(BF16)
        x = _ffn(h2, x1, mod, s, wgu[:, :hidden], wgu[:, hidden:], w_down[l].astype(BF16),
                 g_final, l == depth - 1).reshape(b, s, d)
    return x


def kernel(x_prompt, x_sample, c_prompt, c_sample, w_ada, b_ada, g_mix, w_in, lam_q1, lam_k1,
           lam_q2, lam_k2, g_subln, w_a, w_b, w_out, g_ffn, w_gu, w_down, g_final):
    depth = w_in.shape[0]
    bp = c_prompt.shape[0]
    c_all = jnp.concatenate([c_prompt, c_sample], axis=0)
    mods = [_ada(c_all, w_ada[l], b_ada[l]).reshape(c_all.shape[0], 6, D_MODEL) for l in range(depth)]
    weights = (g_mix, w_in, lam_q1, lam_k1, lam_q2, lam_k2, g_subln, w_a, w_b, w_out, g_ffn,
               w_gu, w_down, g_final)
    y_prompt = _trunk(x_prompt, [m[:bp] for m in mods], weights)
    y_sample = _trunk(x_sample, [m[bp:] for m in mods], weights)
    return (y_prompt, y_sample)
```

```python
import functools
import math

import jax
import jax.numpy as jnp
from jax import lax
from jax.experimental import pallas as pl
from jax.experimental.pallas import tpu as pltpu

F32 = jnp.float32
BF16 = jnp.bfloat16

D_MODEL = 1024
DA_HEADS = 8
DA_HEAD_DIM = 64
DA_PAIR = 2 * DA_HEAD_DIM
DL_PAIRS = ((128, 1), (512, 4), (2048, 16))
DL_HEADS = 8
DL_HEAD_DIM = 64
DL_W = DL_HEADS * DL_HEAD_DIM
DL_BLOCK = 64
ROPE_THETA = 500000.0
ROPE_ROT = DL_HEAD_DIM // 4
ROPE_HALF = ROPE_ROT // 2
EPS = 1e-6
NEG = -1e30
LOG2E = math.log2(math.e)
LN2 = math.log(2.0)
Q_SCALE = DA_HEAD_DIM ** -0.5 * LOG2E

LANES = 128
COL_TILE = 512
N_IN = 2 * DA_HEADS * DA_PAIR + DA_HEADS * DA_PAIR + 9 * DL_W + 2 * D_MODEL
N_COL_TILES = N_IN // COL_TILE
CT_QA, CT_KA, CT_VA = 0, 2, 4
CT_DL = 6
CT_GA = 15
VMEM_LIMIT = 56 * 1024 * 1024


def _cparams(sem):
    return pltpu.CompilerParams(dimension_semantics=sem, vmem_limit_bytes=VMEM_LIMIT)


def _nt_dot(a, b):
    return lax.dot_general(a, b, (((1,), (1,)), ((), ())), preferred_element_type=F32)


def _dot(a, b):
    return jnp.dot(a, b, preferred_element_type=F32)


def _rms(x):
    return x * lax.rsqrt(jnp.mean(x * x, axis=-1, keepdims=True) + EPS)


def _ada_kernel(c_ref, w_ref, b_ref, o_ref):
    c = c_ref[...]
    a = c * (1.0 / (1.0 + jnp.exp(-c)))
    w = w_ref[...]
    a_hi = a.astype(BF16)
    a_lo = (a - a_hi.astype(F32)).astype(BF16)
    w_hi = w.astype(BF16)
    w_lo = (w - w_hi.astype(F32)).astype(BF16)
    o_ref[...] = _dot(a_hi, w_hi) + _dot(a_hi, w_lo) + _dot(a_lo, w_hi) + b_ref[...]


def _ada(c, w_ada, b_ada):
    nb, d = c.shape
    n = w_ada.shape[1]
    tn = COL_TILE
    return pl.pallas_call(
        _ada_kernel,
        grid=(n // tn,),
        in_specs=[pl.BlockSpec((nb, d), lambda j: (0, 0)),
                  pl.BlockSpec((d, tn), lambda j: (0, j)),
                  pl.BlockSpec((1, tn), lambda j: (0, j))],
        out_specs=pl.BlockSpec((nb, tn), lambda j: (0, j)),
        out_shape=jax.ShapeDtypeStruct((nb, n), F32),
        compiler_params=_cparams(("arbitrary",)),
        name="ada",
    )(c, w_ada, b_ada.reshape(1, n))


def _inproj_kernel(x_ref, mod_ref, g_ref, w_ref, cs0_ref, sn0_ref, cs1_ref, sn1_ref, cs2_ref, sn2_ref,
                   qa_ref, ka_ref, va_ref, zg_ref, zb0_ref, zb1_ref, zb2_ref,
                   h0_scr, h1_scr, h2_scr, hs_scr, *, row_chunk):
    j = pl.program_id(1)
    tm, d = x_ref.shape
    h_scrs = (h0_scr, h1_scr, h2_scr)
    tabs = ((cs0_ref, sn0_ref), (cs1_ref, sn1_ref), (cs2_ref, sn2_ref))
    dils = tuple(dil for _, dil in DL_PAIRS)

    @pl.when(j == 0)
    def _():
        y = _rms(x_ref[...]) * g_ref[...]
        h = y * (1.0 + mod_ref[1:2, :]) + mod_ref[0:1, :]
        h0_scr[...] = h.astype(BF16)
        for cb in range(d // LANES):
            hs_scr[cb] = h[:, cb * LANES:(cb + 1) * LANES]
        for g in range(1, len(dils)):
            n = tm // dils[g]
            for c in range(dils[g]):
                for cb in range(d // LANES):
                    h_scrs[g][c * n:(c + 1) * n, cb * LANES:(cb + 1) * LANES] = (
                        hs_scr[cb, pl.ds(c, n, stride=dils[g]), :].astype(BF16))

    def project(h_scr, tab, scale, emit):
        if tab is not None:
            lane = lax.broadcasted_iota(jnp.int32, (row_chunk, LANES), 1)
            first_half = lax.rem(lane, DL_HEAD_DIM) < ROPE_HALF
        for r in range(tm // row_chunk):
            rows = pl.ds(r * row_chunk, row_chunk)
            z = _dot(h_scr[rows, :], w_ref[...])
            if tab is not None:
                cs, sn = scale(tab[0][rows, :], tab[1][rows, :])
                slabs = []
                for cb in range(COL_TILE // LANES):
                    zc = z[:, cb * LANES:(cb + 1) * LANES]
                    partner = jnp.where(first_half, pltpu.roll(zc, LANES - ROPE_HALF, 1),
                                        pltpu.roll(zc, ROPE_HALF, 1))
                    slabs.append(zc * cs + partner * sn)
                z = jnp.concatenate(slabs, axis=1)
            emit(r * row_chunk, z.astype(BF16))

    def head_major(dst_ref):
        def emit(row0, z):
            for hh in range(COL_TILE // DA_PAIR):
                dst_ref[hh, row0:row0 + row_chunk, :] = z[:, hh * DA_PAIR:(hh + 1) * DA_PAIR]
        return emit

    @pl.when(j < CT_KA)
    def _():
        project(h0_scr, tabs[0], lambda cs, sn: (cs * Q_SCALE, sn * Q_SCALE), head_major(qa_ref))

    @pl.when((j >= CT_KA) & (j < CT_VA))
    def _():
        project(h0_scr, tabs[0], lambda cs, sn: (cs, sn), head_major(ka_ref))

    @pl.when((j >= CT_VA) & (j < CT_DL))
    def _():
        project(h0_scr, None, None, head_major(va_ref))

    @pl.when(j >= CT_GA)
    def _():
        def emit(row0, z):
            zg_ref[row0:row0 + row_chunk, :] = z
        project(h0_scr, None, None, emit)

    for g, zb_ref in enumerate((zb0_ref, zb1_ref, zb2_ref)):
        @pl.when((j >= CT_DL + 3 * g) & (j < CT_DL + 3 * g + 3))
        def _(g=g, zb_ref=zb_ref):
            kind = j - (CT_DL + 3 * g)
            n = tm // dils[g]
            piece = min(n, row_chunk)

            def scale(cs, sn):
                f = jnp.where(kind == 0, Q_SCALE, 1.0).astype(F32)
                return jnp.where(kind < 2, cs * f, 1.0), jnp.where(kind < 2, sn * f, 0.0)

            def emit(row0, z):
                for p in range(row_chunk // piece):
                    r0 = row0 + p * piece
                    zb_ref[r0 // n, r0 % n:r0 % n + piece, :] = z[p * piece:(p + 1) * piece, :]
            project(h_scrs[g], tabs[g], scale, emit)


def _rope_tables(seq):
    inv = 1.0 / (ROPE_THETA ** (jnp.arange(0, ROPE_ROT, 2, dtype=F32) / ROPE_ROT))
    ang = jnp.arange(seq, dtype=F32)[:, None] * inv[None, :]
    cos, sin = jnp.cos(ang), jnp.sin(ang)
    ones = jnp.ones((seq, DL_HEAD_DIM - ROPE_ROT), F32)
    zeros = jnp.zeros((seq, DL_HEAD_DIM - ROPE_ROT), F32)
    cs = jnp.concatenate([cos, cos, ones], axis=1)
    sn = jnp.concatenate([-sin, sin, zeros], axis=1)
    reps = LANES // DL_HEAD_DIM
    return jnp.tile(cs, (1, reps)), jnp.tile(sn, (1, reps))


def _residue_major_rows(tab, tm, dil):
    s, w = tab.shape
    return tab.reshape(s // tm, tm // dil, dil, w).transpose(0, 2, 1, 3).reshape(s, w)


def _inproj(x, mod, g_mix, w_in_bf):
    b, s, d = x.shape
    tm = min(1024, s)
    nsb = s // tm
    m = b * s
    cs, sn = _rope_tables(s)
    tables = []
    for _, dil in DL_PAIRS:
        assert s % (dil * DL_BLOCK) == 0, "sequence must tile into dilated blocks without padding"
        tables += [_residue_major_rows(cs, tm, dil), _residue_major_rows(sn, tm, dil)]
    heads_per_tile = COL_TILE // DA_PAIR
    head_shape = jax.ShapeDtypeStruct((b, DA_HEADS, s, DA_PAIR), BF16)

    def head_spec(first_tile):
        return pl.BlockSpec((None, heads_per_tile, tm, DA_PAIR),
                            lambda i, j: (i // nsb, jnp.clip(j - first_tile, 0, 1), i % nsb, 0))

    out_shape = [head_shape, head_shape, head_shape,
                 jax.ShapeDtypeStruct((m, (N_COL_TILES - CT_GA) * COL_TILE), BF16)]
    out_specs = [head_spec(CT_QA), head_spec(CT_KA), head_spec(CT_VA),
                 pl.BlockSpec((tm, COL_TILE), lambda i, j: (i, jnp.maximum(j - CT_GA, 0)))]
    for g, (_, dil) in enumerate(DL_PAIRS):
        out_shape.append(jax.ShapeDtypeStruct((b, dil, s // dil, 3 * DL_W), BF16))
        out_specs.append(pl.BlockSpec(
            (None, dil, tm // dil, COL_TILE),
            lambda i, j, g=g: (i // nsb, 0, i % nsb, jnp.clip(j - (CT_DL + 3 * g), 0, 2))))
    table_spec = pl.BlockSpec((tm, LANES), lambda i, j: (i % nsb, 0))
    return pl.pallas_call(
        functools.partial(_inproj_kernel, row_chunk=min(256, tm)),
        grid=(b * nsb, N_COL_TILES),
        in_specs=[pl.BlockSpec((tm, d), lambda i, j: (i, 0)),
                  pl.BlockSpec((None, 6, d), lambda i, j: (i // nsb, 0, 0)),
                  pl.BlockSpec((1, d), lambda i, j: (0, 0)),
                  pl.BlockSpec((d, COL_TILE), lambda i, j: (0, j))] + [table_spec] * 6,
        out_specs=out_specs,
        out_shape=out_shape,
        scratch_shapes=[pltpu.VMEM((tm, d), BF16), pltpu.VMEM((tm, d), BF16), pltpu.VMEM((tm, d), BF16),
                        pltpu.VMEM((d // LANES, tm, LANES), F32)],
        compiler_params=_cparams(("parallel", "arbitrary")),
        name="inproj",
    )(x.reshape(m, d), mod, g_mix.reshape(1, d), w_in_bf, *tables)


def _diffattn_kernel(q_ref, k_ref, v_ref, lq1_ref, lk1_ref, lq2_ref, lk2_ref, gs_ref, o_ref,
                     v2_scr, *, tk, lambda_init):
    qi = pl.program_id(2)
    tq = q_ref.shape[0]
    seq = k_ref.shape[0]

    @pl.when(qi == 0)
    def _():
        v2_scr[:, :DA_PAIR] = v_ref[...]
        v2_scr[:, DA_PAIR:] = jnp.ones((seq, DA_PAIR), BF16)

    lane = lax.broadcasted_iota(jnp.int32, (tq, DA_PAIR), 1)
    q = q_ref[...]
    zero = jnp.zeros_like(q)
    qmaps = (jnp.where(lane < DA_HEAD_DIM, q, zero), jnp.where(lane >= DA_HEAD_DIM, q, zero))
    m = [None, None]
    acc = [None, None]
    for kk in range(seq // tk):
        kb = k_ref[kk * tk:(kk + 1) * tk, :]
        vb = v2_scr[kk * tk:(kk + 1) * tk, :]
        for mp in range(2):
            s = _nt_dot(qmaps[mp], kb)
            m_cur = jnp.max(s, axis=1, keepdims=True)
            if kk == 0:
                m_new = jnp.broadcast_to(m_cur, (tq, LANES))
            else:
                m_new = jnp.maximum(m[mp], m_cur)
            p = jnp.exp2(s - jnp.tile(m_new, (1, tk // LANES)))
            pv = _dot(p.astype(BF16), vb)
            if kk == 0:
                acc[mp] = pv
            else:
                alpha = jnp.exp2(m[mp] - m_new)
                acc[mp] = acc[mp] * jnp.tile(alpha, (1, 2)) + pv
            m[mp] = m_new

    o1 = acc[0][:, :DA_PAIR] / acc[0][:, DA_PAIR:]
    o2 = acc[1][:, :DA_PAIR] / acc[1][:, DA_PAIR:]
    lam = (jnp.exp(jnp.sum(lq1_ref[...] * lk1_ref[...], axis=1, keepdims=True))
           - jnp.exp(jnp.sum(lq2_ref[...] * lk2_ref[...], axis=1, keepdims=True)) + lambda_init)
    o = o1 - lam * o2
    o_ref[...] = ((_rms(o) * gs_ref[...]) * (1.0 - lambda_init)).astype(BF16)


def _diffattn(qa, ka, va, lam_q1, lam_k1, lam_q2, lam_k2, g_subln, lambda_init):
    b, nh, seq, _ = qa.shape
    tq = min(512, seq)
    tk = min(512, seq)
    lam_spec = pl.BlockSpec((1, DA_HEAD_DIM), lambda bi, h, qi: (0, 0))
    kv_spec = pl.BlockSpec((None, None, seq, DA_PAIR), lambda bi, h, qi: (bi, h, 0, 0))
    return pl.pallas_call(
        functools.partial(_diffattn_kernel, tk=tk, lambda_init=lambda_init),
        grid=(b, nh, seq // tq),
        in_specs=[pl.BlockSpec((None, None, tq, DA_PAIR), lambda bi, h, qi: (bi, h, qi, 0)),
                  kv_spec, kv_spec, lam_spec, lam_spec, lam_spec, lam_spec,
                  pl.BlockSpec((1, DA_PAIR), lambda bi, h, qi: (0, 0))],
        out_specs=pl.BlockSpec((None, tq, DA_PAIR), lambda bi, h, qi: (bi, qi, h)),
        out_shape=jax.ShapeDtypeStruct((b, seq, nh * DA_PAIR), BF16),
        scratch_shapes=[pltpu.VMEM((seq, 2 * DA_PAIR), BF16)],
        compiler_params=_cparams(("parallel", "parallel", "arbitrary")),
        name="diffattn",
    )(qa, ka, va, lam_q1.reshape(1, -1), lam_k1.reshape(1, -1), lam_q2.reshape(1, -1),
      lam_k2.reshape(1, -1), g_subln.reshape(1, -1))


def _dilated_kernel(q_ref, k_ref, v_ref, o_ref, lse_ref, *, sub, kw, radius):
    n = pl.program_id(2)
    tq = q_ref.shape[0]
    t = k_ref.shape[0]
    lane = lax.broadcasted_iota(jnp.int32, (sub, LANES), 1)
    low = lane < DL_HEAD_DIM
    zero = jnp.zeros((sub, LANES), BF16)
    ones = jnp.ones((kw, LANES), BF16)
    o_rows, lse_rows = [], []
    for sb in range(tq // sub):
        q0 = n * tq + sb * sub
        ks = pl.multiple_of(jnp.clip(q0 - radius, 0, t - kw), DL_BLOCK)
        qpos = lax.broadcasted_iota(jnp.int32, (sub, kw), 0) + q0
        kpos = lax.broadcasted_iota(jnp.int32, (sub, kw), 1) + ks
        band = jnp.abs(kpos - qpos) <= radius
        qrows = slice(sb * sub, (sb + 1) * sub)
        scores = []
        for j in range(DL_W // LANES):
            cols = slice(j * LANES, (j + 1) * LANES)
            qp = q_ref[qrows, cols]
            kp = k_ref[pl.ds(ks, kw), cols]
            for par in range(2):
                qm = jnp.where(low, qp, zero) if par == 0 else jnp.where(low, zero, qp)
                scores.append(_nt_dot(qm, kp))
        lses, o_cols = [], []
        for j in range(DL_W // LANES):
            cols = slice(j * LANES, (j + 1) * LANES)
            vp = jnp.concatenate([v_ref[pl.ds(ks, kw), cols], ones], axis=1)
            outs = []
            for par in range(2):
                s = jnp.where(band, scores[2 * j + par], NEG)
                m = jnp.max(s, axis=1, keepdims=True)
                p = jnp.exp2(s - m)
                r = _dot(p.astype(BF16), vp)
                den = r[:, LANES:]
                outs.append(r[:, :LANES] / den)
                lses.append(m * LN2 + jnp.log(den))
            o_cols.append(jnp.where(low, outs[0], outs[1]).astype(BF16))
        lse_tile = jnp.zeros((sub, LANES), F32)
        for hd, lse in enumerate(lses):
            lse_tile = jnp.where(lane == hd, lse, lse_tile)
        o_rows.append(jnp.concatenate(o_cols, axis=1))
        lse_rows.append(lse_tile)
    o_ref[...] = jnp.concatenate(o_rows, axis=0)
    lse_ref[...] = jnp.concatenate(lse_rows, axis=0)


def _dilated(zb, g, window, dil):
    b, _, t, _ = zb.shape
    radius = window // (2 * dil)
    tq = min(256, t)
    sub = min(2 * radius, tq)
    kw = min(sub + 2 * radius, t)
    return pl.pallas_call(
        functools.partial(_dilated_kernel, sub=sub, kw=kw, radius=radius),
        grid=(b, dil, t // tq),
        in_specs=[pl.BlockSpec((None, None, tq, DL_W), lambda bi, c, n: (bi, c, n, 0)),
                  pl.BlockSpec((None, None, t, DL_W), lambda bi, c, n: (bi, c, 0, 1)),
                  pl.BlockSpec((None, None, t, DL_W), lambda bi, c, n: (bi, c, 0, 2))],
        out_specs=[pl.BlockSpec((None, None, tq, DL_W), lambda bi, c, n: (bi, c, n, 0)),
                   pl.BlockSpec((None, None, tq, LANES), lambda bi, c, n: (bi, c, n, 0))],
        out_shape=[jax.ShapeDtypeStruct((b, dil, t, DL_W), BF16),
                   jax.ShapeDtypeStruct((b, dil, t, LANES), F32)],
        compiler_params=_cparams(("parallel", "parallel", "arbitrary")),
        name=f"dilated{g}",
    )(zb, zb, zb)


def _natural_order(ref, scr):
    dil, n, w = ref.shape
    slabs = []
    for cb in range(w // LANES):
        cols = slice(cb * LANES, (cb + 1) * LANES)
        if dil == 1:
            slabs.append(ref[0, :, cols].astype(F32))
            continue
        for c in range(dil):
            scr[cb, pl.ds(c, n, stride=dil), :] = ref[c, :, cols].astype(F32)
        slabs.append(scr[cb])
    return slabs


def _merge_kernel(x_ref, mod_ref, oa_ref, o0_ref, o1_ref, o2_ref, l0_ref, l1_ref, l2_ref,
                  ga_ref, gb_ref, wa_ref, wb_ref, wo_ref, ex_ref, g_ref, x1_ref, h2_ref,
                  os1_scr, os2_scr, ls1_scr, ls2_scr):
    l0, = _natural_order(l0_ref, None)
    l1, = _natural_order(l1_ref, ls1_scr)
    l2, = _natural_order(l2_ref, ls2_scr)
    o0 = _natural_order(o0_ref, None)
    o1 = _natural_order(o1_ref, os1_scr)
    o2 = _natural_order(o2_ref, os2_scr)
    mx = jnp.maximum(jnp.maximum(l0, l1), l2)
    e0, e1, e2 = jnp.exp(l0 - mx), jnp.exp(l1 - mx), jnp.exp(l2 - mx)
    inv = 1.0 / (e0 + e1 + e2)
    ob = None
    for e, o in ((e0, o0), (e1, o1), (e2, o2)):
        w = e * inv
        w_hi = w.astype(BF16)
        w_lo = (w - w_hi.astype(F32)).astype(BF16)
        wx = _dot(w_hi, ex_ref[...]) + _dot(w_lo, ex_ref[...])
        term = [wx[:, cb * LANES:(cb + 1) * LANES] * o[cb] for cb in range(len(o))]
        ob = term if ob is None else [a + t for a, t in zip(ob, term)]
    ob = jnp.concatenate(ob, axis=1)
    ya = _dot(oa_ref[...], wa_ref[...])
    yb = _dot(ob.astype(BF16), wb_ref[...])
    ga = ga_ref[...].astype(F32)
    gb = gb_ref[...].astype(F32)
    merged = ya / (1.0 + jnp.exp(-ga)) + yb / (1.0 + jnp.exp(-gb))
    y = _dot(merged.astype(BF16), wo_ref[...])
    x1 = x_ref[...] + mod_ref[2:3, :] * y
    x1_ref[...] = x1
    h2 = (_rms(x1) * g_ref[...]) * (1.0 + mod_ref[4:5, :]) + mod_ref[3:4, :]
    h2_ref[...] = h2.astype(BF16)


def _merge(x, mod, zg, oa, obs, lses, wa_bf, wb_bf, wo_bf, g_ffn):
    b, s, d = x.shape
    m = b * s
    tm = min(512, s)
    nsb = s // tm
    row = lambda w: pl.BlockSpec((tm, w), lambda i: (i, 0))
    full = lambda shape: pl.BlockSpec(shape, lambda i: (0, 0))

    def residue_major(dil, w):
        return pl.BlockSpec((None, dil, tm // dil, w), lambda i: (i // nsb, 0, i % nsb, 0))

    dils = [dil for _, dil in DL_PAIRS]
    head = jnp.arange(LANES)[:, None]
    col = jnp.arange(DL_W)[None, :]
    expand = ((col // DL_HEAD_DIM) == head).astype(BF16)
    x1, h2 = pl.pallas_call(
        _merge_kernel,
        grid=(m // tm,),
        in_specs=[row(d), pl.BlockSpec((None, 6, d), lambda i: (i // nsb, 0, 0)),
                  row(DA_HEADS * DA_PAIR)]
                 + [residue_major(dil, DL_W) for dil in dils]
                 + [residue_major(dil, LANES) for dil in dils]
                 + [pl.BlockSpec((tm, d), lambda i: (i, 0)), pl.BlockSpec((tm, d), lambda i: (i, 1)),
                    full(wa_bf.shape), full(wb_bf.shape), full(wo_bf.shape), full(expand.shape),
                    full((1, d))],
        out_specs=[row(d), row(d)],
        out_shape=[jax.ShapeDtypeStruct((m, d), F32), jax.ShapeDtypeStruct((m, d), BF16)],
        scratch_shapes=[pltpu.VMEM((DL_W // LANES, tm, LANES), F32),
                        pltpu.VMEM((DL_W // LANES, tm, LANES), F32),
                        pltpu.VMEM((1, tm, LANES), F32), pltpu.VMEM((1, tm, LANES), F32)],
        compiler_params=_cparams(("parallel",)),
        name="merge",
    )(x.reshape(m, d), mod, oa.reshape(m, -1), *obs, *lses, zg, zg, wa_bf, wb_bf, wo_bf, expand,
      g_ffn.reshape(1, d))
    return x1, h2


def _ffn_kernel(h_ref, x1_ref, mod_ref, wg_ref, wu_ref, wd_ref, gf_ref, o_ref, *, chunk, final_norm):
    h = h_ref[...]
    hidden = wd_ref.shape[0]
    acc = None
    for c in range(hidden // chunk):
        cols = slice(c * chunk, (c + 1) * chunk)
        gate = _dot(h, wg_ref[:, cols])
        up = _dot(h, wu_ref[:, cols])
        f = (gate / (1.0 + jnp.exp(-gate))) * up
        part = _dot(f.astype(BF16), wd_ref[cols, :])
        acc = part if acc is None else acc + part
    x2 = x1_ref[...] + mod_ref[5:6, :] * acc
    o_ref[...] = _rms(x2) * gf_ref[...] if final_norm else x2


def _ffn(h2, x1, mod, s, wg_bf, wu_bf, wd_bf, g_final, final_norm):
    m, d = x1.shape
    tm = min(512, s)
    nsb = s // tm
    hidden = wd_bf.shape[0]
    chunk = hidden // 2
    assert chunk % LANES == 0
    row = pl.BlockSpec((tm, d), lambda i: (i, 0))
    full = lambda shape: pl.BlockSpec(shape, lambda i: (0, 0))
    return pl.pallas_call(
        functools.partial(_ffn_kernel, chunk=chunk, final_norm=final_norm),
        grid=(m // tm,),
        in_specs=[row, row, pl.BlockSpec((None, 6, d), lambda i: (i // nsb, 0, 0)),
                  full(wg_bf.shape), full(wu_bf.shape), full(wd_bf.shape), full((1, d))],
        out_specs=row,
        out_shape=jax.ShapeDtypeStruct((m, d), F32),
        compiler_params=_cparams(("parallel",)),
        name="ffn",
    )(h2, x1, mod, wg_bf, wu_bf, wd_bf, g_final.reshape(1, d))


def _trunk(x, mod_all, weights):
    (g_mix, w_in, lam_q1, lam_k1, lam_q2, lam_k2, g_subln, w_a, w_b, w_out, g_ffn,
     w_gu, w_down, g_final) = weights
    b, s, d = x.shape
    depth = w_in.shape[0]
    hidden = w_down.shape[1]
    for l in range(depth):
        lambda_init = 0.8 - 0.6 * math.exp(-0.3 * l)
        mod = mod_all[l]
        qa, ka, va, zg, *zbs = _inproj(x, mod, g_mix[l], w_in[l].astype(BF16))
        oa = _diffattn(qa, ka, va, lam_q1[l], lam_k1[l], lam_q2[l], lam_k2[l], g_subln[l], lambda_init)
        obs, lses = [], []
        for g, (window, dil) in enumerate(DL_PAIRS):
            o, lse = _dilated(zbs[g], g, window, dil)
            obs.append(o)
            lses.append(lse)
        x1, h2 = _merge(x, mod, zg, oa, obs, lses, w_a[l].astype(BF16), w_b[l].astype(BF16),
                        w_out[l].astype(BF16), g_ffn[l])
        wgu = w_gu[l].astype(BF16)
        x = _ffn(h2, x1, mod, s, wgu[:, :hidden], wgu[:, hidden:], w_down[l].astype(BF16),
                 g_final, l == depth - 1).reshape(b, s, d)
    return x


def kernel(x_prompt, x_sample, c_prompt, c_sample, w_ada, b_ada, g_mix, w_in, lam_q1, lam_k1,
           lam_q2, lam_k2, g_subln, w_a, w_b, w_out, g_ffn, w_gu, w_down, g_final):
    depth = w_in.shape[0]
    bp = c_prompt.shape[0]
    c_all = jnp.concatenate([c_prompt, c_sample], axis=0)
    mods = [_ada(c_all, w_ada[l], b_ada[l]).reshape(c_all.shape[0], 6, D_MODEL) for l in range(depth)]
    weights = (g_mix, w_in, lam_q1, lam_k1, lam_q2, lam_k2, g_subln, w_a, w_b, w_out, g_ffn,
               w_gu, w_down, g_final)
    y_prompt = _trunk(x_prompt, [m[:bp] for m in mods], weights)
    y_sample = _trunk(x_sample, [m[bp:] for m in mods], weights)
    return (y_prompt, y_sample)
```

```python
import functools
import math

import jax
import jax.numpy as jnp
from jax import lax
from jax.experimental import pallas as pl
from jax.experimental.pallas import tpu as pltpu

F32 = jnp.float32
BF16 = jnp.bfloat16

D_MODEL = 1024
DA_HEADS = 8
DA_HEAD_DIM = 64
DA_PAIR = 2 * DA_HEAD_DIM
DL_PAIRS = ((128, 1), (512, 4), (2048, 16))
DL_HEADS = 8
DL_HEAD_DIM = 64
DL_W = DL_HEADS * DL_HEAD_DIM
DL_BLOCK = 64
ROPE_THETA = 500000.0
ROPE_ROT = DL_HEAD_DIM // 4
ROPE_HALF = ROPE_ROT // 2
EPS = 1e-6
NEG = -1e30
LOG2E = math.log2(math.e)
LN2 = math.log(2.0)
Q_SCALE = DA_HEAD_DIM ** -0.5 * LOG2E

LANES = 128
COL_TILE = 512
N_IN = 2 * DA_HEADS * DA_PAIR + DA_HEADS * DA_PAIR + 9 * DL_W + 2 * D_MODEL
N_COL_TILES = N_IN // COL_TILE
CT_QA, CT_KA, CT_VA = 0, 2, 4
CT_DL = 6
CT_GA = 15
VMEM_LIMIT = 56 * 1024 * 1024


def _cparams(sem):
    return pltpu.CompilerParams(dimension_semantics=sem, vmem_limit_bytes=VMEM_LIMIT)


def _nt_dot(a, b):
    return lax.dot_general(a, b, (((1,), (1,)), ((), ())), preferred_element_type=F32)


def _dot(a, b):
    return jnp.dot(a, b, preferred_element_type=F32)


def _rms(x):
    return x * lax.rsqrt(jnp.mean(x * x, axis=-1, keepdims=True) + EPS)


def _ada_kernel(c_ref, w_ref, b_ref, o_ref):
    c = c_ref[...]
    a = c * (1.0 / (1.0 + jnp.exp(-c)))
    w = w_ref[...]
    a_hi = a.astype(BF16)
    a_lo = (a - a_hi.astype(F32)).astype(BF16)
    w_hi = w.astype(BF16)
    w_lo = (w - w_hi.astype(F32)).astype(BF16)
    o_ref[...] = _dot(a_hi, w_hi) + _dot(a_hi, w_lo) + _dot(a_lo, w_hi) + b_ref[...]


def _ada(c, w_ada, b_ada):
    nb, d = c.shape
    n = w_ada.shape[1]
    tn = COL_TILE
    return pl.pallas_call(
        _ada_kernel,
        grid=(n // tn,),
        in_specs=[pl.BlockSpec((nb, d), lambda j: (0, 0)),
                  pl.BlockSpec((d, tn), lambda j: (0, j)),
                  pl.BlockSpec((1, tn), lambda j: (0, j))],
        out_specs=pl.BlockSpec((nb, tn), lambda j: (0, j)),
        out_shape=jax.ShapeDtypeStruct((nb, n), F32),
        compiler_params=_cparams(("arbitrary",)),
        name="ada",
    )(c, w_ada, b_ada.reshape(1, n))


def _inproj_kernel(x_ref, mod_ref, g_ref, w_ref, cs0_ref, sn0_ref, cs1_ref, sn1_ref, cs2_ref, sn2_ref,
                   qa_ref, ka_ref, va_ref, zg_ref, zb0_ref, zb1_ref, zb2_ref,
                   h0_scr, h1_scr, h2_scr, hs_scr, *, row_chunk):
    j = pl.program_id(1)
    tm, d = x_ref.shape
    h_scrs = (h0_scr, h1_scr, h2_scr)
    tabs = ((cs0_ref, sn0_ref), (cs1_ref, sn1_ref), (cs2_ref, sn2_ref))
    dils = tuple(dil for _, dil in DL_PAIRS)

    @pl.when(j == 0)
    def _():
        y = _rms(x_ref[...]) * g_ref[...]
        h = y * (1.0 + mod_ref[1:2, :]) + mod_ref[0:1, :]
        h0_scr[...] = h.astype(BF16)
        for cb in range(d // LANES):
            hs_scr[cb] = h[:, cb * LANES:(cb + 1) * LANES]
        for g in range(1, len(dils)):
            n = tm // dils[g]
            for c in range(dils[g]):
                for cb in range(d // LANES):
                    h_scrs[g][c * n:(c + 1) * n, cb * LANES:(cb + 1) * LANES] = (
                        hs_scr[cb, pl.ds(c, n, stride=dils[g]), :].astype(BF16))

    def project(h_scr, tab, scale, emit):
        if tab is not None:
            lane = lax.broadcasted_iota(jnp.int32, (row_chunk, LANES), 1)
            first_half = lax.rem(lane, DL_HEAD_DIM) < ROPE_HALF
        for r in range(tm // row_chunk):
            rows = pl.ds(r * row_chunk, row_chunk)
            z = _dot(h_scr[rows, :], w_ref[...])
            if tab is not None:
                cs, sn = scale(tab[0][rows, :], tab[1][rows, :])
                slabs = []
                for cb in range(COL_TILE // LANES):
                    zc = z[:, cb * LANES:(cb + 1) * LANES]
                    partner = jnp.where(first_half, pltpu.roll(zc, LANES - ROPE_HALF, 1),
                                        pltpu.roll(zc, ROPE_HALF, 1))
                    slabs.append(zc * cs + partner * sn)
                z = jnp.concatenate(slabs, axis=1)
            emit(r * row_chunk, z.astype(BF16))

    def head_major(dst_ref):
        def emit(row0, z):
            for hh in range(COL_TILE // DA_PAIR):
                dst_ref[hh, row0:row0 + row_chunk, :] = z[:, hh * DA_PAIR:(hh + 1) * DA_PAIR]
        return emit

    @pl.when(j < CT_KA)
    def _():
        project(h0_scr, tabs[0], lambda cs, sn: (cs * Q_SCALE, sn * Q_SCALE), head_major(qa_ref))

    @pl.when((j >= CT_KA) & (j < CT_VA))
    def _():
        project(h0_scr, tabs[0], lambda cs, sn: (cs, sn), head_major(ka_ref))

    @pl.when((j >= CT_VA) & (j < CT_DL))
    def _():
        project(h0_scr, None, None, head_major(va_ref))

    @pl.when(j >= CT_GA)
    def _():
        def emit(row0, z):
            zg_ref[row0:row0 + row_chunk, :] = z
        project(h0_scr, None, None, emit)

    for g, zb_ref in enumerate((zb0_ref, zb1_ref, zb2_ref)):
        @pl.when((j >= CT_DL + 3 * g) & (j < CT_DL + 3 * g + 3))
        def _(g=g, zb_ref=zb_ref):
            kind = j - (CT_DL + 3 * g)
            n = tm // dils[g]
            piece = min(n, row_chunk)

            def scale(cs, sn):
                f = jnp.where(kind == 0, Q_SCALE, 1.0).astype(F32)
                return jnp.where(kind < 2, cs * f, 1.0), jnp.where(kind < 2, sn * f, 0.0)

            def emit(row0, z):
                for p in range(row_chunk // piece):
                    r0 = row0 + p * piece
                    zb_ref[r0 // n, r0 % n:r0 % n + piece, :] = z[p * piece:(p + 1) * piece, :]
            project(h_scrs[g], tabs[g], scale, emit)


def _rope_tables(seq):
    inv = 1.0 / (ROPE_THETA ** (jnp.arange(0, ROPE_ROT, 2, dtype=F32) / ROPE_ROT))
    ang = jnp.arange(seq, dtype=F32)[:, None] * inv[None, :]
    cos, sin = jnp.cos(ang), jnp.sin(ang)
    ones = jnp.ones((seq, DL_HEAD_DIM - ROPE_ROT), F32)
    zeros = jnp.zeros((seq, DL_HEAD_DIM - ROPE_ROT), F32)
    cs = jnp.concatenate([cos, cos, ones], axis=1)
    sn = jnp.concatenate([-sin, sin, zeros], axis=1)
    reps = LANES // DL_HEAD_DIM
    return jnp.tile(cs, (1, reps)), jnp.tile(sn, (1, reps))


def _residue_major_rows(tab, tm, dil):
    s, w = tab.shape
    return tab.reshape(s // tm, tm // dil, dil, w).transpose(0, 2, 1, 3).reshape(s, w)


def _inproj(x, mod, g_mix, w_in_bf):
    b, s, d = x.shape
    tm = min(1024, s)
    nsb = s // tm
    m = b * s
    cs, sn = _rope_tables(s)
    tables = []
    for _, dil in DL_PAIRS:
        assert s % (dil * DL_BLOCK) == 0, "sequence must tile into dilated blocks without padding"
        tables += [_residue_major_rows(cs, tm, dil), _residue_major_rows(sn, tm, dil)]
    heads_per_tile = COL_TILE // DA_PAIR
    head_shape = jax.ShapeDtypeStruct((b, DA_HEADS, s, DA_PAIR), BF16)

    def head_spec(first_tile):
        return pl.BlockSpec((None, heads_per_tile, tm, DA_PAIR),
                            lambda i, j: (i // nsb, jnp.clip(j - first_tile, 0, 1), i % nsb, 0))

    out_shape = [head_shape, head_shape, head_shape,
                 jax.ShapeDtypeStruct((m, (N_COL_TILES - CT_GA) * COL_TILE), BF16)]
    out_specs = [head_spec(CT_QA), head_spec(CT_KA), head_spec(CT_VA),
                 pl.BlockSpec((tm, COL_TILE), lambda i, j: (i, jnp.maximum(j - CT_GA, 0)))]
    for g, (_, dil) in enumerate(DL_PAIRS):
        out_shape.append(jax.ShapeDtypeStruct((b, dil, s // dil, 3 * DL_W), BF16))
        out_specs.append(pl.BlockSpec(
            (None, dil, tm // dil, COL_TILE),
            lambda i, j, g=g: (i // nsb, 0, i % nsb, jnp.clip(j - (CT_DL + 3 * g), 0, 2))))
    table_spec = pl.BlockSpec((tm, LANES), lambda i, j: (i % nsb, 0))
    if s >= 4096:
        w_in_bf = w_in_bf.reshape(d, N_COL_TILES, COL_TILE).transpose(1, 0, 2)
        w_spec = pl.BlockSpec((None, d, COL_TILE), lambda i, j: (j, 0, 0))
    else:
        w_spec = pl.BlockSpec((d, COL_TILE), lambda i, j: (0, j))
    return pl.pallas_call(
        functools.partial(_inproj_kernel, row_chunk=min(256, tm)),
        grid=(b * nsb, N_COL_TILES),
        in_specs=[pl.BlockSpec((tm, d), lambda i, j: (i, 0)),
                  pl.BlockSpec((None, 6, d), lambda i, j: (i // nsb, 0, 0)),
                  pl.BlockSpec((1, d), lambda i, j: (0, 0)),
                  w_spec] + [table_spec] * 6,
        out_specs=out_specs,
        out_shape=out_shape,
        scratch_shapes=[pltpu.VMEM((tm, d), BF16), pltpu.VMEM((tm, d), BF16), pltpu.VMEM((tm, d), BF16),
                        pltpu.VMEM((d // LANES, tm, LANES), F32)],
        compiler_params=_cparams(("parallel", "arbitrary")),
        name="inproj",
    )(x.reshape(m, d), mod, g_mix.reshape(1, d), w_in_bf, *tables)


def _diffattn_kernel(q_ref, k_ref, v_ref, lq1_ref, lk1_ref, lq2_ref, lk2_ref, gs_ref, o_ref,
                     v2_scr, *, tk, lambda_init):
    qi = pl.program_id(2)
    tq = q_ref.shape[0]
    seq = k_ref.shape[0]

    @pl.when(qi == 0)
    def _():
        v2_scr[:, :DA_PAIR] = v_ref[...]
        v2_scr[:, DA_PAIR:] = jnp.ones((seq, DA_PAIR), BF16)

    lane = lax.broadcasted_iota(jnp.int32, (tq, DA_PAIR), 1)
    q = q_ref[...]
    zero = jnp.zeros_like(q)
    qmaps = (jnp.where(lane < DA_HEAD_DIM, q, zero), jnp.where(lane >= DA_HEAD_DIM, q, zero))
    m = [None, None]
    acc = [None, None]
    for kk in range(seq // tk):
        kb = k_ref[kk * tk:(kk + 1) * tk, :]
        vb = v2_scr[kk * tk:(kk + 1) * tk, :]
        for mp in range(2):
            s = _nt_dot(qmaps[mp], kb)
            m_cur = jnp.max(s, axis=1, keepdims=True)
            if kk == 0:
                m_new = jnp.broadcast_to(m_cur, (tq, LANES))
            else:
                m_new = jnp.maximum(m[mp], m_cur)
            p = jnp.exp2(s - jnp.tile(m_new, (1, tk // LANES)))
            pv = _dot(p.astype(BF16), vb)
            if kk == 0:
                acc[mp] = pv
            else:
                alpha = jnp.exp2(m[mp] - m_new)
                acc[mp] = acc[mp] * jnp.tile(alpha, (1, 2)) + pv
            m[mp] = m_new

    o1 = acc[0][:, :DA_PAIR] / acc[0][:, DA_PAIR:]
    o2 = acc[1][:, :DA_PAIR] / acc[1][:, DA_PAIR:]
    lam = (jnp.exp(jnp.sum(lq1_ref[...] * lk1_ref[...], axis=1, keepdims=True))
           - jnp.exp(jnp.sum(lq2_ref[...] * lk2_ref[...], axis=1, keepdims=True)) + lambda_init)
    o = o1 - lam * o2
    o_ref[...] = ((_rms(o) * gs_ref[...]) * (1.0 - lambda_init)).astype(BF16)


def _diffattn(qa, ka, va, lam_q1, lam_k1, lam_q2, lam_k2, g_subln, lambda_init):
    b, nh, seq, _ = qa.shape
    tq = min(1024, seq)
    tk = min(512, seq)
    lam_spec = pl.BlockSpec((1, DA_HEAD_DIM), lambda bi, h, qi: (0, 0))
    kv_spec = pl.BlockSpec((None, None, seq, DA_PAIR), lambda bi, h, qi: (bi, h, 0, 0))
    return pl.pallas_call(
        functools.partial(_diffattn_kernel, tk=tk, lambda_init=lambda_init),
        grid=(b, nh, seq // tq),
        in_specs=[pl.BlockSpec((None, None, tq, DA_PAIR), lambda bi, h, qi: (bi, h, qi, 0)),
                  kv_spec, kv_spec, lam_spec, lam_spec, lam_spec, lam_spec,
                  pl.BlockSpec((1, DA_PAIR), lambda bi, h, qi: (0, 0))],
        out_specs=pl.BlockSpec((None, tq, DA_PAIR), lambda bi, h, qi: (bi, qi, h)),
        out_shape=jax.ShapeDtypeStruct((b, seq, nh * DA_PAIR), BF16),
        scratch_shapes=[pltpu.VMEM((seq, 2 * DA_PAIR), BF16)],
        compiler_params=_cparams(("parallel", "parallel", "arbitrary")),
        name="diffattn",
    )(qa, ka, va, lam_q1.reshape(1, -1), lam_k1.reshape(1, -1), lam_q2.reshape(1, -1),
      lam_k2.reshape(1, -1), g_subln.reshape(1, -1))


def _dilated_kernel(q_ref, k_ref, v_ref, o_ref, lse_ref, *, sub, kw, radius):
    n = pl.program_id(2)
    tq = q_ref.shape[0]
    t = k_ref.shape[0]
    lane = lax.broadcasted_iota(jnp.int32, (sub, LANES), 1)
    low = lane < DL_HEAD_DIM
    zero = jnp.zeros((sub, LANES), BF16)
    ones = jnp.ones((kw, LANES), BF16)
    o_rows, lse_rows = [], []
    for sb in range(tq // sub):
        q0 = n * tq + sb * sub
        ks = pl.multiple_of(jnp.clip(q0 - radius, 0, t - kw), DL_BLOCK)
        qpos = lax.broadcasted_iota(jnp.int32, (sub, kw), 0) + q0
        kpos = lax.broadcasted_iota(jnp.int32, (sub, kw), 1) + ks
        band = jnp.abs(kpos - qpos) <= radius
        qrows = slice(sb * sub, (sb + 1) * sub)
        scores = []
        for j in range(DL_W // LANES):
            cols = slice(j * LANES, (j + 1) * LANES)
            qp = q_ref[qrows, cols]
            kp = k_ref[pl.ds(ks, kw), cols]
            for par in range(2):
                qm = jnp.where(low, qp, zero) if par == 0 else jnp.where(low, zero, qp)
                scores.append(_nt_dot(qm, kp))
        lses, o_cols = [], []
        for j in range(DL_W // LANES):
            cols = slice(j * LANES, (j + 1) * LANES)
            vp = jnp.concatenate([v_ref[pl.ds(ks, kw), cols], ones], axis=1)
            outs = []
            for par in range(2):
                s = jnp.where(band, scores[2 * j + par], NEG)
                m = jnp.max(s, axis=1, keepdims=True)
                p = jnp.exp2(s - m)
                r = _dot(p.astype(BF16), vp)
                den = r[:, LANES:]
                outs.append(r[:, :LANES] / den)
                lses.append(m * LN2 + jnp.log(den))
            o_cols.append(jnp.where(low, outs[0], outs[1]).astype(BF16))
        lse_tile = jnp.zeros((sub, LANES), F32)
        for hd, lse in enumerate(lses):
            lse_tile = jnp.where(lane == hd, lse, lse_tile)
        o_rows.append(jnp.concatenate(o_cols, axis=1))
        lse_rows.append(lse_tile)
    o_ref[...] = jnp.concatenate(o_rows, axis=0)
    lse_ref[...] = jnp.concatenate(lse_rows, axis=0)


def _dilated(zb, g, window, dil):
    b, _, t, _ = zb.shape
    radius = window // (2 * dil)
    tq = min(256, t)
    sub = min(2 * radius, tq)
    kw = min(sub + 2 * radius, t)
    return pl.pallas_call(
        functools.partial(_dilated_kernel, sub=sub, kw=kw, radius=radius),
        grid=(b, dil, t // tq),
        in_specs=[pl.BlockSpec((None, None, tq, DL_W), lambda bi, c, n: (bi, c, n, 0)),
                  pl.BlockSpec((None, None, t, DL_W), lambda bi, c, n: (bi, c, 0, 1)),
                  pl.BlockSpec((None, None, t, DL_W), lambda bi, c, n: (bi, c, 0, 2))],
        out_specs=[pl.BlockSpec((None, None, tq, DL_W), lambda bi, c, n: (bi, c, n, 0)),
                   pl.BlockSpec((None, None, tq, LANES), lambda bi, c, n: (bi, c, n, 0))],
        out_shape=[jax.ShapeDtypeStruct((b, dil, t, DL_W), BF16),
                   jax.ShapeDtypeStruct((b, dil, t, LANES), F32)],
        compiler_params=_cparams(("parallel", "parallel", "arbitrary")),
        name=f"dilated{g}",
    )(zb, zb, zb)


def _natural_order(ref, scr):
    dil, n, w = ref.shape
    slabs = []
    for cb in range(w // LANES):
        cols = slice(cb * LANES, (cb + 1) * LANES)
        if dil == 1:
            slabs.append(ref[0, :, cols].astype(F32))
            continue
        for c in range(dil):
            scr[cb, pl.ds(c, n, stride=dil), :] = ref[c, :, cols].astype(F32)
        slabs.append(scr[cb])
    return slabs


def _merge_kernel(x_ref, mod_ref, oa_ref, o0_ref, o1_ref, o2_ref, l0_ref, l1_ref, l2_ref,
                  ga_ref, gb_ref, wa_ref, wb_ref, wo_ref, ex_ref, g_ref, x1_ref, h2_ref,
                  os1_scr, os2_scr, ls1_scr, ls2_scr):
    l0, = _natural_order(l0_ref, None)
    l1, = _natural_order(l1_ref, ls1_scr)
    l2, = _natural_order(l2_ref, ls2_scr)
    o0 = _natural_order(o0_ref, None)
    o1 = _natural_order(o1_ref, os1_scr)
    o2 = _natural_order(o2_ref, os2_scr)
    mx = jnp.maximum(jnp.maximum(l0, l1), l2)
    e0, e1, e2 = jnp.exp(l0 - mx), jnp.exp(l1 - mx), jnp.exp(l2 - mx)
    inv = 1.0 / (e0 + e1 + e2)
    ob = None
    for e, o in ((e0, o0), (e1, o1), (e2, o2)):
        w = e * inv
        w_hi = w.astype(BF16)
        w_lo = (w - w_hi.astype(F32)).astype(BF16)
        wx = _dot(w_hi, ex_ref[...]) + _dot(w_lo, ex_ref[...])
        term = [wx[:, cb * LANES:(cb + 1) * LANES] * o[cb] for cb in range(len(o))]
        ob = term if ob is None else [a + t for a, t in zip(ob, term)]
    ob = jnp.concatenate(ob, axis=1)
    ya = _dot(oa_ref[...], wa_ref[...])
    yb = _dot(ob.astype(BF16), wb_ref[...])
    ga = ga_ref[...].astype(F32)
    gb = gb_ref[...].astype(F32)
    merged = ya / (1.0 + jnp.exp(-ga)) + yb / (1.0 + jnp.exp(-gb))
    y = _dot(merged.astype(BF16), wo_ref[...])
    x1 = x_ref[...] + mod_ref[2:3, :] * y
    x1_ref[...] = x1
    h2 = (_rms(x1) * g_ref[...]) * (1.0 + mod_ref[4:5, :]) + mod_ref[3:4, :]
    h2_ref[...] = h2.astype(BF16)


def _merge(x, mod, zg, oa, obs, lses, wa_bf, wb_bf, wo_bf, g_ffn):
    b, s, d = x.shape
    m = b * s
    tm = min(512, s)
    nsb = s // tm
    row = lambda w: pl.BlockSpec((tm, w), lambda i: (i, 0))
    full = lambda shape: pl.BlockSpec(shape, lambda i: (0, 0))

    def residue_major(dil, w):
        return pl.BlockSpec((None, dil, tm // dil, w), lambda i: (i // nsb, 0, i % nsb, 0))

    dils = [dil for _, dil in DL_PAIRS]
    head = jnp.arange(LANES)[:, None]
    col = jnp.arange(DL_W)[None, :]
    expand = ((col // DL_HEAD_DIM) == head).astype(BF16)
    x1, h2 = pl.pallas_call(
        _merge_kernel,
        grid=(m // tm,),
        in_specs=[row(d), pl.BlockSpec((None, 6, d), lambda i: (i // nsb, 0, 0)),
                  row(DA_HEADS * DA_PAIR)]
                 + [residue_major(dil, DL_W) for dil in dils]
                 + [residue_major(dil, LANES) for dil in dils]
                 + [pl.BlockSpec((tm, d), lambda i: (i, 0)), pl.BlockSpec((tm, d), lambda i: (i, 1)),
                    full(wa_bf.shape), full(wb_bf.shape), full(wo_bf.shape), full(expand.shape),
                    full((1, d))],
        out_specs=[row(d), row(d)],
        out_shape=[jax.ShapeDtypeStruct((m, d), F32), jax.ShapeDtypeStruct((m, d), BF16)],
        scratch_shapes=[pltpu.VMEM((DL_W // LANES, tm, LANES), F32),
                        pltpu.VMEM((DL_W // LANES, tm, LANES), F32),
                        pltpu.VMEM((1, tm, LANES), F32), pltpu.VMEM((1, tm, LANES), F32)],
        compiler_params=_cparams(("parallel",)),
        name="merge",
    )(x.reshape(m, d), mod, oa.reshape(m, -1), *obs, *lses, zg, zg, wa_bf, wb_bf, wo_bf, expand,
      g_ffn.reshape(1, d))
    return x1, h2


def _ffn_kernel(h_ref, x1_ref, mod_ref, wg_ref, wu_ref, wd_ref, gf_ref, o_ref, *, chunk, final_norm):
    h = h_ref[...]
    hidden = wd_ref.shape[0]
    acc = None
    for c in range(hidden // chunk):
        cols = slice(c * chunk, (c + 1) * chunk)
        gate = _dot(h, wg_ref[:, cols])
        up = _dot(h, wu_ref[:, cols])
        f = (gate / (1.0 + jnp.exp(-gate))) * up
        part = _dot(f.astype(BF16), wd_ref[cols, :])
        acc = part if acc is None else acc + part
    x2 = x1_ref[...] + mod_ref[5:6, :] * acc
    o_ref[...] = _rms(x2) * gf_ref[...] if final_norm else x2


def _ffn(h2, x1, mod, s, wg_bf, wu_bf, wd_bf, g_final, final_norm):
    m, d = x1.shape
    tm = min(512, s)
    nsb = s // tm
    hidden = wd_bf.shape[0]
    chunk = hidden // 2
    assert chunk % LANES == 0
    row = pl.BlockSpec((tm, d), lambda i: (i, 0))
    full = lambda shape: pl.BlockSpec(shape, lambda i: (0, 0), pipeline_mode=pl.Buffered(1))
    return pl.pallas_call(
        functools.partial(_ffn_kernel, chunk=chunk, final_norm=final_norm),
        grid=(m // tm,),
        in_specs=[row, row, pl.BlockSpec((None, 6, d), lambda i: (i // nsb, 0, 0)),
                  full(wg_bf.shape), full(wu_bf.shape), full(wd_bf.shape), full((1, d))],
        out_specs=row,
        out_shape=jax.ShapeDtypeStruct((m, d), F32),
        compiler_params=_cparams(("parallel",)),
        name="ffn",
    )(h2, x1, mod, wg_bf, wu_bf, wd_bf, g_final.reshape(1, d))


def _trunk(x, mod_all, weights):
    (g_mix, w_in, lam_q1, lam_k1, lam_q2, lam_k2, g_subln, w_a, w_b, w_out, g_ffn,
     w_gu, w_down, g_final) = weights
    b, s, d = x.shape
    depth = w_in.shape[0]
    hidden = w_down.shape[1]
    for l in range(depth):
        lambda_init = 0.8 - 0.6 * math.exp(-0.3 * l)
        mod = mod_all[l]
        qa, ka, va, zg, *zbs = _inproj(x, mod, g_mix[l], w_in[l].astype(BF16))
        oa = _diffattn(qa, ka, va, lam_q1[l], lam_k1[l], lam_q2[l], lam_k2[l], g_subln[l], lambda_init)
        obs, lses = [], []
        for g, (window, dil) in enumerate(DL_PAIRS):
            o, lse = _dilated(zbs[g], g, window, dil)
            obs.append(o)
            lses.append(lse)
        x1, h2 = _merge(x, mod, zg, oa, obs, lses, w_a[l].astype(BF16), w_b[l].astype(BF16),
                        w_out[l].astype(BF16), g_ffn[l])
        wgu = w_gu[l].astype(BF16)
        x = _ffn(h2, x1, mod, s, wgu[:, :hidden], wgu[:, hidden:], w_down[l].astype(BF16),
                 g_final, l == depth - 1).reshape(b, s, d)
    return x


def kernel(x_prompt, x_sample, c_prompt, c_sample, w_ada, b_ada, g_mix, w_in, lam_q1, lam_k1,
           lam_q2, lam_k2, g_subln, w_a, w_b, w_out, g_ffn, w_gu, w_down, g_final):
    depth = w_in.shape[0]
    bp = c_prompt.shape[0]
    c_all = jnp.concatenate([c_prompt, c_sample], axis=0)
    mods = [_ada(c_all, w_ada[l], b_ada[l]).reshape(c_all.shape[0], 6, D_MODEL) for l in range(depth)]
    weights = (g_mix, w_in, lam_q1, lam_k1, lam_q2, lam_k2, g_subln, w_a, w_b, w_out, g_ffn,
               w_gu, w_down, g_final)
    y_prompt = _trunk(x_prompt, [m[:bp] for m in mods], weights)
    y_sample = _trunk(x_sample, [m[bp:] for m in mods], weights)
    return (y_prompt, y_sample)
```

```python
import functools
import math

import jax
import jax.numpy as jnp
from jax import lax
from jax.experimental import pallas as pl
from jax.experimental.pallas import tpu as pltpu

F32 = jnp.float32
BF16 = jnp.bfloat16

D_MODEL = 1024
DA_HEADS = 8
DA_HEAD_DIM = 64
DA_PAIR = 2 * DA_HEAD_DIM
DL_PAIRS = ((128, 1), (512, 4), (2048, 16))
DL_HEADS = 8
DL_HEAD_DIM = 64
DL_W = DL_HEADS * DL_HEAD_DIM
DL_BLOCK = 64
ROPE_THETA = 500000.0
ROPE_ROT = DL_HEAD_DIM // 4
ROPE_HALF = ROPE_ROT // 2
EPS = 1e-6
NEG = -1e30
LOG2E = math.log2(math.e)
LN2 = math.log(2.0)
Q_SCALE = DA_HEAD_DIM ** -0.5 * LOG2E

LANES = 128
COL_TILE = 512
N_IN = 2 * DA_HEADS * DA_PAIR + DA_HEADS * DA_PAIR + 9 * DL_W + 2 * D_MODEL
N_COL_TILES = N_IN // COL_TILE
CT_QA, CT_KA, CT_VA = 0, 2, 4
CT_DL = 6
CT_GA = 15
VMEM_LIMIT = 56 * 1024 * 1024


def _cparams(sem):
    return pltpu.CompilerParams(dimension_semantics=sem, vmem_limit_bytes=VMEM_LIMIT)


def _nt_dot(a, b):
    return lax.dot_general(a, b, (((1,), (1,)), ((), ())), preferred_element_type=F32)


def _dot(a, b):
    return jnp.dot(a, b, preferred_element_type=F32)


def _rms(x):
    return x * lax.rsqrt(jnp.mean(x * x, axis=-1, keepdims=True) + EPS)


def _ada_kernel(c_ref, w_ref, b_ref, o_ref):
    c = c_ref[...]
    a = c * (1.0 / (1.0 + jnp.exp(-c)))
    w = w_ref[...]
    a_hi = a.astype(BF16)
    a_lo = (a - a_hi.astype(F32)).astype(BF16)
    w_hi = w.astype(BF16)
    w_lo = (w - w_hi.astype(F32)).astype(BF16)
    o_ref[...] = _dot(a_hi, w_hi) + _dot(a_hi, w_lo) + _dot(a_lo, w_hi) + b_ref[...]


def _ada(c, w_ada, b_ada):
    nb, d = c.shape
    n = w_ada.shape[1]
    tn = COL_TILE
    return pl.pallas_call(
        _ada_kernel,
        grid=(n // tn,),
        in_specs=[pl.BlockSpec((nb, d), lambda j: (0, 0)),
                  pl.BlockSpec((d, tn), lambda j: (0, j)),
                  pl.BlockSpec((1, tn), lambda j: (0, j))],
        out_specs=pl.BlockSpec((nb, tn), lambda j: (0, j)),
        out_shape=jax.ShapeDtypeStruct((nb, n), F32),
        compiler_params=_cparams(("arbitrary",)),
        name="ada",
    )(c, w_ada, b_ada.reshape(1, n))


def _inproj_kernel(x_ref, mod_ref, g_ref, w_ref, tab_ref, hm_ref, zg_ref, zb0_ref, zb1_ref, zb2_ref,
                   h0_scr, h1_scr, h2_scr, hs_scr, *, row_chunk):
    j = pl.program_id(1)
    tm, d = x_ref.shape
    h_scrs = (h0_scr, h1_scr, h2_scr)
    dils = tuple(dil for _, dil in DL_PAIRS)

    @pl.when(j == 0)
    def _():
        y = _rms(x_ref[...]) * g_ref[...]
        h = y * (1.0 + mod_ref[1:2, :]) + mod_ref[0:1, :]
        h0_scr[...] = h.astype(BF16)
        for cb in range(d // LANES):
            hs_scr[cb] = h[:, cb * LANES:(cb + 1) * LANES]
        for g in range(1, len(dils)):
            n = tm // dils[g]
            for c in range(dils[g]):
                for cb in range(d // LANES):
                    h_scrs[g][c * n:(c + 1) * n, cb * LANES:(cb + 1) * LANES] = (
                        hs_scr[cb, pl.ds(c, n, stride=dils[g]), :].astype(BF16))

    def project(h_scr, g, scale, emit):
        for r in range(tm // row_chunk):
            rows = pl.ds(r * row_chunk, row_chunk)
            z = _dot(h_scr[rows, :], w_ref[...])
            if g is not None:
                cs, sn = scale(tab_ref[2 * g, rows, :], tab_ref[2 * g + 1, rows, :])
                z = jnp.concatenate(
                    [zc * cs + pltpu.roll(zc, LANES // 2, 1) * sn
                     for zc in (z[:, cb * LANES:(cb + 1) * LANES] for cb in range(COL_TILE // LANES))],
                    axis=1)
            emit(r * row_chunk, z.astype(BF16))

    def head_major(row0, z):
        for hh in range(COL_TILE // DA_PAIR):
            hm_ref[hh, row0:row0 + row_chunk, :] = z[:, hh * DA_PAIR:(hh + 1) * DA_PAIR]

    @pl.when(j < CT_KA)
    def _():
        project(h0_scr, 0, lambda cs, sn: (cs * Q_SCALE, sn * Q_SCALE), head_major)

    @pl.when((j >= CT_KA) & (j < CT_VA))
    def _():
        project(h0_scr, 0, lambda cs, sn: (cs, sn), head_major)

    @pl.when((j >= CT_VA) & (j < CT_DL))
    def _():
        project(h0_scr, None, None, head_major)

    @pl.when(j >= CT_GA)
    def _():
        def emit(row0, z):
            zg_ref[row0:row0 + row_chunk, :] = z
        project(h0_scr, None, None, emit)

    for g, zb_ref in enumerate((zb0_ref, zb1_ref, zb2_ref)):
        @pl.when((j >= CT_DL + 3 * g) & (j < CT_DL + 3 * g + 3))
        def _(g=g, zb_ref=zb_ref):
            kind = j - (CT_DL + 3 * g)
            n = tm // dils[g]
            piece = min(n, row_chunk)

            def scale(cs, sn):
                f = jnp.where(kind == 0, Q_SCALE, 1.0).astype(F32)
                return jnp.where(kind < 2, cs * f, 1.0), jnp.where(kind < 2, sn * f, 0.0)

            def emit(row0, z):
                for p in range(row_chunk // piece):
                    r0 = row0 + p * piece
                    zb_ref[r0 // n, r0 % n:r0 % n + piece, :] = z[p * piece:(p + 1) * piece, :]
            project(h_scrs[g], g, scale, emit)


def _rotary_lane_layout(w):
    d, n = w.shape
    blk = w.reshape(d, n // LANES, LANES)
    a, b, h = ROPE_HALF, ROPE_ROT, DL_HEAD_DIM
    swapped = jnp.concatenate([blk[..., :a], blk[..., h:h + a], blk[..., b:h], blk[..., a:b],
                               blk[..., h + a:]], axis=-1)
    per_tile = COL_TILE // LANES
    qk_tiles = [CT_QA, CT_QA + 1, CT_KA, CT_KA + 1] + [CT_DL + 3 * g + t for g in range(len(DL_PAIRS))
                                                      for t in (0, 1)]
    is_qk = jnp.zeros((n // LANES,), bool).at[
        jnp.asarray([t * per_tile + c for t in qk_tiles for c in range(per_tile)])].set(True)
    return jnp.where(is_qk[None, :, None], swapped, blk).reshape(d, n)


def _first_head_lanes(lane):
    return (lane < ROPE_HALF) | ((lane >= ROPE_ROT) & (lane < DL_HEAD_DIM + ROPE_HALF))


def _rope_tables(seq):
    inv = 1.0 / (ROPE_THETA ** (jnp.arange(0, ROPE_ROT, 2, dtype=F32) / ROPE_ROT))
    ang = jnp.arange(seq, dtype=F32)[:, None] * inv[None, :]
    cos, sin = jnp.cos(ang), jnp.sin(ang)
    ones = jnp.ones((seq, DL_HEAD_DIM - ROPE_ROT), F32)
    zeros = jnp.zeros((seq, DL_HEAD_DIM - ROPE_ROT), F32)
    cs = jnp.concatenate([cos, cos, ones, cos, cos, ones], axis=1)
    sn = jnp.concatenate([-sin, -sin, zeros, sin, sin, zeros], axis=1)
    return cs, sn


def _residue_major_rows(tab, tm, dil):
    s, w = tab.shape
    return tab.reshape(s // tm, tm // dil, dil, w).transpose(0, 2, 1, 3).reshape(s, w)


def _inproj(x, mod, g_mix, w_in_bf):
    b, s, d = x.shape
    tm = min(1024, s)
    nsb = s // tm
    m = b * s
    cs, sn = _rope_tables(s)
    tables = []
    for _, dil in DL_PAIRS:
        assert s % (dil * DL_BLOCK) == 0, "sequence must tile into dilated blocks without padding"
        tables += [_residue_major_rows(cs, tm, dil), _residue_major_rows(sn, tm, dil)]
    tables = jnp.stack(tables)
    heads_per_tile = COL_TILE // DA_PAIR
    out_shape = [jax.ShapeDtypeStruct((b, 3 * DA_HEADS, s, DA_PAIR), BF16),
                 jax.ShapeDtypeStruct((m, (N_COL_TILES - CT_GA) * COL_TILE), BF16)]
    out_specs = [pl.BlockSpec((None, heads_per_tile, tm, DA_PAIR),
                              lambda i, j: (i // nsb, jnp.minimum(j, CT_DL - 1), i % nsb, 0)),
                 pl.BlockSpec((tm, COL_TILE), lambda i, j: (i, jnp.maximum(j - CT_GA, 0)))]
    for g, (_, dil) in enumerate(DL_PAIRS):
        out_shape.append(jax.ShapeDtypeStruct((b, dil, s // dil, 3 * DL_W), BF16))
        out_specs.append(pl.BlockSpec(
            (None, dil, tm // dil, COL_TILE),
            lambda i, j, g=g: (i // nsb, 0, i % nsb, jnp.clip(j - (CT_DL + 3 * g), 0, 2))))
    return pl.pallas_call(
        functools.partial(_inproj_kernel, row_chunk=min(256, tm)),
        grid=(b * nsb, N_COL_TILES),
        in_specs=[pl.BlockSpec((tm, d), lambda i, j: (i, 0)),
                  pl.BlockSpec((None, 6, d), lambda i, j: (i // nsb, 0, 0)),
                  pl.BlockSpec((1, d), lambda i, j: (0, 0)),
                  pl.BlockSpec((d, COL_TILE), lambda i, j: (0, j)),
                  pl.BlockSpec((2 * len(DL_PAIRS), tm, LANES), lambda i, j: (0, i % nsb, 0))],
        out_specs=out_specs,
        out_shape=out_shape,
        scratch_shapes=[pltpu.VMEM((tm, d), BF16), pltpu.VMEM((tm, d), BF16), pltpu.VMEM((tm, d), BF16),
                        pltpu.VMEM((d // LANES, tm, LANES), F32)],
        compiler_params=_cparams(("parallel", "arbitrary")),
        name="inproj",
    )(x.reshape(m, d), mod, g_mix.reshape(1, d), _rotary_lane_layout(w_in_bf), tables)


def _diffattn_kernel(q_ref, k_ref, v_ref, lq1_ref, lk1_ref, lq2_ref, lk2_ref, gs_ref, o_ref,
                     v2_scr, *, tk, lambda_init):
    qi = pl.program_id(2)
    tq = q_ref.shape[0]
    seq = k_ref.shape[0]

    @pl.when(qi == 0)
    def _():
        v2_scr[:, :DA_PAIR] = v_ref[...]
        v2_scr[:, DA_PAIR:] = jnp.ones((seq, DA_PAIR), BF16)

    lane = lax.broadcasted_iota(jnp.int32, (tq, DA_PAIR), 1)
    q = q_ref[...]
    zero = jnp.zeros_like(q)
    first = _first_head_lanes(lane)
    qmaps = (jnp.where(first, q, zero), jnp.where(first, zero, q))
    m = [None, None]
    acc = [None, None]
    for kk in range(seq // tk):
        kb = k_ref[kk * tk:(kk + 1) * tk, :]
        vb = v2_scr[kk * tk:(kk + 1) * tk, :]
        for mp in range(2):
            s = _nt_dot(qmaps[mp], kb)
            m_cur = jnp.max(s, axis=1, keepdims=True)
            if kk == 0:
                m_new = jnp.broadcast_to(m_cur, (tq, LANES))
            else:
                m_new = jnp.maximum(m[mp], m_cur)
            p = jnp.exp2(s - jnp.tile(m_new, (1, tk // LANES)))
            pv = _dot(p.astype(BF16), vb)
            if kk == 0:
                acc[mp] = pv
            else:
                alpha = jnp.exp2(m[mp] - m_new)
                acc[mp] = acc[mp] * jnp.tile(alpha, (1, 2)) + pv
            m[mp] = m_new

    o1 = acc[0][:, :DA_PAIR] / acc[0][:, DA_PAIR:]
    o2 = acc[1][:, :DA_PAIR] / acc[1][:, DA_PAIR:]
    lam = (jnp.exp(jnp.sum(lq1_ref[...] * lk1_ref[...], axis=1, keepdims=True))
           - jnp.exp(jnp.sum(lq2_ref[...] * lk2_ref[...], axis=1, keepdims=True)) + lambda_init)
    o = o1 - lam * o2
    o_ref[...] = ((_rms(o) * gs_ref[...]) * (1.0 - lambda_init)).astype(BF16)


def _diffattn(hm, lam_q1, lam_k1, lam_q2, lam_k2, g_subln, lambda_init):
    b, nh3, seq, _ = hm.shape
    nh = nh3 // 3
    tq = seq if seq <= 2048 else 1024
    tk = min(512, seq)
    lam_spec = pl.BlockSpec((1, DA_HEAD_DIM), lambda bi, h, qi: (0, 0))
    return pl.pallas_call(
        functools.partial(_diffattn_kernel, tk=tk, lambda_init=lambda_init),
        grid=(b, nh, seq // tq),
        in_specs=[pl.BlockSpec((None, None, tq, DA_PAIR), lambda bi, h, qi: (bi, h, qi, 0)),
                  pl.BlockSpec((None, None, seq, DA_PAIR), lambda bi, h, qi: (bi, nh + h, 0, 0)),
                  pl.BlockSpec((None, None, seq, DA_PAIR), lambda bi, h, qi: (bi, 2 * nh + h, 0, 0)),
                  lam_spec, lam_spec, lam_spec, lam_spec,
                  pl.BlockSpec((1, DA_PAIR), lambda bi, h, qi: (0, 0))],
        out_specs=pl.BlockSpec((None, tq, DA_PAIR), lambda bi, h, qi: (bi, qi, h)),
        out_shape=jax.ShapeDtypeStruct((b, seq, nh * DA_PAIR), BF16),
        scratch_shapes=[pltpu.VMEM((seq, 2 * DA_PAIR), BF16)],
        compiler_params=_cparams(("parallel", "parallel", "arbitrary")),
        name="diffattn",
    )(hm, hm, hm, lam_q1.reshape(1, -1), lam_k1.reshape(1, -1), lam_q2.reshape(1, -1),
      lam_k2.reshape(1, -1), g_subln.reshape(1, -1))


def _dilated_kernel(q_ref, k_ref, v_ref, o_ref, lse_ref, *, sub, kw, radius):
    n = pl.program_id(2)
    tq = q_ref.shape[0]
    t = k_ref.shape[0]
    lane = lax.broadcasted_iota(jnp.int32, (sub, LANES), 1)
    low = lane < DL_HEAD_DIM
    first = _first_head_lanes(lane)
    zero = jnp.zeros((sub, LANES), BF16)
    ones = jnp.ones((kw, LANES), BF16)
    n_pairs = DL_W // LANES
    o_rows, lse_rows = [], []
    for sb in range(tq // sub):
        q0 = n * tq + sb * sub
        ks = pl.multiple_of(jnp.clip(q0 - radius, 0, t - kw), DL_BLOCK)
        qpos = lax.broadcasted_iota(jnp.int32, (sub, kw), 0) + q0
        kpos = lax.broadcasted_iota(jnp.int32, (sub, kw), 1) + ks
        band = jnp.abs(kpos - qpos) <= radius
        all_scores = []
        for j in range(n_pairs):
            cols = slice(j * LANES, (j + 1) * LANES)
            qp = q_ref[sb * sub:(sb + 1) * sub, cols]
            kp = k_ref[pl.ds(ks, kw), cols]
            all_scores.append([_nt_dot(jnp.where(first, qp, zero), kp),
                               _nt_dot(jnp.where(first, zero, qp), kp)])
        ms, dens, o_cols = [], [], []
        for j in range(n_pairs):
            scores = all_scores[j]
            cols = slice(j * LANES, (j + 1) * LANES)
            vp = jnp.concatenate([v_ref[pl.ds(ks, kw), cols], ones], axis=1)
            outs = []
            for par in range(2):
                s = jnp.where(band, scores[par], NEG)
                m = jnp.max(s, axis=1, keepdims=True)
                p = jnp.exp2(s - m)
                r = _dot(p.astype(BF16), vp)
                den = r[:, LANES:]
                outs.append(r[:, :LANES] / den)
                ms.append(m)
                dens.append(den)
            o_cols.append(jnp.where(low, outs[0], outs[1]).astype(BF16))
        m_tile = jnp.zeros((sub, LANES), F32)
        den_tile = jnp.ones((sub, LANES), F32)
        for hd in range(len(ms)):
            m_tile = jnp.where(lane == hd, ms[hd], m_tile)
            den_tile = jnp.where(lane == hd, dens[hd], den_tile)
        lse_tile = m_tile * LN2 + jnp.log(den_tile)
        o_rows.append(jnp.concatenate(o_cols, axis=1))
        lse_rows.append(lse_tile)
    o_ref[...] = jnp.concatenate(o_rows, axis=0)
    lse_ref[...] = jnp.concatenate(lse_rows, axis=0)


def _dilated(zb, g, window, dil):
    b, _, t, _ = zb.shape
    radius = window // (2 * dil)
    tq = min(256, t)
    sub = min(2 * radius, tq)
    kw = min(sub + 2 * radius, t)
    return pl.pallas_call(
        functools.partial(_dilated_kernel, sub=sub, kw=kw, radius=radius),
        grid=(b, dil, t // tq),
        in_specs=[pl.BlockSpec((None, None, tq, DL_W), lambda bi, c, n: (bi, c, n, 0)),
                  pl.BlockSpec((None, None, t, DL_W), lambda bi, c, n: (bi, c, 0, 1)),
                  pl.BlockSpec((None, None, t, DL_W), lambda bi, c, n: (bi, c, 0, 2))],
        out_specs=[pl.BlockSpec((None, None, tq, DL_W), lambda bi, c, n: (bi, c, n, 0)),
                   pl.BlockSpec((None, None, tq, LANES), lambda bi, c, n: (bi, c, n, 0))],
        out_shape=[jax.ShapeDtypeStruct((b, dil, t, DL_W), BF16),
                   jax.ShapeDtypeStruct((b, dil, t, LANES), F32)],
        compiler_params=_cparams(("parallel", "parallel", "arbitrary")),
        name=f"dilated{g}",
    )(zb, zb, zb)


def _natural_order(ref, scr):
    dil, n, w = ref.shape
    slabs = []
    for cb in range(w // LANES):
        cols = slice(cb * LANES, (cb + 1) * LANES)
        if dil == 1:
            slabs.append(ref[0, :, cols].astype(F32))
            continue
        for c in range(dil):
            scr[cb, pl.ds(c, n, stride=dil), :] = ref[c, :, cols].astype(F32)
        slabs.append(scr[cb])
    return slabs


def _merge_kernel(x_ref, mod_ref, oa_ref, o0_ref, o1_ref, o2_ref, l0_ref, l1_ref, l2_ref,
                  ga_ref, gb_ref, wa_ref, wb_ref, wo_ref, ex_ref, g_ref, x1_ref, h2_ref,
                  os1_scr, os2_scr, ls1_scr, ls2_scr):
    l0, = _natural_order(l0_ref, None)
    l1, = _natural_order(l1_ref, ls1_scr)
    l2, = _natural_order(l2_ref, ls2_scr)
    o0 = _natural_order(o0_ref, None)
    o1 = _natural_order(o1_ref, os1_scr)
    o2 = _natural_order(o2_ref, os2_scr)
    mx = jnp.maximum(jnp.maximum(l0, l1), l2)
    e0, e1, e2 = jnp.exp(l0 - mx), jnp.exp(l1 - mx), jnp.exp(l2 - mx)
    inv = 1.0 / (e0 + e1 + e2)
    ob = None
    for e, o in ((e0, o0), (e1, o1), (e2, o2)):
        w = e * inv
        w_hi = w.astype(BF16)
        w_lo = (w - w_hi.astype(F32)).astype(BF16)
        wx = _dot(jnp.concatenate([w_hi, w_lo], axis=1), ex_ref[...])
        term = [wx[:, cb * LANES:(cb + 1) * LANES] * o[cb] for cb in range(len(o))]
        ob = term if ob is None else [a + t for a, t in zip(ob, term)]
    ob = jnp.concatenate(ob, axis=1)
    ya = _dot(oa_ref[...], wa_ref[...])
    yb = _dot(ob.astype(BF16), wb_ref[...])
    ga = ga_ref[...].astype(F32)
    gb = gb_ref[...].astype(F32)
    merged = ya / (1.0 + jnp.exp(-ga)) + yb / (1.0 + jnp.exp(-gb))
    y = _dot(merged.astype(BF16), wo_ref[...])
    x1 = x_ref[...] + mod_ref[2:3, :] * y
    x1_ref[...] = x1
    h2 = (_rms(x1) * g_ref[...]) * (1.0 + mod_ref[4:5, :]) + mod_ref[3:4, :]
    h2_ref[...] = h2.astype(BF16)


def _merge(x, mod, zg, oa, obs, lses, wa_bf, wb_bf, wo_bf, g_ffn):
    b, s, d = x.shape
    m = b * s
    tm = min(512, s)
    nsb = s // tm
    row = lambda w: pl.BlockSpec((tm, w), lambda i: (i, 0))
    full = lambda shape: pl.BlockSpec(shape, lambda i: (0, 0))

    def residue_major(dil, w):
        return pl.BlockSpec((None, dil, tm // dil, w), lambda i: (i // nsb, 0, i % nsb, 0))

    dils = [dil for _, dil in DL_PAIRS]
    head = jnp.arange(LANES)[:, None]
    col = jnp.arange(DL_W)[None, :]
    expand = ((col // DL_HEAD_DIM) == head).astype(BF16)
    expand = jnp.concatenate([expand, expand], axis=0)
    x1, h2 = pl.pallas_call(
        _merge_kernel,
        grid=(m // tm,),
        in_specs=[row(d), pl.BlockSpec((None, 6, d), lambda i: (i // nsb, 0, 0)),
                  row(DA_HEADS * DA_PAIR)]
                 + [residue_major(dil, DL_W) for dil in dils]
                 + [residue_major(dil, LANES) for dil in dils]
                 + [pl.BlockSpec((tm, d), lambda i: (i, 0)), pl.BlockSpec((tm, d), lambda i: (i, 1)),
                    full(wa_bf.shape), full(wb_bf.shape), full(wo_bf.shape), full(expand.shape),
                    full((1, d))],
        out_specs=[row(d), row(d)],
        out_shape=[jax.ShapeDtypeStruct((m, d), F32), jax.ShapeDtypeStruct((m, d), BF16)],
        scratch_shapes=[pltpu.VMEM((DL_W // LANES, tm, LANES), F32),
                        pltpu.VMEM((DL_W // LANES, tm, LANES), F32),
                        pltpu.VMEM((1, tm, LANES), F32), pltpu.VMEM((1, tm, LANES), F32)],
        compiler_params=_cparams(("parallel",)),
        name="merge",
    )(x.reshape(m, d), mod, oa.reshape(m, -1), *obs, *lses, zg, zg, wa_bf, wb_bf, wo_bf, expand,
      g_ffn.reshape(1, d))
    return x1, h2


def _ffn_kernel(h_ref, x1_ref, mod_ref, wg_ref, wu_ref, wd_ref, gf_ref, o_ref, *, chunk, final_norm):
    h = h_ref[...]
    hidden = wd_ref.shape[0]
    acc = None
    for c in range(hidden // chunk):
        cols = slice(c * chunk, (c + 1) * chunk)
        gate = _dot(h, wg_ref[:, cols])
        up = _dot(h, wu_ref[:, cols])
        f = (gate / (1.0 + jnp.exp(-gate))) * up
        part = _dot(f.astype(BF16), wd_ref[cols, :])
        acc = part if acc is None else acc + part
    x2 = x1_ref[...] + mod_ref[5:6, :] * acc
    o_ref[...] = _rms(x2) * gf_ref[...] if final_norm else x2


def _ffn(h2, x1, mod, s, wg_bf, wu_bf, wd_bf, g_final, final_norm):
    m, d = x1.shape
    tm = min(512, s)
    nsb = s // tm
    hidden = wd_bf.shape[0]
    chunk = hidden // 2
    assert chunk % LANES == 0
    row = pl.BlockSpec((tm, d), lambda i: (i, 0))
    full = lambda shape: pl.BlockSpec(shape, lambda i: (0, 0), pipeline_mode=pl.Buffered(1))
    return pl.pallas_call(
        functools.partial(_ffn_kernel, chunk=chunk, final_norm=final_norm),
        grid=(m // tm,),
        in_specs=[row, row, pl.BlockSpec((None, 6, d), lambda i: (i // nsb, 0, 0)),
                  full(wg_bf.shape), full(wu_bf.shape), full(wd_bf.shape), full((1, d))],
        out_specs=row,
        out_shape=jax.ShapeDtypeStruct((m, d), F32),
        compiler_params=_cparams(("parallel",)),
        name="ffn",
    )(h2, x1, mod, wg_bf, wu_bf, wd_bf, g_final.reshape(1, d))


def _trunk(x, mod_all, weights):
    (g_mix, w_in, lam_q1, lam_k1, lam_q2, lam_k2, g_subln, w_a, w_b, w_out, g_ffn,
     w_gu, w_down, g_final) = weights
    b, s, d = x.shape
    depth = w_in.shape[0]
    hidden = w_down.shape[1]
    for l in range(depth):
        lambda_init = 0.8 - 0.6 * math.exp(-0.3 * l)
        mod = mod_all[l]
        hm, zg, *zbs = _inproj(x, mod, g_mix[l], w_in[l].astype(BF16))
        oa = _diffattn(hm, lam_q1[l], lam_k1[l], lam_q2[l], lam_k2[l], g_subln[l], lambda_init)
        obs, lses = [], []
        for g, (window, dil) in enumerate(DL_PAIRS):
            o, lse = _dilated(zbs[g], g, window, dil)
            obs.append(o)
            lses.append(lse)
        x1, h2 = _merge(x, mod, zg, oa, obs, lses, w_a[l].astype(BF16), w_b[l].astype(BF16),
                        w_out[l].astype(BF16), g_ffn[l])
        wgu = w_gu[l].astype(BF16)
        x = _ffn(h2, x1, mod, s, wgu[:, :hidden], wgu[:, hidden:], w_down[l].astype(BF16),
                 g_final, l == depth - 1).reshape(b, s, d)
    return x


def kernel(x_prompt, x_sample, c_prompt, c_sample, w_ada, b_ada, g_mix, w_in, lam_q1, lam_k1,
           lam_q2, lam_k2, g_subln, w_a, w_b, w_out, g_ffn, w_gu, w_down, g_final):
    depth = w_in.shape[0]
    bp = c_prompt.shape[0]
    c_all = jnp.concatenate([c_prompt, c_sample], axis=0)
    mods = [_ada(c_all, w_ada[l], b_ada[l]).reshape(c_all.shape[0], 6, D_MODEL) for l in range(depth)]
    weights = (g_mix, w_in, lam_q1, lam_k1, lam_q2, lam_k2, g_subln, w_a, w_b, w_out, g_ffn,
               w_gu, w_down, g_final)
    y_prompt = _trunk(x_prompt, [m[:bp] for m in mods], weights)
    y_sample = _trunk(x_sample, [m[bp:] for m in mods], weights)
    return (y_prompt, y_sample)
```

```python
import functools
import math

import jax
import jax.numpy as jnp
from jax import lax
from jax.experimental import pallas as pl
from jax.experimental.pallas import tpu as pltpu

F32 = jnp.float32
BF16 = jnp.bfloat16

D_MODEL = 1024
DA_HEADS = 8
DA_HEAD_DIM = 64
DA_PAIR = 2 * DA_HEAD_DIM
DL_PAIRS = ((128, 1), (512, 4), (2048, 16))
DL_HEADS = 8
DL_HEAD_DIM = 64
DL_W = DL_HEADS * DL_HEAD_DIM
DL_BLOCK = 64
ROPE_THETA = 500000.0
ROPE_ROT = DL_HEAD_DIM // 4
ROPE_HALF = ROPE_ROT // 2
EPS = 1e-6
NEG = -1e30
LOG2E = math.log2(math.e)
LN2 = math.log(2.0)
Q_SCALE = DA_HEAD_DIM ** -0.5 * LOG2E

LANES = 128
COL_TILE = 512
N_IN = 2 * DA_HEADS * DA_PAIR + DA_HEADS * DA_PAIR + 9 * DL_W + 2 * D_MODEL
N_COL_TILES = N_IN // COL_TILE
CT_QA, CT_KA, CT_VA = 0, 2, 4
CT_DL = 6
CT_GA = 15
VMEM_LIMIT = 56 * 1024 * 1024


def _cparams(sem):
    return pltpu.CompilerParams(dimension_semantics=sem, vmem_limit_bytes=VMEM_LIMIT)


def _nt_dot(a, b):
    return lax.dot_general(a, b, (((1,), (1,)), ((), ())), preferred_element_type=F32)


def _dot(a, b):
    return jnp.dot(a, b, preferred_element_type=F32)


def _rms(x):
    return x * lax.rsqrt(jnp.mean(x * x, axis=-1, keepdims=True) + EPS)


def _ada_kernel(c_ref, w_ref, b_ref, o_ref):
    c = c_ref[...]
    a = c * (1.0 / (1.0 + jnp.exp(-c)))
    w = w_ref[...]
    a_hi = a.astype(BF16)
    a_lo = (a - a_hi.astype(F32)).astype(BF16)
    w_hi = w.astype(BF16)
    w_lo = (w - w_hi.astype(F32)).astype(BF16)
    o_ref[...] = _dot(a_hi, w_hi) + _dot(a_hi, w_lo) + _dot(a_lo, w_hi) + b_ref[...]


def _ada(c, w_ada, b_ada):
    nb, d = c.shape
    n = w_ada.shape[1]
    tn = COL_TILE
    return pl.pallas_call(
        _ada_kernel,
        grid=(n // tn,),
        in_specs=[pl.BlockSpec((nb, d), lambda j: (0, 0)),
                  pl.BlockSpec((d, tn), lambda j: (0, j)),
                  pl.BlockSpec((1, tn), lambda j: (0, j))],
        out_specs=pl.BlockSpec((nb, tn), lambda j: (0, j)),
        out_shape=jax.ShapeDtypeStruct((nb, n), F32),
        compiler_params=_cparams(("arbitrary",)),
        name="ada",
    )(c, w_ada, b_ada.reshape(1, n))


def _norm_kernel(x_ref, mod_ref, g_ref, p1_ref, p2_ref, h0_ref, h1_ref, h2_ref):
    blk = p1_ref.shape[0]
    for r in range(x_ref.shape[0] // blk):
        rows = pl.ds(r * blk, blk)
        y = _rms(x_ref[rows, :]) * g_ref[...]
        h = (y * (1.0 + mod_ref[1:2, :]) + mod_ref[0:1, :]).astype(BF16)
        h0_ref[rows, :] = h
        h1_ref[rows, :] = _dot(p1_ref[...], h).astype(BF16)
        h2_ref[rows, :] = _dot(p2_ref[...], h).astype(BF16)


def _residue_major_perm(rows, dil):
    dst = jnp.arange(rows)
    src = (dst % (rows // dil)) * dil + dst // (rows // dil)
    return (jnp.arange(rows)[None, :] == src[:, None]).astype(BF16)


def _norm(x, mod, g_mix, blk):
    b, s, d = x.shape
    m = b * s
    tm = min(1024, s)
    nsb = s // tm
    row = pl.BlockSpec((tm, d), lambda i: (i, 0))
    const = lambda shape: pl.BlockSpec(shape, lambda i: (0, 0), pipeline_mode=pl.Buffered(1))
    perms = [_residue_major_perm(blk, dil) for _, dil in DL_PAIRS[1:]]
    return pl.pallas_call(
        _norm_kernel,
        grid=(m // tm,),
        in_specs=[row, pl.BlockSpec((None, 6, d), lambda i: (i // nsb, 0, 0)), const((1, d)),
                  const((blk, blk)), const((blk, blk))],
        out_specs=[row, row, row],
        out_shape=[jax.ShapeDtypeStruct((m, d), BF16)] * 3,
        compiler_params=_cparams(("parallel",)),
        name="norm",
    )(x.reshape(m, d), mod, g_mix.reshape(1, d), *perms)


def _project(h_ref, w_ref, tab_ref, scale, emit, row_chunk):
    for r in range(h_ref.shape[0] // row_chunk):
        rows = pl.ds(r * row_chunk, row_chunk)
        z = _dot(h_ref[rows, :], w_ref[...])
        if scale is not None:
            cs, sn = scale(tab_ref[0, rows, :], tab_ref[1, rows, :])
            z = jnp.concatenate(
                [zc * cs + pltpu.roll(zc, LANES // 2, 1) * sn
                 for zc in (z[:, cb * LANES:(cb + 1) * LANES] for cb in range(COL_TILE // LANES))],
                axis=1)
        emit(r * row_chunk, z.astype(BF16))


def _qkv_scale(kind):
    def scale(cs, sn):
        f = jnp.where(kind == 0, Q_SCALE, 1.0).astype(F32)
        return jnp.where(kind < 2, cs * f, 1.0), jnp.where(kind < 2, sn * f, 0.0)
    return scale


def _residue_major_emit(zb_ref, blk, row_chunk):
    dil = zb_ref.shape[0]
    n = blk // dil
    piece = min(n, row_chunk)

    def emit(row0, z):
        for p in range(row_chunk // piece):
            r0 = row0 + p * piece
            pb, ro = divmod(r0, blk)
            zb_ref[ro // n, pb * n + ro % n:pb * n + ro % n + piece, :] = z[p * piece:(p + 1) * piece, :]
    return emit


_N_MAIN = CT_DL + 3
_N_MAIN_TILES = _N_MAIN + (N_COL_TILES - CT_GA)


def _proj_main_kernel(h_ref, w_ref, tab_ref, hm_ref, zb_ref, zg_ref, *, blk, row_chunk):
    j = pl.program_id(1)

    def head_major(row0, z):
        for hh in range(COL_TILE // DA_PAIR):
            hm_ref[hh, row0:row0 + row_chunk, :] = z[:, hh * DA_PAIR:(hh + 1) * DA_PAIR]

    def gates(row0, z):
        zg_ref[row0:row0 + row_chunk, :] = z

    @pl.when(j < CT_KA)
    def _():
        _project(h_ref, w_ref, tab_ref, lambda cs, sn: (cs * Q_SCALE, sn * Q_SCALE), head_major, row_chunk)

    @pl.when((j >= CT_KA) & (j < CT_VA))
    def _():
        _project(h_ref, w_ref, tab_ref, lambda cs, sn: (cs, sn), head_major, row_chunk)

    @pl.when((j >= CT_VA) & (j < CT_DL))
    def _():
        _project(h_ref, w_ref, tab_ref, None, head_major, row_chunk)

    @pl.when((j >= CT_DL) & (j < _N_MAIN))
    def _():
        _project(h_ref, w_ref, tab_ref, _qkv_scale(j - CT_DL),
                 _residue_major_emit(zb_ref, blk, row_chunk), row_chunk)

    @pl.when(j >= _N_MAIN)
    def _():
        _project(h_ref, w_ref, tab_ref, None, gates, row_chunk)


def _proj_group_kernel(h_ref, w_ref, tab_ref, zb_ref, *, blk, row_chunk):
    _project(h_ref, w_ref, tab_ref, _qkv_scale(pl.program_id(1)),
             _residue_major_emit(zb_ref, blk, row_chunk), row_chunk)


def _rotary_lane_layout(w):
    d, n = w.shape
    blk = w.reshape(d, n // LANES, LANES)
    a, b, h = ROPE_HALF, ROPE_ROT, DL_HEAD_DIM
    swapped = jnp.concatenate([blk[..., :a], blk[..., h:h + a], blk[..., b:h], blk[..., a:b],
                               blk[..., h + a:]], axis=-1)
    per_tile = COL_TILE // LANES
    qk_tiles = [CT_QA, CT_QA + 1, CT_KA, CT_KA + 1] + [CT_DL + 3 * g + t for g in range(len(DL_PAIRS))
                                                      for t in (0, 1)]
    is_qk = jnp.zeros((n // LANES,), bool).at[
        jnp.asarray([t * per_tile + c for t in qk_tiles for c in range(per_tile)])].set(True)
    return jnp.where(is_qk[None, :, None], swapped, blk).reshape(d, n)


def _first_head_lanes(lane):
    return (lane < ROPE_HALF) | ((lane >= ROPE_ROT) & (lane < DL_HEAD_DIM + ROPE_HALF))


def _rope_tables(seq):
    inv = 1.0 / (ROPE_THETA ** (jnp.arange(0, ROPE_ROT, 2, dtype=F32) / ROPE_ROT))
    ang = jnp.arange(seq, dtype=F32)[:, None] * inv[None, :]
    cos, sin = jnp.cos(ang), jnp.sin(ang)
    ones = jnp.ones((seq, DL_HEAD_DIM - ROPE_ROT), F32)
    zeros = jnp.zeros((seq, DL_HEAD_DIM - ROPE_ROT), F32)
    cs = jnp.concatenate([cos, cos, ones, cos, cos, ones], axis=1)
    sn = jnp.concatenate([-sin, -sin, zeros, sin, sin, zeros], axis=1)
    return jnp.stack([cs, sn])


def _residue_major_rows(tab, blk, dil):
    k, s, w = tab.shape
    return tab.reshape(k, s // blk, blk // dil, dil, w).transpose(0, 1, 3, 2, 4).reshape(k, s, w)


def _inproj(x, mod, g_mix, w_in_bf):
    b, s, d = x.shape
    m = b * s
    blk = min(256, s)
    tm = min(2048, s)
    nsb = s // tm
    row_chunk = min(256, tm)
    for _, dil in DL_PAIRS:
        assert s % (dil * DL_BLOCK) == 0, "sequence must tile into dilated blocks without padding"
    hs = _norm(x, mod, g_mix, blk)
    w = _rotary_lane_layout(w_in_bf)
    tab = _rope_tables(s)
    h_spec = pl.BlockSpec((tm, d), lambda i, j: (i, 0))
    tab_spec = pl.BlockSpec((2, tm, LANES), lambda i, j: (0, i % nsb, 0))

    def zb_shape(dil):
        return jax.ShapeDtypeStruct((b, dil, s // dil, 3 * DL_W), BF16)

    def zb_spec(dil, first):
        return pl.BlockSpec((None, dil, tm // dil, COL_TILE),
                            lambda i, j: (i // nsb, 0, i % nsb, jnp.clip(j - first, 0, 2)))

    hm, zb0, zg = pl.pallas_call(
        functools.partial(_proj_main_kernel, blk=blk, row_chunk=row_chunk),
        grid=(b * nsb, _N_MAIN_TILES),
        in_specs=[h_spec,
                  pl.BlockSpec((d, COL_TILE), lambda i, j: (0, jnp.where(j < _N_MAIN, j, j + CT_GA - _N_MAIN))),
                  tab_spec],
        out_specs=[pl.BlockSpec((None, COL_TILE // DA_PAIR, tm, DA_PAIR),
                                lambda i, j: (i // nsb, jnp.minimum(j, CT_DL - 1), i % nsb, 0)),
                   zb_spec(1, CT_DL),
                   pl.BlockSpec((tm, COL_TILE), lambda i, j: (i, jnp.maximum(j - _N_MAIN, 0)))],
        out_shape=[jax.ShapeDtypeStruct((b, 3 * DA_HEADS, s, DA_PAIR), BF16), zb_shape(1),
                   jax.ShapeDtypeStruct((m, (N_COL_TILES - CT_GA) * COL_TILE), BF16)],
        compiler_params=_cparams(("parallel", "arbitrary")),
        name="proj_main",
    )(hs[0], w, tab)
    zbs = [zb0]
    for g in range(1, len(DL_PAIRS)):
        dil = DL_PAIRS[g][1]
        first = CT_DL + 3 * g
        zbs.append(pl.pallas_call(
            functools.partial(_proj_group_kernel, blk=blk, row_chunk=row_chunk),
            grid=(b * nsb, 3),
            in_specs=[h_spec, pl.BlockSpec((d, COL_TILE), lambda i, j, first=first: (0, first + j)), tab_spec],
            out_specs=zb_spec(dil, 0),
            out_shape=zb_shape(dil),
            compiler_params=_cparams(("parallel", "arbitrary")),
            name=f"proj_group{g}",
        )(hs[g], w, _residue_major_rows(tab, blk, dil)))
    return hm, zg, *zbs


def _diffattn_kernel(q_ref, k_ref, v_ref, lq1_ref, lk1_ref, lq2_ref, lk2_ref, gs_ref, o_ref,
                     v2_scr, *, tk, lambda_init):
    qi = pl.program_id(2)
    tq = q_ref.shape[0]
    seq = k_ref.shape[0]

    @pl.when(qi == 0)
    def _():
        v2_scr[:, :DA_PAIR] = v_ref[...]
        v2_scr[:, DA_PAIR:] = jnp.ones((seq, DA_PAIR), BF16)

    lane = lax.broadcasted_iota(jnp.int32, (tq, DA_PAIR), 1)
    q = q_ref[...]
    zero = jnp.zeros_like(q)
    first = _first_head_lanes(lane)
    qmaps = (jnp.where(first, q, zero), jnp.where(first, zero, q))
    m = [None, None]
    acc = [None, None]
    for kk in range(seq // tk):
        kb = k_ref[kk * tk:(kk + 1) * tk, :]
        vb = v2_scr[kk * tk:(kk + 1) * tk, :]
        for mp in range(2):
            s = _nt_dot(qmaps[mp], kb)
            m_cur = jnp.max(s, axis=1, keepdims=True)
            if kk == 0:
                m_new = jnp.broadcast_to(m_cur, (tq, LANES))
            else:
                m_new = jnp.maximum(m[mp], m_cur)
            p = jnp.exp2(s - jnp.tile(m_new, (1, tk // LANES)))
            pv = _dot(p.astype(BF16), vb)
            if kk == 0:
                acc[mp] = pv
            else:
                alpha = jnp.exp2(m[mp] - m_new)
                acc[mp] = acc[mp] * jnp.tile(alpha, (1, 2)) + pv
            m[mp] = m_new

    o1 = acc[0][:, :DA_PAIR] / acc[0][:, DA_PAIR:]
    o2 = acc[1][:, :DA_PAIR] / acc[1][:, DA_PAIR:]
    lam = (jnp.exp(jnp.sum(lq1_ref[...] * lk1_ref[...], axis=1, keepdims=True))
           - jnp.exp(jnp.sum(lq2_ref[...] * lk2_ref[...], axis=1, keepdims=True)) + lambda_init)
    o = o1 - lam * o2
    o_ref[...] = ((_rms(o) * gs_ref[...]) * (1.0 - lambda_init)).astype(BF16)


def _diffattn(hm, lam_q1, lam_k1, lam_q2, lam_k2, g_subln, lambda_init):
    b, nh3, seq, _ = hm.shape
    nh = nh3 // 3
    tq = seq if seq <= 2048 else 1024
    tk = min(512, seq)
    lam_spec = pl.BlockSpec((1, DA_HEAD_DIM), lambda bi, h, qi: (0, 0))
    return pl.pallas_call(
        functools.partial(_diffattn_kernel, tk=tk, lambda_init=lambda_init),
        grid=(b, nh, seq // tq),
        in_specs=[pl.BlockSpec((None, None, tq, DA_PAIR), lambda bi, h, qi: (bi, h, qi, 0)),
                  pl.BlockSpec((None, None, seq, DA_PAIR), lambda bi, h, qi: (bi, nh + h, 0, 0)),
                  pl.BlockSpec((None, None, seq, DA_PAIR), lambda bi, h, qi: (bi, 2 * nh + h, 0, 0)),
                  lam_spec, lam_spec, lam_spec, lam_spec,
                  pl.BlockSpec((1, DA_PAIR), lambda bi, h, qi: (0, 0))],
        out_specs=pl.BlockSpec((None, tq, DA_PAIR), lambda bi, h, qi: (bi, qi, h)),
        out_shape=jax.ShapeDtypeStruct((b, seq, nh * DA_PAIR), BF16),
        scratch_shapes=[pltpu.VMEM((seq, 2 * DA_PAIR), BF16)],
        compiler_params=_cparams(("parallel", "parallel", "arbitrary")),
        name="diffattn",
    )(hm, hm, hm, lam_q1.reshape(1, -1), lam_k1.reshape(1, -1), lam_q2.reshape(1, -1),
      lam_k2.reshape(1, -1), g_subln.reshape(1, -1))


def _dilated_kernel(q_ref, k_ref, v_ref, o_ref, lse_ref, *, sub, kw, radius):
    n = pl.program_id(2)
    tq = q_ref.shape[0]
    t = k_ref.shape[0]
    lane = lax.broadcasted_iota(jnp.int32, (sub, LANES), 1)
    low = lane < DL_HEAD_DIM
    first = _first_head_lanes(lane)
    zero = jnp.zeros((sub, LANES), BF16)
    ones = jnp.ones((kw, LANES), BF16)
    n_pairs = DL_W // LANES
    o_rows, lse_rows = [], []
    for sb in range(tq // sub):
        q0 = n * tq + sb * sub
        ks = pl.multiple_of(jnp.clip(q0 - radius, 0, t - kw), DL_BLOCK)
        qpos = lax.broadcasted_iota(jnp.int32, (sub, kw), 0) + q0
        kpos = lax.broadcasted_iota(jnp.int32, (sub, kw), 1) + ks
        band = jnp.abs(kpos - qpos) <= radius
        band = jnp.concatenate([band, band], axis=0)
        all_scores = []
        for j in range(n_pairs):
            cols = slice(j * LANES, (j + 1) * LANES)
            qp = q_ref[sb * sub:(sb + 1) * sub, cols]
            qcat = jnp.concatenate([jnp.where(first, qp, zero), jnp.where(first, zero, qp)], axis=0)
            all_scores.append(_nt_dot(qcat, k_ref[pl.ds(ks, kw), cols]))
        ms, dens, o_cols = [], [], []
        for j in range(n_pairs):
            cols = slice(j * LANES, (j + 1) * LANES)
            vp = jnp.concatenate([v_ref[pl.ds(ks, kw), cols], ones], axis=1)
            s = jnp.where(band, all_scores[j], NEG)
            m = jnp.max(s, axis=1, keepdims=True)
            p = jnp.exp2(s - m)
            r = _dot(p.astype(BF16), vp)
            den = r[:, LANES:]
            o2 = r[:, :LANES] / den
            ms += [m[:sub], m[sub:]]
            dens += [den[:sub], den[sub:]]
            o_cols.append(jnp.where(low, o2[:sub], o2[sub:]).astype(BF16))
        m_tile = jnp.zeros((sub, LANES), F32)
        den_tile = jnp.ones((sub, LANES), F32)
        for hd in range(len(ms)):
            m_tile = jnp.where(lane == hd, ms[hd], m_tile)
            den_tile = jnp.where(lane == hd, dens[hd], den_tile)
        lse_tile = m_tile * LN2 + jnp.log(den_tile)
        o_rows.append(jnp.concatenate(o_cols, axis=1))
        lse_rows.append(lse_tile)
    o_ref[...] = jnp.concatenate(o_rows, axis=0)
    lse_ref[...] = jnp.concatenate(lse_rows, axis=0)


def _dilated(zb, g, window, dil):
    b, _, t, _ = zb.shape
    radius = window // (2 * dil)
    tq = min(256, t)
    sub = min(2 * radius, tq)
    kw = min(sub + 2 * radius, t)
    return pl.pallas_call(
        functools.partial(_dilated_kernel, sub=sub, kw=kw, radius=radius),
        grid=(b, dil, t // tq),
        in_specs=[pl.BlockSpec((None, None, tq, DL_W), lambda bi, c, n: (bi, c, n, 0)),
                  pl.BlockSpec((None, None, t, DL_W), lambda bi, c, n: (bi, c, 0, 1)),
                  pl.BlockSpec((None, None, t, DL_W), lambda bi, c, n: (bi, c, 0, 2))],
        out_specs=[pl.BlockSpec((None, None, tq, DL_W), lambda bi, c, n: (bi, c, n, 0)),
                   pl.BlockSpec((None, None, tq, LANES), lambda bi, c, n: (bi, c, n, 0))],
        out_shape=[jax.ShapeDtypeStruct((b, dil, t, DL_W), BF16),
                   jax.ShapeDtypeStruct((b, dil, t, LANES), F32)],
        compiler_params=_cparams(("parallel", "parallel", "arbitrary")),
        name=f"dilated{g}",
    )(zb, zb, zb)


def _natural_order(ref, scr):
    dil, n, w = ref.shape
    slabs = []
    for cb in range(w // LANES):
        cols = slice(cb * LANES, (cb + 1) * LANES)
        if dil == 1:
            slabs.append(ref[0, :, cols].astype(F32))
            continue
        for c in range(dil):
            scr[cb, pl.ds(c, n, stride=dil), :] = ref[c, :, cols].astype(F32)
        slabs.append(scr[cb])
    return slabs


def _merge_kernel(x_ref, mod_ref, oa_ref, o0_ref, o1_ref, o2_ref, l0_ref, l1_ref, l2_ref,
                  ga_ref, gb_ref, wa_ref, wb_ref, wo_ref, ex_ref, g_ref, x1_ref, h2_ref,
                  os1_scr, os2_scr, ls1_scr, ls2_scr):
    l0, = _natural_order(l0_ref, None)
    l1, = _natural_order(l1_ref, ls1_scr)
    l2, = _natural_order(l2_ref, ls2_scr)
    o0 = _natural_order(o0_ref, None)
    o1 = _natural_order(o1_ref, os1_scr)
    o2 = _natural_order(o2_ref, os2_scr)
    mx = jnp.maximum(jnp.maximum(l0, l1), l2)
    e0, e1, e2 = jnp.exp(l0 - mx), jnp.exp(l1 - mx), jnp.exp(l2 - mx)
    inv = 1.0 / (e0 + e1 + e2)
    ob = None
    for e, o in ((e0, o0), (e1, o1), (e2, o2)):
        w = e * inv
        w_hi = w.astype(BF16)
        w_lo = (w - w_hi.astype(F32)).astype(BF16)
        wx = _dot(jnp.concatenate([w_hi, w_lo], axis=1), ex_ref[...])
        term = [wx[:, cb * LANES:(cb + 1) * LANES] * o[cb] for cb in range(len(o))]
        ob = term if ob is None else [a + t for a, t in zip(ob, term)]
    ob = jnp.concatenate(ob, axis=1)
    ya = _dot(oa_ref[...], wa_ref[...])
    yb = _dot(ob.astype(BF16), wb_ref[...])
    ga = ga_ref[...].astype(F32)
    gb = gb_ref[...].astype(F32)
    merged = ya / (1.0 + jnp.exp(-ga)) + yb / (1.0 + jnp.exp(-gb))
    y = _dot(merged.astype(BF16), wo_ref[...])
    x1 = x_ref[...] + mod_ref[2:3, :] * y
    x1_ref[...] = x1
    h2 = (_rms(x1) * g_ref[...]) * (1.0 + mod_ref[4:5, :]) + mod_ref[3:4, :]
    h2_ref[...] = h2.astype(BF16)


def _merge(x, mod, zg, oa, obs, lses, wa_bf, wb_bf, wo_bf, g_ffn):
    b, s, d = x.shape
    m = b * s
    tm = min(512, s)
    nsb = s // tm
    row = lambda w: pl.BlockSpec((tm, w), lambda i: (i, 0))
    full = lambda shape: pl.BlockSpec(shape, lambda i: (0, 0))

    def residue_major(dil, w):
        return pl.BlockSpec((None, dil, tm // dil, w), lambda i: (i // nsb, 0, i % nsb, 0))

    dils = [dil for _, dil in DL_PAIRS]
    head = jnp.arange(LANES)[:, None]
    col = jnp.arange(DL_W)[None, :]
    expand = ((col // DL_HEAD_DIM) == head).astype(BF16)
    expand = jnp.concatenate([expand, expand], axis=0)
    x1, h2 = pl.pallas_call(
        _merge_kernel,
        grid=(m // tm,),
        in_specs=[row(d), pl.BlockSpec((None, 6, d), lambda i: (i // nsb, 0, 0)),
                  row(DA_HEADS * DA_PAIR)]
                 + [residue_major(dil, DL_W) for dil in dils]
                 + [residue_major(dil, LANES) for dil in dils]
                 + [pl.BlockSpec((tm, d), lambda i: (i, 0)), pl.BlockSpec((tm, d), lambda i: (i, 1)),
                    full(wa_bf.shape), full(wb_bf.shape), full(wo_bf.shape), full(expand.shape),
                    full((1, d))],
        out_specs=[row(d), row(d)],
        out_shape=[jax.ShapeDtypeStruct((m, d), F32), jax.ShapeDtypeStruct((m, d), BF16)],
        scratch_shapes=[pltpu.VMEM((DL_W // LANES, tm, LANES), F32),
                        pltpu.VMEM((DL_W // LANES, tm, LANES), F32),
                        pltpu.VMEM((1, tm, LANES), F32), pltpu.VMEM((1, tm, LANES), F32)],
        compiler_params=_cparams(("parallel",)),
        name="merge",
    )(x.reshape(m, d), mod, oa.reshape(m, -1), *obs, *lses, zg, zg, wa_bf, wb_bf, wo_bf, expand,
      g_ffn.reshape(1, d))
    return x1, h2


def _ffn_kernel(h_ref, x1_ref, mod_ref, wg_ref, wu_ref, wd_ref, gf_ref, o_ref, *, chunk, final_norm):
    h = h_ref[...]
    hidden = wd_ref.shape[0]
    acc = None
    for c in range(hidden // chunk):
        cols = slice(c * chunk, (c + 1) * chunk)
        gate = _dot(h, wg_ref[:, cols])
        up = _dot(h, wu_ref[:, cols])
        f = (gate / (1.0 + jnp.exp(-gate))) * up
        part = _dot(f.astype(BF16), wd_ref[cols, :])
        acc = part if acc is None else acc + part
    x2 = x1_ref[...] + mod_ref[5:6, :] * acc
    o_ref[...] = _rms(x2) * gf_ref[...] if final_norm else x2


def _ffn(h2, x1, mod, s, wg_bf, wu_bf, wd_bf, g_final, final_norm):
    m, d = x1.shape
    tm = min(512, s)
    nsb = s // tm
    hidden = wd_bf.shape[0]
    chunk = hidden // 2
    assert chunk % LANES == 0
    row = pl.BlockSpec((tm, d), lambda i: (i, 0))
    full = lambda shape: pl.BlockSpec(shape, lambda i: (0, 0), pipeline_mode=pl.Buffered(1))
    return pl.pallas_call(
        functools.partial(_ffn_kernel, chunk=chunk, final_norm=final_norm),
        grid=(m // tm,),
        in_specs=[row, row, pl.BlockSpec((None, 6, d), lambda i: (i // nsb, 0, 0)),
                  full(wg_bf.shape), full(wu_bf.shape), full(wd_bf.shape), full((1, d))],
        out_specs=row,
        out_shape=jax.ShapeDtypeStruct((m, d), F32),
        compiler_params=_cparams(("parallel",)),
        name="ffn",
    )(h2, x1, mod, wg_bf, wu_bf, wd_bf, g_final.reshape(1, d))


def _trunk(x, mod_all, weights):
    (g_mix, w_in, lam_q1, lam_k1, lam_q2, lam_k2, g_subln, w_a, w_b, w_out, g_ffn,
     w_gu, w_down, g_final) = weights
    b, s, d = x.shape
    depth = w_in.shape[0]
    hidden = w_down.shape[1]
    for l in range(depth):
        lambda_init = 0.8 - 0.6 * math.exp(-0.3 * l)
        mod = mod_all[l]
        hm, zg, *zbs = _inproj(x, mod, g_mix[l], w_in[l].astype(BF16))
        oa = _diffattn(hm, lam_q1[l], lam_k1[l], lam_q2[l], lam_k2[l], g_subln[l], lambda_init)
        obs, lses = [], []
        for g, (window, dil) in enumerate(DL_PAIRS):
            o, lse = _dilated(zbs[g], g, window, dil)
            obs.append(o)
            lses.append(lse)
        x1, h2 = _merge(x, mod, zg, oa, obs, lses, w_a[l].astype(BF16), w_b[l].astype(BF16),
                        w_out[l].astype(BF16), g_ffn[l])
        wgu = w_gu[l].astype(BF16)
        x = _ffn(h2, x1, mod, s, wgu[:, :hidden], wgu[:, hidden:], w_down[l].astype(BF16),
                 g_final, l == depth - 1).reshape(b, s, d)
    return x


def kernel(x_prompt, x_sample, c_prompt, c_sample, w_ada, b_ada, g_mix, w_in, lam_q1, lam_k1,
           lam_q2, lam_k2, g_subln, w_a, w_b, w_out, g_ffn, w_gu, w_down, g_final):
    depth = w_in.shape[0]
    bp = c_prompt.shape[0]
    c_all = jnp.concatenate([c_prompt, c_sample], axis=0)
    mods = [_ada(c_all, w_ada[l], b_ada[l]).reshape(c_all.shape[0], 6, D_MODEL) for l in range(depth)]
    weights = (g_mix, w_in, lam_q1, lam_k1, lam_q2, lam_k2, g_subln, w_a, w_b, w_out, g_ffn,
               w_gu, w_down, g_final)
    y_prompt = _trunk(x_prompt, [m[:bp] for m in mods], weights)
    y_sample = _trunk(x_sample, [m[bp:] for m in mods], weights)
    return (y_prompt, y_sample)
```

```python
import functools
import math

import jax
import jax.numpy as jnp
from jax import lax
from jax.experimental import pallas as pl
from jax.experimental.pallas import tpu as pltpu

F32 = jnp.float32
BF16 = jnp.bfloat16

D_MODEL = 1024
DA_HEADS = 8
DA_HEAD_DIM = 64
DA_PAIR = 2 * DA_HEAD_DIM
DL_PAIRS = ((128, 1), (512, 4), (2048, 16))
DL_HEADS = 8
DL_HEAD_DIM = 64
DL_W = DL_HEADS * DL_HEAD_DIM
DL_BLOCK = 64
ROPE_THETA = 500000.0
ROPE_ROT = DL_HEAD_DIM // 4
ROPE_HALF = ROPE_ROT // 2
EPS = 1e-6
NEG = -1e30
LOG2E = math.log2(math.e)
LN2 = math.log(2.0)
Q_SCALE = DA_HEAD_DIM ** -0.5 * LOG2E

LANES = 128
COL_TILE = 512
N_IN = 2 * DA_HEADS * DA_PAIR + DA_HEADS * DA_PAIR + 9 * DL_W + 2 * D_MODEL
N_COL_TILES = N_IN // COL_TILE
CT_QA, CT_KA, CT_VA = 0, 2, 4
CT_DL = 6
CT_GA = 15
VMEM_LIMIT = 56 * 1024 * 1024


def _cparams(sem):
    return pltpu.CompilerParams(dimension_semantics=sem, vmem_limit_bytes=VMEM_LIMIT)


def _nt_dot(a, b):
    return lax.dot_general(a, b, (((1,), (1,)), ((), ())), preferred_element_type=F32)


def _dot(a, b):
    return jnp.dot(a, b, preferred_element_type=F32)


def _rms(x):
    return x * lax.rsqrt(jnp.mean(x * x, axis=-1, keepdims=True) + EPS)


def _ada_kernel(c_ref, w_ref, b_ref, o_ref):
    c = c_ref[...]
    a = c * (1.0 / (1.0 + jnp.exp(-c)))
    w = w_ref[...]
    a_hi = a.astype(BF16)
    a_lo = (a - a_hi.astype(F32)).astype(BF16)
    w_hi = w.astype(BF16)
    w_lo = (w - w_hi.astype(F32)).astype(BF16)
    o_ref[...] = _dot(a_hi, w_hi) + _dot(a_hi, w_lo) + _dot(a_lo, w_hi) + b_ref[...]


def _ada(c, w_ada, b_ada):
    nb, d = c.shape
    n = w_ada.shape[1]
    tn = COL_TILE
    return pl.pallas_call(
        _ada_kernel,
        grid=(n // tn,),
        in_specs=[pl.BlockSpec((nb, d), lambda j: (0, 0)),
                  pl.BlockSpec((d, tn), lambda j: (0, j)),
                  pl.BlockSpec((1, tn), lambda j: (0, j))],
        out_specs=pl.BlockSpec((nb, tn), lambda j: (0, j)),
        out_shape=jax.ShapeDtypeStruct((nb, n), F32),
        compiler_params=_cparams(("arbitrary",)),
        name="ada",
    )(c, w_ada, b_ada.reshape(1, n))


def _norm_kernel(x_ref, mod_ref, g_ref, p1_ref, p2_ref, h0_ref, h1_ref, h2_ref):
    blk = p1_ref.shape[0]
    for r in range(x_ref.shape[0] // blk):
        rows = pl.ds(r * blk, blk)
        y = _rms(x_ref[rows, :]) * g_ref[...]
        h = (y * (1.0 + mod_ref[1:2, :]) + mod_ref[0:1, :]).astype(BF16)
        h0_ref[rows, :] = h
        h1_ref[rows, :] = _dot(p1_ref[...], h).astype(BF16)
        h2_ref[rows, :] = _dot(p2_ref[...], h).astype(BF16)


def _residue_major_perm(rows, dil):
    dst = jnp.arange(rows)
    src = (dst % (rows // dil)) * dil + dst // (rows // dil)
    return (jnp.arange(rows)[None, :] == src[:, None]).astype(BF16)


def _norm(x, mod, g_mix, blk):
    b, s, d = x.shape
    m = b * s
    tm = min(1024, s)
    nsb = s // tm
    row = pl.BlockSpec((tm, d), lambda i: (i, 0))
    const = lambda shape: pl.BlockSpec(shape, lambda i: (0, 0), pipeline_mode=pl.Buffered(1))
    perms = [_residue_major_perm(blk, dil) for _, dil in DL_PAIRS[1:]]
    return pl.pallas_call(
        _norm_kernel,
        grid=(m // tm,),
        in_specs=[row, pl.BlockSpec((None, 6, d), lambda i: (i // nsb, 0, 0)), const((1, d)),
                  const((blk, blk)), const((blk, blk))],
        out_specs=[row, row, row],
        out_shape=[jax.ShapeDtypeStruct((m, d), BF16)] * 3,
        compiler_params=_cparams(("parallel",)),
        name="norm",
    )(x.reshape(m, d), mod, g_mix.reshape(1, d), *perms)


def _project(h_ref, w_ref, tab_ref, scale, emit, row_chunk):
    for r in range(h_ref.shape[0] // row_chunk):
        rows = pl.ds(r * row_chunk, row_chunk)
        z = _dot(h_ref[rows, :], w_ref[...])
        if scale is not None:
            cs, sn = scale(tab_ref[0, rows, :], tab_ref[1, rows, :])
            z = jnp.concatenate(
                [zc * cs + pltpu.roll(zc, LANES // 2, 1) * sn
                 for zc in (z[:, cb * LANES:(cb + 1) * LANES] for cb in range(COL_TILE // LANES))],
                axis=1)
        emit(r * row_chunk, z.astype(BF16))


def _qkv_scale(kind):
    def scale(cs, sn):
        f = jnp.where(kind == 0, Q_SCALE, 1.0).astype(F32)
        return jnp.where(kind < 2, cs * f, 1.0), jnp.where(kind < 2, sn * f, 0.0)
    return scale


def _residue_major_emit(zb_ref, blk, row_chunk):
    dil = zb_ref.shape[0]
    n = blk // dil
    piece = min(n, row_chunk)

    def emit(row0, z):
        for p in range(row_chunk // piece):
            r0 = row0 + p * piece
            pb, ro = divmod(r0, blk)
            zb_ref[ro // n, pb * n + ro % n:pb * n + ro % n + piece, :] = z[p * piece:(p + 1) * piece, :]
    return emit


_N_MAIN = CT_DL + 3
_N_MAIN_TILES = _N_MAIN + (N_COL_TILES - CT_GA)


def _proj_main_kernel(h_ref, w_ref, tab_ref, hm_ref, zb_ref, zg_ref, *, blk, row_chunk):
    j = pl.program_id(1)

    def head_major(row0, z):
        for hh in range(COL_TILE // DA_PAIR):
            hm_ref[hh, row0:row0 + row_chunk, :] = z[:, hh * DA_PAIR:(hh + 1) * DA_PAIR]

    def gates(row0, z):
        zg_ref[row0:row0 + row_chunk, :] = z

    @pl.when(j < CT_KA)
    def _():
        _project(h_ref, w_ref, tab_ref, lambda cs, sn: (cs * Q_SCALE, sn * Q_SCALE), head_major, row_chunk)

    @pl.when((j >= CT_KA) & (j < CT_VA))
    def _():
        _project(h_ref, w_ref, tab_ref, lambda cs, sn: (cs, sn), head_major, row_chunk)

    @pl.when((j >= CT_VA) & (j < CT_DL))
    def _():
        _project(h_ref, w_ref, tab_ref, None, head_major, row_chunk)

    @pl.when((j >= CT_DL) & (j < _N_MAIN))
    def _():
        _project(h_ref, w_ref, tab_ref, _qkv_scale(j - CT_DL),
                 _residue_major_emit(zb_ref, blk, row_chunk), row_chunk)

    @pl.when(j >= _N_MAIN)
    def _():
        _project(h_ref, w_ref, tab_ref, None, gates, row_chunk)


def _proj_group_kernel(h_ref, w_ref, tab_ref, zb_ref, *, blk, row_chunk):
    _project(h_ref, w_ref, tab_ref, _qkv_scale(pl.program_id(1)),
             _residue_major_emit(zb_ref, blk, row_chunk), row_chunk)


def _rotary_lane_layout(w):
    d, n = w.shape
    blk = w.reshape(d, n // LANES, LANES)
    a, b, h = ROPE_HALF, ROPE_ROT, DL_HEAD_DIM
    swapped = jnp.concatenate([blk[..., :a], blk[..., h:h + a], blk[..., b:h], blk[..., a:b],
                               blk[..., h + a:]], axis=-1)
    per_tile = COL_TILE // LANES
    qk_tiles = [CT_QA, CT_QA + 1, CT_KA, CT_KA + 1] + [CT_DL + 3 * g + t for g in range(len(DL_PAIRS))
                                                      for t in (0, 1)]
    is_qk = jnp.zeros((n // LANES,), bool).at[
        jnp.asarray([t * per_tile + c for t in qk_tiles for c in range(per_tile)])].set(True)
    return jnp.where(is_qk[None, :, None], swapped, blk).reshape(d, n)


def _first_head_lanes(lane):
    return (lane < ROPE_HALF) | ((lane >= ROPE_ROT) & (lane < DL_HEAD_DIM + ROPE_HALF))


def _rope_tables(seq):
    inv = 1.0 / (ROPE_THETA ** (jnp.arange(0, ROPE_ROT, 2, dtype=F32) / ROPE_ROT))
    ang = jnp.arange(seq, dtype=F32)[:, None] * inv[None, :]
    cos, sin = jnp.cos(ang), jnp.sin(ang)
    ones = jnp.ones((seq, DL_HEAD_DIM - ROPE_ROT), F32)
    zeros = jnp.zeros((seq, DL_HEAD_DIM - ROPE_ROT), F32)
    cs = jnp.concatenate([cos, cos, ones, cos, cos, ones], axis=1)
    sn = jnp.concatenate([-sin, -sin, zeros, sin, sin, zeros], axis=1)
    return jnp.stack([cs, sn])


def _residue_major_rows(tab, blk, dil):
    k, s, w = tab.shape
    return tab.reshape(k, s // blk, blk // dil, dil, w).transpose(0, 1, 3, 2, 4).reshape(k, s, w)


def _inproj(x, mod, g_mix, w_in_bf):
    b, s, d = x.shape
    m = b * s
    blk = min(256, s)
    tm = min(2048, s)
    nsb = s // tm
    row_chunk = min(256, tm)
    for _, dil in DL_PAIRS:
        assert s % (dil * DL_BLOCK) == 0, "sequence must tile into dilated blocks without padding"
    hs = _norm(x, mod, g_mix, blk)
    w = _rotary_lane_layout(w_in_bf)
    tab = _rope_tables(s)
    h_spec = pl.BlockSpec((tm, d), lambda i, j: (i, 0))
    tab_spec = pl.BlockSpec((2, tm, LANES), lambda i, j: (0, i % nsb, 0))

    def zb_shape(dil):
        return jax.ShapeDtypeStruct((b, dil, s // dil, 3 * DL_W), BF16)

    def zb_spec(dil, first):
        return pl.BlockSpec((None, dil, tm // dil, COL_TILE),
                            lambda i, j: (i // nsb, 0, i % nsb, jnp.clip(j - first, 0, 2)))

    hm, zb0, zg = pl.pallas_call(
        functools.partial(_proj_main_kernel, blk=blk, row_chunk=row_chunk),
        grid=(b * nsb, _N_MAIN_TILES),
        in_specs=[h_spec,
                  pl.BlockSpec((d, COL_TILE), lambda i, j: (0, jnp.where(j < _N_MAIN, j, j + CT_GA - _N_MAIN))),
                  tab_spec],
        out_specs=[pl.BlockSpec((None, COL_TILE // DA_PAIR, tm, DA_PAIR),
                                lambda i, j: (i // nsb, jnp.minimum(j, CT_DL - 1), i % nsb, 0)),
                   zb_spec(1, CT_DL),
                   pl.BlockSpec((tm, COL_TILE), lambda i, j: (i, jnp.maximum(j - _N_MAIN, 0)))],
        out_shape=[jax.ShapeDtypeStruct((b, 3 * DA_HEADS, s, DA_PAIR), BF16), zb_shape(1),
                   jax.ShapeDtypeStruct((m, (N_COL_TILES - CT_GA) * COL_TILE), BF16)],
        compiler_params=_cparams(("parallel", "arbitrary")),
        name="proj_main",
    )(hs[0], w, tab)
    zbs = [zb0]
    for g in range(1, len(DL_PAIRS)):
        dil = DL_PAIRS[g][1]
        first = CT_DL + 3 * g
        zbs.append(pl.pallas_call(
            functools.partial(_proj_group_kernel, blk=blk, row_chunk=row_chunk),
            grid=(b * nsb, 3),
            in_specs=[h_spec, pl.BlockSpec((d, COL_TILE), lambda i, j, first=first: (0, first + j)), tab_spec],
            out_specs=zb_spec(dil, 0),
            out_shape=zb_shape(dil),
            compiler_params=_cparams(("parallel", "arbitrary")),
            name=f"proj_group{g}",
        )(hs[g], w, _residue_major_rows(tab, blk, dil)))
    return hm, zg, *zbs


def _diffattn_kernel(q_ref, k_ref, v_ref, lq1_ref, lk1_ref, lq2_ref, lk2_ref, gs_ref, o_ref,
                     v2_scr, *, tk, lambda_init):
    qi = pl.program_id(2)
    tq = q_ref.shape[0]
    seq = k_ref.shape[0]

    @pl.when(qi == 0)
    def _():
        v2_scr[:, :DA_PAIR] = v_ref[...]
        v2_scr[:, DA_PAIR:] = jnp.ones((seq, DA_PAIR), BF16)

    lane = lax.broadcasted_iota(jnp.int32, (tq, DA_PAIR), 1)
    q = q_ref[...]
    zero = jnp.zeros_like(q)
    first = _first_head_lanes(lane)
    qmaps = (jnp.where(first, q, zero), jnp.where(first, zero, q))
    m = [None, None]
    acc = [None, None]
    for kk in range(seq // tk):
        kb = k_ref[kk * tk:(kk + 1) * tk, :]
        vb = v2_scr[kk * tk:(kk + 1) * tk, :]
        for mp in range(2):
            s = _nt_dot(qmaps[mp], kb)
            m_cur = jnp.max(s, axis=1, keepdims=True)
            if kk == 0:
                m_new = jnp.broadcast_to(m_cur, (tq, LANES))
            else:
                m_new = jnp.maximum(m[mp], m_cur)
            p = jnp.exp2(s - jnp.tile(m_new, (1, tk // LANES)))
            pv = _dot(p.astype(BF16), vb)
            if kk == 0:
                acc[mp] = pv
            else:
                alpha = jnp.exp2(m[mp] - m_new)
                acc[mp] = acc[mp] * jnp.tile(alpha, (1, 2)) + pv
            m[mp] = m_new

    o1 = acc[0][:, :DA_PAIR] / acc[0][:, DA_PAIR:]
    o2 = acc[1][:, :DA_PAIR] / acc[1][:, DA_PAIR:]
    lam = (jnp.exp(jnp.sum(lq1_ref[...] * lk1_ref[...], axis=1, keepdims=True))
           - jnp.exp(jnp.sum(lq2_ref[...] * lk2_ref[...], axis=1, keepdims=True)) + lambda_init)
    o = o1 - lam * o2
    o_ref[...] = ((_rms(o) * gs_ref[...]) * (1.0 - lambda_init)).astype(BF16)


def _diffattn(hm, lam_q1, lam_k1, lam_q2, lam_k2, g_subln, lambda_init):
    b, nh3, seq, _ = hm.shape
    nh = nh3 // 3
    tq = seq if seq <= 2048 else 1024
    tk = min(256, seq)
    lam_spec = pl.BlockSpec((1, DA_HEAD_DIM), lambda bi, h, qi: (0, 0))
    return pl.pallas_call(
        functools.partial(_diffattn_kernel, tk=tk, lambda_init=lambda_init),
        grid=(b, nh, seq // tq),
        in_specs=[pl.BlockSpec((None, None, tq, DA_PAIR), lambda bi, h, qi: (bi, h, qi, 0)),
                  pl.BlockSpec((None, None, seq, DA_PAIR), lambda bi, h, qi: (bi, nh + h, 0, 0)),
                  pl.BlockSpec((None, None, seq, DA_PAIR), lambda bi, h, qi: (bi, 2 * nh + h, 0, 0)),
                  lam_spec, lam_spec, lam_spec, lam_spec,
                  pl.BlockSpec((1, DA_PAIR), lambda bi, h, qi: (0, 0))],
        out_specs=pl.BlockSpec((None, tq, DA_PAIR), lambda bi, h, qi: (bi, qi, h)),
        out_shape=jax.ShapeDtypeStruct((b, seq, nh * DA_PAIR), BF16),
        scratch_shapes=[pltpu.VMEM((seq, 2 * DA_PAIR), BF16)],
        compiler_params=_cparams(("parallel", "parallel", "arbitrary")),
        name="diffattn",
    )(hm, hm, hm, lam_q1.reshape(1, -1), lam_k1.reshape(1, -1), lam_q2.reshape(1, -1),
      lam_k2.reshape(1, -1), g_subln.reshape(1, -1))


def _dilated_kernel(q_ref, k_ref, v_ref, o_ref, lse_ref, *, sub, kw, radius):
    n = pl.program_id(2)
    tq = q_ref.shape[0]
    t = k_ref.shape[0]
    lane = lax.broadcasted_iota(jnp.int32, (sub, LANES), 1)
    low = lane < DL_HEAD_DIM
    first = _first_head_lanes(lane)
    zero = jnp.zeros((sub, LANES), BF16)
    ones = jnp.ones((kw, LANES), BF16)
    n_pairs = DL_W // LANES
    o_rows, lse_rows = [], []
    for sb in range(tq // sub):
        q0 = n * tq + sb * sub
        ks = pl.multiple_of(jnp.clip(q0 - radius, 0, t - kw), DL_BLOCK)
        qpos = lax.broadcasted_iota(jnp.int32, (sub, kw), 0) + q0
        kpos = lax.broadcasted_iota(jnp.int32, (sub, kw), 1) + ks
        band = jnp.abs(kpos - qpos) <= radius
        band = jnp.concatenate([band, band], axis=0)
        all_scores = []
        for j in range(n_pairs):
            cols = slice(j * LANES, (j + 1) * LANES)
            qp = q_ref[sb * sub:(sb + 1) * sub, cols]
            qcat = jnp.concatenate([jnp.where(first, qp, zero), jnp.where(first, zero, qp)], axis=0)
            all_scores.append(_nt_dot(qcat, k_ref[pl.ds(ks, kw), cols]))
        ms, dens, o_cols = [], [], []
        for j in range(n_pairs):
            cols = slice(j * LANES, (j + 1) * LANES)
            vp = jnp.concatenate([v_ref[pl.ds(ks, kw), cols], ones], axis=1)
            s = jnp.where(band, all_scores[j], NEG)
            m = jnp.max(s, axis=1, keepdims=True)
            p = jnp.exp2(s - m)
            r = _dot(p.astype(BF16), vp)
            den = r[:, LANES:]
            o2 = r[:, :LANES] / den
            ms += [m[:sub], m[sub:]]
            dens += [den[:sub], den[sub:]]
            o_cols.append(jnp.where(low, o2[:sub], o2[sub:]).astype(BF16))
        m_tile = jnp.zeros((sub, LANES), F32)
        den_tile = jnp.ones((sub, LANES), F32)
        for hd in range(len(ms)):
            m_tile = jnp.where(lane == hd, ms[hd], m_tile)
            den_tile = jnp.where(lane == hd, dens[hd], den_tile)
        lse_tile = m_tile * LN2 + jnp.log(den_tile)
        o_rows.append(jnp.concatenate(o_cols, axis=1))
        lse_rows.append(lse_tile)
    o_ref[...] = jnp.concatenate(o_rows, axis=0)
    lse_ref[...] = jnp.concatenate(lse_rows, axis=0)


def _dilated(zb, g, window, dil):
    b, _, t, _ = zb.shape
    radius = window // (2 * dil)
    tq = min(256, t)
    sub = min(2 * radius, tq)
    kw = min(sub + 2 * radius, t)
    return pl.pallas_call(
        functools.partial(_dilated_kernel, sub=sub, kw=kw, radius=radius),
        grid=(b, dil, t // tq),
        in_specs=[pl.BlockSpec((None, None, tq, DL_W), lambda bi, c, n: (bi, c, n, 0)),
                  pl.BlockSpec((None, None, t, DL_W), lambda bi, c, n: (bi, c, 0, 1)),
                  pl.BlockSpec((None, None, t, DL_W), lambda bi, c, n: (bi, c, 0, 2))],
        out_specs=[pl.BlockSpec((None, None, tq, DL_W), lambda bi, c, n: (bi, c, n, 0)),
                   pl.BlockSpec((None, None, tq, LANES), lambda bi, c, n: (bi, c, n, 0))],
        out_shape=[jax.ShapeDtypeStruct((b, dil, t, DL_W), BF16),
                   jax.ShapeDtypeStruct((b, dil, t, LANES), F32)],
        compiler_params=_cparams(("parallel", "parallel", "arbitrary")),
        name=f"dilated{g}",
    )(zb, zb, zb)


def _natural_order(ref, scr):
    dil, n, w = ref.shape
    slabs = []
    for cb in range(w // LANES):
        cols = slice(cb * LANES, (cb + 1) * LANES)
        if dil == 1:
            slabs.append(ref[0, :, cols].astype(F32))
            continue
        for c in range(dil):
            scr[cb, pl.ds(c, n, stride=dil), :] = ref[c, :, cols].astype(F32)
        slabs.append(scr[cb])
    return slabs


def _merge_kernel(x_ref, mod_ref, oa_ref, o0_ref, o1_ref, o2_ref, l0_ref, l1_ref, l2_ref,
                  ga_ref, gb_ref, wa_ref, wb_ref, wo_ref, ex_ref, g_ref, x1_ref, h2_ref,
                  os1_scr, os2_scr, ls1_scr, ls2_scr):
    l0, = _natural_order(l0_ref, None)
    l1, = _natural_order(l1_ref, ls1_scr)
    l2, = _natural_order(l2_ref, ls2_scr)
    o0 = _natural_order(o0_ref, None)
    o1 = _natural_order(o1_ref, os1_scr)
    o2 = _natural_order(o2_ref, os2_scr)
    mx = jnp.maximum(jnp.maximum(l0, l1), l2)
    e0, e1, e2 = jnp.exp(l0 - mx), jnp.exp(l1 - mx), jnp.exp(l2 - mx)
    inv = 1.0 / (e0 + e1 + e2)
    ob = None
    for e, o in ((e0, o0), (e1, o1), (e2, o2)):
        w = e * inv
        w_hi = w.astype(BF16)
        w_lo = (w - w_hi.astype(F32)).astype(BF16)
        wx = _dot(jnp.concatenate([w_hi, w_lo], axis=1), ex_ref[...])
        term = [wx[:, cb * LANES:(cb + 1) * LANES] * o[cb] for cb in range(len(o))]
        ob = term if ob is None else [a + t for a, t in zip(ob, term)]
    ob = jnp.concatenate(ob, axis=1)
    ya = _dot(oa_ref[...], wa_ref[...])
    yb = _dot(ob.astype(BF16), wb_ref[...])
    ga = ga_ref[...].astype(F32)
    gb = gb_ref[...].astype(F32)
    merged = ya / (1.0 + jnp.exp(-ga)) + yb / (1.0 + jnp.exp(-gb))
    y = _dot(merged.astype(BF16), wo_ref[...])
    x1 = x_ref[...] + mod_ref[2:3, :] * y
    x1_ref[...] = x1
    h2 = (_rms(x1) * g_ref[...]) * (1.0 + mod_ref[4:5, :]) + mod_ref[3:4, :]
    h2_ref[...] = h2.astype(BF16)


def _merge(x, mod, zg, oa, obs, lses, wa_bf, wb_bf, wo_bf, g_ffn):
    b, s, d = x.shape
    m = b * s
    tm = min(512, s)
    nsb = s // tm
    row = lambda w: pl.BlockSpec((tm, w), lambda i: (i, 0))
    full = lambda shape: pl.BlockSpec(shape, lambda i: (0, 0))

    def residue_major(dil, w):
        return pl.BlockSpec((None, dil, tm // dil, w), lambda i: (i // nsb, 0, i % nsb, 0))

    dils = [dil for _, dil in DL_PAIRS]
    head = jnp.arange(LANES)[:, None]
    col = jnp.arange(DL_W)[None, :]
    expand = ((col // DL_HEAD_DIM) == head).astype(BF16)
    expand = jnp.concatenate([expand, expand], axis=0)
    x1, h2 = pl.pallas_call(
        _merge_kernel,
        grid=(m // tm,),
        in_specs=[row(d), pl.BlockSpec((None, 6, d), lambda i: (i // nsb, 0, 0)),
                  row(DA_HEADS * DA_PAIR)]
                 + [residue_major(dil, DL_W) for dil in dils]
                 + [residue_major(dil, LANES) for dil in dils]
                 + [pl.BlockSpec((tm, d), lambda i: (i, 0)), pl.BlockSpec((tm, d), lambda i: (i, 1)),
                    full(wa_bf.shape), full(wb_bf.shape), full(wo_bf.shape), full(expand.shape),
                    full((1, d))],
        out_specs=[row(d), row(d)],
        out_shape=[jax.ShapeDtypeStruct((m, d), F32), jax.ShapeDtypeStruct((m, d), BF16)],
        scratch_shapes=[pltpu.VMEM((DL_W // LANES, tm, LANES), F32),
                        pltpu.VMEM((DL_W // LANES, tm, LANES), F32),
                        pltpu.VMEM((1, tm, LANES), F32), pltpu.VMEM((1, tm, LANES), F32)],
        compiler_params=_cparams(("parallel",)),
        name="merge",
    )(x.reshape(m, d), mod, oa.reshape(m, -1), *obs, *lses, zg, zg, wa_bf, wb_bf, wo_bf, expand,
      g_ffn.reshape(1, d))
    return x1, h2


def _ffn_kernel(h_ref, x1_ref, mod_ref, wg_ref, wu_ref, wd_ref, gf_ref, o_ref, *, chunk, final_norm):
    h = h_ref[...]
    hidden = wd_ref.shape[0]
    acc = None
    for c in range(hidden // chunk):
        cols = slice(c * chunk, (c + 1) * chunk)
        gate = _dot(h, wg_ref[:, cols])
        up = _dot(h, wu_ref[:, cols])
        f = (gate / (1.0 + jnp.exp(-gate))) * up
        part = _dot(f.astype(BF16), wd_ref[cols, :])
        acc = part if acc is None else acc + part
    x2 = x1_ref[...] + mod_ref[5:6, :] * acc
    o_ref[...] = _rms(x2) * gf_ref[...] if final_norm else x2


def _ffn(h2, x1, mod, s, wg_bf, wu_bf, wd_bf, g_final, final_norm):
    m, d = x1.shape
    tm = min(512, s)
    nsb = s // tm
    hidden = wd_bf.shape[0]
    chunk = hidden // 2
    assert chunk % LANES == 0
    row = pl.BlockSpec((tm, d), lambda i: (i, 0))
    full = lambda shape: pl.BlockSpec(shape, lambda i: (0, 0), pipeline_mode=pl.Buffered(1))
    return pl.pallas_call(
        functools.partial(_ffn_kernel, chunk=chunk, final_norm=final_norm),
        grid=(m // tm,),
        in_specs=[row, row, pl.BlockSpec((None, 6, d), lambda i: (i // nsb, 0, 0)),
                  full(wg_bf.shape), full(wu_bf.shape), full(wd_bf.shape), full((1, d))],
        out_specs=row,
        out_shape=jax.ShapeDtypeStruct((m, d), F32),
        compiler_params=_cparams(("parallel",)),
        name="ffn",
    )(h2, x1, mod, wg_bf, wu_bf, wd_bf, g_final.reshape(1, d))


def _trunk(x, mod_all, weights):
    (g_mix, w_in, lam_q1, lam_k1, lam_q2, lam_k2, g_subln, w_a, w_b, w_out, g_ffn,
     w_gu, w_down, g_final) = weights
    b, s, d = x.shape
    depth = w_in.shape[0]
    hidden = w_down.shape[1]
    for l in range(depth):
        lambda_init = 0.8 - 0.6 * math.exp(-0.3 * l)
        mod = mod_all[l]
        hm, zg, *zbs = _inproj(x, mod, g_mix[l], w_in[l].astype(BF16))
        oa = _diffattn(hm, lam_q1[l], lam_k1[l], lam_q2[l], lam_k2[l], g_subln[l], lambda_init)
        obs, lses = [], []
        for g, (window, dil) in enumerate(DL_PAIRS):
            o, lse = _dilated(zbs[g], g, window, dil)
            obs.append(o)
            lses.append(lse)
        x1, h2 = _merge(x, mod, zg, oa, obs, lses, w_a[l].astype(BF16), w_b[l].astype(BF16),
                        w_out[l].astype(BF16), g_ffn[l])
        wgu = w_gu[l].astype(BF16)
        x = _ffn(h2, x1, mod, s, wgu[:, :hidden], wgu[:, hidden:], w_down[l].astype(BF16),
                 g_final, l == depth - 1).reshape(b, s, d)
    return x


def kernel(x_prompt, x_sample, c_prompt, c_sample, w_ada, b_ada, g_mix, w_in, lam_q1, lam_k1,
           lam_q2, lam_k2, g_subln, w_a, w_b, w_out, g_ffn, w_gu, w_down, g_final):
    depth = w_in.shape[0]
    bp = c_prompt.shape[0]
    c_all = jnp.concatenate([c_prompt, c_sample], axis=0)
    mods = [_ada(c_all, w_ada[l], b_ada[l]).reshape(c_all.shape[0], 6, D_MODEL) for l in range(depth)]
    weights = (g_mix, w_in, lam_q1, lam_k1, lam_q2, lam_k2, g_subln, w_a, w_b, w_out, g_ffn,
               w_gu, w_down, g_final)
    y_prompt = _trunk(x_prompt, [m[:bp] for m in mods], weights)
    y_sample = _trunk(x_sample, [m[bp:] for m in mods], weights)
    return (y_prompt, y_sample)
```

```python
import functools
import math

import jax
import jax.numpy as jnp
from jax import lax
from jax.experimental import pallas as pl
from jax.experimental.pallas import tpu as pltpu

F32 = jnp.float32
BF16 = jnp.bfloat16

D_MODEL = 1024
DA_HEADS = 8
DA_HEAD_DIM = 64
DA_PAIR = 2 * DA_HEAD_DIM
DL_PAIRS = ((128, 1), (512, 4), (2048, 16))
DL_HEADS = 8
DL_HEAD_DIM = 64
DL_W = DL_HEADS * DL_HEAD_DIM
DL_BLOCK = 64
ROPE_THETA = 500000.0
ROPE_ROT = DL_HEAD_DIM // 4
ROPE_HALF = ROPE_ROT // 2
EPS = 1e-6
NEG = -1e30
LOG2E = math.log2(math.e)
LN2 = math.log(2.0)
Q_SCALE = DA_HEAD_DIM ** -0.5 * LOG2E

LANES = 128
COL_TILE = 512
N_IN = 2 * DA_HEADS * DA_PAIR + DA_HEADS * DA_PAIR + 9 * DL_W + 2 * D_MODEL
N_COL_TILES = N_IN // COL_TILE
CT_QA, CT_KA, CT_VA = 0, 2, 4
CT_DL = 6
CT_GA = 15
VMEM_LIMIT = 56 * 1024 * 1024


def _cparams(sem):
    return pltpu.CompilerParams(dimension_semantics=sem, vmem_limit_bytes=VMEM_LIMIT)


def _nt_dot(a, b):
    return lax.dot_general(a, b, (((1,), (1,)), ((), ())), preferred_element_type=F32)


def _dot(a, b):
    return jnp.dot(a, b, preferred_element_type=F32)


def _rms(x):
    return x * lax.rsqrt(jnp.mean(x * x, axis=-1, keepdims=True) + EPS)


def _ada_kernel(c_ref, w_ref, b_ref, o_ref):
    c = c_ref[...]
    a = c * (1.0 / (1.0 + jnp.exp(-c)))
    w = w_ref[...]
    a_hi = a.astype(BF16)
    a_lo = (a - a_hi.astype(F32)).astype(BF16)
    w_hi = w.astype(BF16)
    w_lo = (w - w_hi.astype(F32)).astype(BF16)
    o_ref[...] = _dot(a_hi, w_hi) + _dot(a_hi, w_lo) + _dot(a_lo, w_hi) + b_ref[...]


def _ada(c, w_ada, b_ada):
    nb, d = c.shape
    n = w_ada.shape[1]
    tn = COL_TILE
    return pl.pallas_call(
        _ada_kernel,
        grid=(n // tn,),
        in_specs=[pl.BlockSpec((nb, d), lambda j: (0, 0)),
                  pl.BlockSpec((d, tn), lambda j: (0, j)),
                  pl.BlockSpec((1, tn), lambda j: (0, j))],
        out_specs=pl.BlockSpec((nb, tn), lambda j: (0, j)),
        out_shape=jax.ShapeDtypeStruct((nb, n), F32),
        compiler_params=_cparams(("arbitrary",)),
        name="ada",
    )(c, w_ada, b_ada.reshape(1, n))


def _norm_kernel(x_ref, mod_ref, g_ref, p1_ref, p2_ref, h0_ref, h1_ref, h2_ref):
    blk = p1_ref.shape[0]
    for r in range(x_ref.shape[0] // blk):
        rows = pl.ds(r * blk, blk)
        y = _rms(x_ref[rows, :]) * g_ref[...]
        h = (y * (1.0 + mod_ref[1:2, :]) + mod_ref[0:1, :]).astype(BF16)
        h0_ref[rows, :] = h
        h1_ref[rows, :] = _dot(p1_ref[...], h).astype(BF16)
        h2_ref[rows, :] = _dot(p2_ref[...], h).astype(BF16)


def _residue_major_perm(rows, dil):
    dst = jnp.arange(rows)
    src = (dst % (rows // dil)) * dil + dst // (rows // dil)
    return (jnp.arange(rows)[None, :] == src[:, None]).astype(BF16)


def _norm(x, mod, g_mix, blk):
    b, s, d = x.shape
    m = b * s
    tm = min(1024, s)
    nsb = s // tm
    row = pl.BlockSpec((tm, d), lambda i: (i, 0))
    const = lambda shape: pl.BlockSpec(shape, lambda i: (0, 0), pipeline_mode=pl.Buffered(1))
    perms = [_residue_major_perm(blk, dil) for _, dil in DL_PAIRS[1:]]
    return pl.pallas_call(
        _norm_kernel,
        grid=(m // tm,),
        in_specs=[row, pl.BlockSpec((None, 6, d), lambda i: (i // nsb, 0, 0)), const((1, d)),
                  const((blk, blk)), const((blk, blk))],
        out_specs=[row, row, row],
        out_shape=[jax.ShapeDtypeStruct((m, d), BF16)] * 3,
        compiler_params=_cparams(("parallel",)),
        name="norm",
    )(x.reshape(m, d), mod, g_mix.reshape(1, d), *perms)


def _project(h_ref, w_ref, tab_ref, scale, emit, row_chunk):
    for r in range(h_ref.shape[0] // row_chunk):
        rows = pl.ds(r * row_chunk, row_chunk)
        z = _dot(h_ref[rows, :], w_ref[...])
        if scale is not None:
            cs, sn = scale(tab_ref[0, rows, :], tab_ref[1, rows, :])
            z = jnp.concatenate(
                [zc * cs + pltpu.roll(zc, LANES // 2, 1) * sn
                 for zc in (z[:, cb * LANES:(cb + 1) * LANES] for cb in range(COL_TILE // LANES))],
                axis=1)
        emit(r * row_chunk, z.astype(BF16))


def _qkv_scale(kind):
    def scale(cs, sn):
        f = jnp.where(kind == 0, Q_SCALE, 1.0).astype(F32)
        return jnp.where(kind < 2, cs * f, 1.0), jnp.where(kind < 2, sn * f, 0.0)
    return scale


def _residue_major_emit(zb_ref, blk, row_chunk):
    dil = zb_ref.shape[0]
    n = blk // dil
    piece = min(n, row_chunk)

    def emit(row0, z):
        for p in range(row_chunk // piece):
            r0 = row0 + p * piece
            pb, ro = divmod(r0, blk)
            zb_ref[ro // n, pb * n + ro % n:pb * n + ro % n + piece, :] = z[p * piece:(p + 1) * piece, :]
    return emit


_N_MAIN = CT_DL + 3
_N_MAIN_TILES = _N_MAIN + (N_COL_TILES - CT_GA)


def _proj_main_kernel(h_ref, w_ref, tab_ref, hm_ref, zb_ref, zg_ref, *, blk, row_chunk):
    j = pl.program_id(1)

    def head_major(row0, z):
        for hh in range(COL_TILE // DA_PAIR):
            hm_ref[hh, row0:row0 + row_chunk, :] = z[:, hh * DA_PAIR:(hh + 1) * DA_PAIR]

    def gates(row0, z):
        zg_ref[row0:row0 + row_chunk, :] = z

    @pl.when(j < CT_KA)
    def _():
        _project(h_ref, w_ref, tab_ref, lambda cs, sn: (cs * Q_SCALE, sn * Q_SCALE), head_major, row_chunk)

    @pl.when((j >= CT_KA) & (j < CT_VA))
    def _():
        _project(h_ref, w_ref, tab_ref, lambda cs, sn: (cs, sn), head_major, row_chunk)

    @pl.when((j >= CT_VA) & (j < CT_DL))
    def _():
        _project(h_ref, w_ref, tab_ref, None, head_major, row_chunk)

    @pl.when((j >= CT_DL) & (j < _N_MAIN))
    def _():
        _project(h_ref, w_ref, tab_ref, _qkv_scale(j - CT_DL),
                 _residue_major_emit(zb_ref, blk, row_chunk), row_chunk)

    @pl.when(j >= _N_MAIN)
    def _():
        _project(h_ref, w_ref, tab_ref, None, gates, row_chunk)


def _proj_group_kernel(h_ref, w_ref, tab_ref, zb_ref, *, blk, row_chunk):
    _project(h_ref, w_ref, tab_ref, _qkv_scale(pl.program_id(1)),
             _residue_major_emit(zb_ref, blk, row_chunk), row_chunk)


def _rotary_lane_layout(w):
    d, n = w.shape
    blk = w.reshape(d, n // LANES, LANES)
    a, b, h = ROPE_HALF, ROPE_ROT, DL_HEAD_DIM
    swapped = jnp.concatenate([blk[..., :a], blk[..., h:h + a], blk[..., b:h], blk[..., a:b],
                               blk[..., h + a:]], axis=-1)
    per_tile = COL_TILE // LANES
    qk_tiles = [CT_QA, CT_QA + 1, CT_KA, CT_KA + 1] + [CT_DL + 3 * g + t for g in range(len(DL_PAIRS))
                                                      for t in (0, 1)]
    qk_blocks = {t * per_tile + c for t in qk_tiles for c in range(per_tile)}
    is_qk = jnp.asarray([i in qk_blocks for i in range(n // LANES)])
    return jnp.where(is_qk[None, :, None], swapped, blk).reshape(d, n)


def _first_head_lanes(lane):
    return (lane < ROPE_HALF) | ((lane >= ROPE_ROT) & (lane < DL_HEAD_DIM + ROPE_HALF))


def _rope_tables(seq):
    inv = 1.0 / (ROPE_THETA ** (jnp.arange(0, ROPE_ROT, 2, dtype=F32) / ROPE_ROT))
    ang = jnp.arange(seq, dtype=F32)[:, None] * inv[None, :]
    cos, sin = jnp.cos(ang), jnp.sin(ang)
    ones = jnp.ones((seq, DL_HEAD_DIM - ROPE_ROT), F32)
    zeros = jnp.zeros((seq, DL_HEAD_DIM - ROPE_ROT), F32)
    cs = jnp.concatenate([cos, cos, ones, cos, cos, ones], axis=1)
    sn = jnp.concatenate([-sin, -sin, zeros, sin, sin, zeros], axis=1)
    return jnp.stack([cs, sn])


def _residue_major_rows(tab, blk, dil):
    k, s, w = tab.shape
    return tab.reshape(k, s // blk, blk // dil, dil, w).transpose(0, 1, 3, 2, 4).reshape(k, s, w)


def _inproj(x, mod, g_mix, w_in):
    b, s, d = x.shape
    m = b * s
    blk = min(256, s)
    tm = min(2048, s)
    nsb = s // tm
    row_chunk = min(256, tm)
    for _, dil in DL_PAIRS:
        assert s % (dil * DL_BLOCK) == 0, "sequence must tile into dilated blocks without padding"
    hs = _norm(x, mod, g_mix, blk)
    w = _rotary_lane_layout(w_in).astype(BF16)
    tab = _rope_tables(s)
    h_spec = pl.BlockSpec((tm, d), lambda i, j: (i, 0))
    tab_spec = pl.BlockSpec((2, tm, LANES), lambda i, j: (0, i % nsb, 0))

    def zb_shape(dil):
        return jax.ShapeDtypeStruct((b, dil, s // dil, 3 * DL_W), BF16)

    def zb_spec(dil, first):
        return pl.BlockSpec((None, dil, tm // dil, COL_TILE),
                            lambda i, j: (i // nsb, 0, i % nsb, jnp.clip(j - first, 0, 2)))

    hm, zb0, zg = pl.pallas_call(
        functools.partial(_proj_main_kernel, blk=blk, row_chunk=row_chunk),
        grid=(b * nsb, _N_MAIN_TILES),
        in_specs=[h_spec,
                  pl.BlockSpec((d, COL_TILE), lambda i, j: (0, jnp.where(j < _N_MAIN, j, j + CT_GA - _N_MAIN))),
                  tab_spec],
        out_specs=[pl.BlockSpec((None, COL_TILE // DA_PAIR, tm, DA_PAIR),
                                lambda i, j: (i // nsb, jnp.minimum(j, CT_DL - 1), i % nsb, 0)),
                   zb_spec(1, CT_DL),
                   pl.BlockSpec((tm, COL_TILE), lambda i, j: (i, jnp.maximum(j - _N_MAIN, 0)))],
        out_shape=[jax.ShapeDtypeStruct((b, 3 * DA_HEADS, s, DA_PAIR), BF16), zb_shape(1),
                   jax.ShapeDtypeStruct((m, (N_COL_TILES - CT_GA) * COL_TILE), BF16)],
        compiler_params=_cparams(("parallel", "arbitrary")),
        name="proj_main",
    )(hs[0], w, tab)
    zbs = [zb0]
    for g in range(1, len(DL_PAIRS)):
        dil = DL_PAIRS[g][1]
        first = CT_DL + 3 * g
        zbs.append(pl.pallas_call(
            functools.partial(_proj_group_kernel, blk=blk, row_chunk=row_chunk),
            grid=(b * nsb, 3),
            in_specs=[h_spec, pl.BlockSpec((d, COL_TILE), lambda i, j, first=first: (0, first + j)), tab_spec],
            out_specs=zb_spec(dil, 0),
            out_shape=zb_shape(dil),
            compiler_params=_cparams(("parallel", "arbitrary")),
            name=f"proj_group{g}",
        )(hs[g], w, _residue_major_rows(tab, blk, dil)))
    return hm, zg, *zbs


def _diffattn_kernel(q_ref, k_ref, v_ref, lq1_ref, lk1_ref, lq2_ref, lk2_ref, gs_ref, o_ref,
                     v2_scr, *, tk, lambda_init):
    qi = pl.program_id(2)
    tq = q_ref.shape[0]
    seq = k_ref.shape[0]

    @pl.when(qi == 0)
    def _():
        v2_scr[:, :DA_PAIR] = v_ref[...]
        v2_scr[:, DA_PAIR:] = jnp.ones((seq, DA_PAIR), BF16)

    lane = lax.broadcasted_iota(jnp.int32, (tq, DA_PAIR), 1)
    q = q_ref[...]
    zero = jnp.zeros_like(q)
    first = _first_head_lanes(lane)
    qmaps = (jnp.where(first, q, zero), jnp.where(first, zero, q))
    m = [None, None]
    acc = [None, None]
    for kk in range(seq // tk):
        kb = k_ref[kk * tk:(kk + 1) * tk, :]
        vb = v2_scr[kk * tk:(kk + 1) * tk, :]
        for mp in range(2):
            s = _nt_dot(qmaps[mp], kb)
            m_cur = jnp.max(s, axis=1, keepdims=True)
            if kk == 0:
                m_new = jnp.broadcast_to(m_cur, (tq, LANES))
            else:
                m_new = jnp.maximum(m[mp], m_cur)
            p = jnp.exp2(s - jnp.tile(m_new, (1, tk // LANES)))
            pv = _dot(p.astype(BF16), vb)
            if kk == 0:
                acc[mp] = pv
            else:
                alpha = jnp.exp2(m[mp] - m_new)
                acc[mp] = acc[mp] * jnp.tile(alpha, (1, 2)) + pv
            m[mp] = m_new

    o1 = acc[0][:, :DA_PAIR] / acc[0][:, DA_PAIR:]
    o2 = acc[1][:, :DA_PAIR] / acc[1][:, DA_PAIR:]
    lam = (jnp.exp(jnp.sum(lq1_ref[...] * lk1_ref[...], axis=1, keepdims=True))
           - jnp.exp(jnp.sum(lq2_ref[...] * lk2_ref[...], axis=1, keepdims=True)) + lambda_init)
    o = o1 - lam * o2
    o_ref[...] = ((_rms(o) * gs_ref[...]) * (1.0 - lambda_init)).astype(BF16)


def _diffattn(hm, lam_q1, lam_k1, lam_q2, lam_k2, g_subln, lambda_init):
    b, nh3, seq, _ = hm.shape
    nh = nh3 // 3
    tq = seq if seq <= 2048 else 1024
    tk = min(256, seq)
    lam_spec = pl.BlockSpec((1, DA_HEAD_DIM), lambda bi, h, qi: (0, 0))
    return pl.pallas_call(
        functools.partial(_diffattn_kernel, tk=tk, lambda_init=lambda_init),
        grid=(b, nh, seq // tq),
        in_specs=[pl.BlockSpec((None, None, tq, DA_PAIR), lambda bi, h, qi: (bi, h, qi, 0)),
                  pl.BlockSpec((None, None, seq, DA_PAIR), lambda bi, h, qi: (bi, nh + h, 0, 0)),
                  pl.BlockSpec((None, None, seq, DA_PAIR), lambda bi, h, qi: (bi, 2 * nh + h, 0, 0)),
                  lam_spec, lam_spec, lam_spec, lam_spec,
                  pl.BlockSpec((1, DA_PAIR), lambda bi, h, qi: (0, 0))],
        out_specs=pl.BlockSpec((None, tq, DA_PAIR), lambda bi, h, qi: (bi, qi, h)),
        out_shape=jax.ShapeDtypeStruct((b, seq, nh * DA_PAIR), BF16),
        scratch_shapes=[pltpu.VMEM((seq, 2 * DA_PAIR), BF16)],
        compiler_params=_cparams(("parallel", "parallel", "arbitrary")),
        name="diffattn",
    )(hm, hm, hm, lam_q1.reshape(1, -1), lam_k1.reshape(1, -1), lam_q2.reshape(1, -1),
      lam_k2.reshape(1, -1), g_subln.reshape(1, -1))


def _dilated_kernel(q_ref, k_ref, v_ref, o_ref, lse_ref, *, sub, kw, radius):
    n = pl.program_id(2)
    n_cls, tq, _ = q_ref.shape
    t = k_ref.shape[1]
    lane = lax.broadcasted_iota(jnp.int32, (sub, LANES), 1)
    low = lane < DL_HEAD_DIM
    first = _first_head_lanes(lane)
    zero = jnp.zeros((sub, LANES), BF16)
    ones = jnp.ones((kw, LANES), BF16)
    n_pairs = DL_W // LANES
    for c in range(n_cls):
        o_rows, lse_rows = [], []
        for sb in range(tq // sub):
            q0 = n * tq + sb * sub
            ks = pl.multiple_of(jnp.clip(q0 - radius, 0, t - kw), DL_BLOCK)
            qpos = lax.broadcasted_iota(jnp.int32, (sub, kw), 0) + q0
            kpos = lax.broadcasted_iota(jnp.int32, (sub, kw), 1) + ks
            band = jnp.abs(kpos - qpos) <= radius
            band = jnp.concatenate([band, band], axis=0)
            all_scores = []
            for j in range(n_pairs):
                cols = slice(j * LANES, (j + 1) * LANES)
                qp = q_ref[c, sb * sub:(sb + 1) * sub, cols]
                qcat = jnp.concatenate([jnp.where(first, qp, zero), jnp.where(first, zero, qp)], axis=0)
                all_scores.append(_nt_dot(qcat, k_ref[c, pl.ds(ks, kw), cols]))
            ms, dens, o_cols = [], [], []
            for j in range(n_pairs):
                cols = slice(j * LANES, (j + 1) * LANES)
                vp = jnp.concatenate([v_ref[c, pl.ds(ks, kw), cols], ones], axis=1)
                s = jnp.where(band, all_scores[j], NEG)
                m = jnp.max(s, axis=1, keepdims=True)
                p = jnp.exp2(s - m)
                r = _dot(p.astype(BF16), vp)
                den = r[:, LANES:]
                o2 = r[:, :LANES] / den
                ms += [m[:sub], m[sub:]]
                dens += [den[:sub], den[sub:]]
                o_cols.append(jnp.where(low, o2[:sub], o2[sub:]).astype(BF16))
            m_tile = jnp.zeros((sub, LANES), F32)
            den_tile = jnp.ones((sub, LANES), F32)
            for hd in range(len(ms)):
                m_tile = jnp.where(lane == hd, ms[hd], m_tile)
                den_tile = jnp.where(lane == hd, dens[hd], den_tile)
            lse_tile = m_tile * LN2 + jnp.log(den_tile)
            o_rows.append(jnp.concatenate(o_cols, axis=1))
            lse_rows.append(lse_tile)
        o_ref[c] = jnp.concatenate(o_rows, axis=0)
        lse_ref[c] = jnp.concatenate(lse_rows, axis=0)


def _dilated(zb, g, window, dil):
    b, _, t, _ = zb.shape
    radius = window // (2 * dil)
    rows = min(1024, dil * t)
    tq = min(rows, t)
    n_cls = rows // tq
    sub = min(2 * radius, tq)
    kw = min(sub + 2 * radius, t)
    return pl.pallas_call(
        functools.partial(_dilated_kernel, sub=sub, kw=kw, radius=radius),
        grid=(b, dil // n_cls, t // tq),
        in_specs=[pl.BlockSpec((None, n_cls, tq, DL_W), lambda bi, c, n: (bi, c, n, 0)),
                  pl.BlockSpec((None, n_cls, t, DL_W), lambda bi, c, n: (bi, c, 0, 1)),
                  pl.BlockSpec((None, n_cls, t, DL_W), lambda bi, c, n: (bi, c, 0, 2))],
        out_specs=[pl.BlockSpec((None, n_cls, tq, DL_W), lambda bi, c, n: (bi, c, n, 0)),
                   pl.BlockSpec((None, n_cls, tq, LANES), lambda bi, c, n: (bi, c, n, 0))],
        out_shape=[jax.ShapeDtypeStruct((b, dil, t, DL_W), BF16),
                   jax.ShapeDtypeStruct((b, dil, t, LANES), F32)],
        compiler_params=_cparams(("parallel", "parallel", "arbitrary")),
        name=f"dilated{g}",
    )(zb, zb, zb)


def _natural_order(ref, scr):
    dil, n, w = ref.shape
    slabs = []
    for cb in range(w // LANES):
        cols = slice(cb * LANES, (cb + 1) * LANES)
        if dil == 1:
            slabs.append(ref[0, :, cols].astype(F32))
            continue
        for c in range(dil):
            scr[cb, pl.ds(c, n, stride=dil), :] = ref[c, :, cols].astype(F32)
        slabs.append(scr[cb])
    return slabs


def _merge_kernel(x_ref, mod_ref, oa_ref, o0_ref, o1_ref, o2_ref, l0_ref, l1_ref, l2_ref,
                  ga_ref, gb_ref, wa_ref, wb_ref, wo_ref, ex_ref, g_ref, x1_ref, h2_ref,
                  os1_scr, os2_scr, ls1_scr, ls2_scr):
    l0, = _natural_order(l0_ref, None)
    l1, = _natural_order(l1_ref, ls1_scr)
    l2, = _natural_order(l2_ref, ls2_scr)
    o0 = _natural_order(o0_ref, None)
    o1 = _natural_order(o1_ref, os1_scr)
    o2 = _natural_order(o2_ref, os2_scr)
    mx = jnp.maximum(jnp.maximum(l0, l1), l2)
    e0, e1, e2 = jnp.exp(l0 - mx), jnp.exp(l1 - mx), jnp.exp(l2 - mx)
    inv = 1.0 / (e0 + e1 + e2)
    ob = None
    for e, o in ((e0, o0), (e1, o1), (e2, o2)):
        w = e * inv
        w_hi = w.astype(BF16)
        w_lo = (w - w_hi.astype(F32)).astype(BF16)
        wx = _dot(jnp.concatenate([w_hi, w_lo], axis=1), ex_ref[...])
        term = [wx[:, cb * LANES:(cb + 1) * LANES] * o[cb] for cb in range(len(o))]
        ob = term if ob is None else [a + t for a, t in zip(ob, term)]
    ob = jnp.concatenate(ob, axis=1)
    ya = _dot(oa_ref[...], wa_ref[...])
    yb = _dot(ob.astype(BF16), wb_ref[...])
    ga = ga_ref[...].astype(F32)
    gb = gb_ref[...].astype(F32)
    merged = ya / (1.0 + jnp.exp(-ga)) + yb / (1.0 + jnp.exp(-gb))
    y = _dot(merged.astype(BF16), wo_ref[...])
    x1 = x_ref[...] + mod_ref[2:3, :] * y
    x1_ref[...] = x1
    h2 = (_rms(x1) * g_ref[...]) * (1.0 + mod_ref[4:5, :]) + mod_ref[3:4, :]
    h2_ref[...] = h2.astype(BF16)


def _merge(x, mod, zg, oa, obs, lses, wa_bf, wb_bf, wo_bf, g_ffn):
    b, s, d = x.shape
    m = b * s
    tm = min(512, s)
    nsb = s // tm
    row = lambda w: pl.BlockSpec((tm, w), lambda i: (i, 0))
    full = lambda shape: pl.BlockSpec(shape, lambda i: (0, 0))

    def residue_major(dil, w):
        return pl.BlockSpec((None, dil, tm // dil, w), lambda i: (i // nsb, 0, i % nsb, 0))

    dils = [dil for _, dil in DL_PAIRS]
    head = jnp.arange(LANES)[:, None]
    col = jnp.arange(DL_W)[None, :]
    expand = ((col // DL_HEAD_DIM) == head).astype(BF16)
    expand = jnp.concatenate([expand, expand], axis=0)
    x1, h2 = pl.pallas_call(
        _merge_kernel,
        grid=(m // tm,),
        in_specs=[row(d), pl.BlockSpec((None, 6, d), lambda i: (i // nsb, 0, 0)),
                  row(DA_HEADS * DA_PAIR)]
                 + [residue_major(dil, DL_W) for dil in dils]
                 + [residue_major(dil, LANES) for dil in dils]
                 + [pl.BlockSpec((tm, d), lambda i: (i, 0)), pl.BlockSpec((tm, d), lambda i: (i, 1)),
                    full(wa_bf.shape), full(wb_bf.shape), full(wo_bf.shape), full(expand.shape),
                    full((1, d))],
        out_specs=[row(d), row(d)],
        out_shape=[jax.ShapeDtypeStruct((m, d), F32), jax.ShapeDtypeStruct((m, d), BF16)],
        scratch_shapes=[pltpu.VMEM((DL_W // LANES, tm, LANES), F32),
                        pltpu.VMEM((DL_W // LANES, tm, LANES), F32),
                        pltpu.VMEM((1, tm, LANES), F32), pltpu.VMEM((1, tm, LANES), F32)],
        compiler_params=_cparams(("parallel",)),
        name="merge",
    )(x.reshape(m, d), mod, oa.reshape(m, -1), *obs, *lses, zg, zg, wa_bf, wb_bf, wo_bf, expand,
      g_ffn.reshape(1, d))
    return x1, h2


def _ffn_kernel(h_ref, x1_ref, mod_ref, wgu_ref, wd_ref, gf_ref, o_ref, *, chunk, final_norm):
    h = h_ref[...]
    hidden = wd_ref.shape[0]
    acc = None
    for c in range(hidden // chunk):
        cols = slice(c * chunk, (c + 1) * chunk)
        gate = _dot(h, wgu_ref[:, cols])
        up = _dot(h, wgu_ref[:, hidden + c * chunk:hidden + (c + 1) * chunk])
        f = (gate / (1.0 + jnp.exp(-gate))) * up
        part = _dot(f.astype(BF16), wd_ref[cols, :])
        acc = part if acc is None else acc + part
    x2 = x1_ref[...] + mod_ref[5:6, :] * acc
    o_ref[...] = _rms(x2) * gf_ref[...] if final_norm else x2


def _ffn(h2, x1, mod, s, wgu_bf, wd_bf, g_final, final_norm):
    m, d = x1.shape
    tm = min(512, s)
    nsb = s // tm
    hidden = wd_bf.shape[0]
    chunk = hidden // 2
    assert chunk % LANES == 0
    row = pl.BlockSpec((tm, d), lambda i: (i, 0))
    full = lambda shape: pl.BlockSpec(shape, lambda i: (0, 0), pipeline_mode=pl.Buffered(1))
    return pl.pallas_call(
        functools.partial(_ffn_kernel, chunk=chunk, final_norm=final_norm),
        grid=(m // tm,),
        in_specs=[row, row, pl.BlockSpec((None, 6, d), lambda i: (i // nsb, 0, 0)),
                  full(wgu_bf.shape), full(wd_bf.shape), full((1, d))],
        out_specs=row,
        out_shape=jax.ShapeDtypeStruct((m, d), F32),
        compiler_params=_cparams(("parallel",)),
        name="ffn",
    )(h2, x1, mod, wgu_bf, wd_bf, g_final.reshape(1, d))


def _trunk(x, mod_all, weights):
    (g_mix, w_in, lam_q1, lam_k1, lam_q2, lam_k2, g_subln, w_a, w_b, w_out, g_ffn,
     w_gu, w_down, g_final) = weights
    b, s, d = x.shape
    depth = w_in.shape[0]
    for l in range(depth):
        lambda_init = 0.8 - 0.6 * math.exp(-0.3 * l)
        mod = mod_all[l]
        hm, zg, *zbs = _inproj(x, mod, g_mix[l], w_in[l])
        oa = _diffattn(hm, lam_q1[l], lam_k1[l], lam_q2[l], lam_k2[l], g_subln[l], lambda_init)
        obs, lses = [], []
        for g, (window, dil) in enumerate(DL_PAIRS):
            o, lse = _dilated(zbs[g], g, window, dil)
            obs.append(o)
            lses.append(lse)
        x1, h2 = _merge(x, mod, zg, oa, obs, lses, w_a[l].astype(BF16), w_b[l].astype(BF16),
                        w_out[l].astype(BF16), g_ffn[l])
        x = _ffn(h2, x1, mod, s, w_gu[l].astype(BF16), w_down[l].astype(BF16),
                 g_final, l == depth - 1).reshape(b, s, d)
    return x


def kernel(x_prompt, x_sample, c_prompt, c_sample, w_ada, b_ada, g_mix, w_in, lam_q1, lam_k1,
           lam_q2, lam_k2, g_subln, w_a, w_b, w_out, g_ffn, w_gu, w_down, g_final):
    depth = w_in.shape[0]
    bp = c_prompt.shape[0]
    c_all = jnp.concatenate([c_prompt, c_sample], axis=0)
    mods = [_ada(c_all, w_ada[l], b_ada[l]).reshape(c_all.shape[0], 6, D_MODEL) for l in range(depth)]
    weights = (g_mix, w_in, lam_q1, lam_k1, lam_q2, lam_k2, g_subln, w_a, w_b, w_out, g_ffn,
               w_gu, w_down, g_final)
    y_prompt = _trunk(x_prompt, [m[:bp] for m in mods], weights)
    y_sample = _trunk(x_sample, [m[bp:] for m in mods], weights)
    return (y_prompt, y_sample)
```

```python
import functools
import math

import jax
import jax.numpy as jnp
from jax import lax
from jax.experimental import pallas as pl
from jax.experimental.pallas import tpu as pltpu

F32 = jnp.float32
BF16 = jnp.bfloat16

D_MODEL = 1024
DA_HEADS = 8
DA_HEAD_DIM = 64
DA_PAIR = 2 * DA_HEAD_DIM
DL_PAIRS = ((128, 1), (512, 4), (2048, 16))
DL_HEADS = 8
DL_HEAD_DIM = 64
DL_W = DL_HEADS * DL_HEAD_DIM
DL_BLOCK = 64
ROPE_THETA = 500000.0
ROPE_ROT = DL_HEAD_DIM // 4
ROPE_HALF = ROPE_ROT // 2
EPS = 1e-6
NEG = -1e30
LOG2E = math.log2(math.e)
LN2 = math.log(2.0)
Q_SCALE = DA_HEAD_DIM ** -0.5 * LOG2E

LANES = 128
COL_TILE = 512
N_IN = 2 * DA_HEADS * DA_PAIR + DA_HEADS * DA_PAIR + 9 * DL_W + 2 * D_MODEL
N_COL_TILES = N_IN // COL_TILE
CT_QA, CT_KA, CT_VA = 0, 2, 4
CT_DL = 6
CT_GA = 15
VMEM_LIMIT = 56 * 1024 * 1024


def _cparams(sem):
    return pltpu.CompilerParams(dimension_semantics=sem, vmem_limit_bytes=VMEM_LIMIT)


def _nt_dot(a, b):
    return lax.dot_general(a, b, (((1,), (1,)), ((), ())), preferred_element_type=F32)


def _dot(a, b):
    return jnp.dot(a, b, preferred_element_type=F32)


def _rms(x):
    return x * lax.rsqrt(jnp.mean(x * x, axis=-1, keepdims=True) + EPS)


def _ada_kernel(c_ref, w_ref, b_ref, o_ref):
    c = c_ref[...]
    a = c * (1.0 / (1.0 + jnp.exp(-c)))
    w = w_ref[...]
    a_hi = a.astype(BF16)
    a_lo = (a - a_hi.astype(F32)).astype(BF16)
    w_hi = w.astype(BF16)
    w_lo = (w - w_hi.astype(F32)).astype(BF16)
    o_ref[...] = _dot(a_hi, w_hi) + _dot(a_hi, w_lo) + _dot(a_lo, w_hi) + b_ref[...]


def _ada(c, w_ada, b_ada):
    nb, d = c.shape
    n = w_ada.shape[1]
    tn = COL_TILE
    return pl.pallas_call(
        _ada_kernel,
        grid=(n // tn,),
        in_specs=[pl.BlockSpec((nb, d), lambda j: (0, 0)),
                  pl.BlockSpec((d, tn), lambda j: (0, j)),
                  pl.BlockSpec((1, tn), lambda j: (0, j))],
        out_specs=pl.BlockSpec((nb, tn), lambda j: (0, j)),
        out_shape=jax.ShapeDtypeStruct((nb, n), F32),
        compiler_params=_cparams(("arbitrary",)),
        name="ada",
    )(c, w_ada, b_ada.reshape(1, n))


def _norm_kernel(x_ref, mod_ref, g_ref, p1_ref, p2_ref, h0_ref, h1_ref, h2_ref):
    blk = p1_ref.shape[0]
    for r in range(x_ref.shape[0] // blk):
        rows = pl.ds(r * blk, blk)
        y = _rms(x_ref[rows, :]) * g_ref[...]
        h = (y * (1.0 + mod_ref[1:2, :]) + mod_ref[0:1, :]).astype(BF16)
        h0_ref[rows, :] = h
        h1_ref[rows, :] = _dot(p1_ref[...], h).astype(BF16)
        h2_ref[rows, :] = _dot(p2_ref[...], h).astype(BF16)


def _residue_major_perm(rows, dil):
    dst = jnp.arange(rows)
    src = (dst % (rows // dil)) * dil + dst // (rows // dil)
    return (jnp.arange(rows)[None, :] == src[:, None]).astype(BF16)


def _norm(x, mod, g_mix, blk):
    b, s, d = x.shape
    m = b * s
    tm = min(1024, s)
    nsb = s // tm
    row = pl.BlockSpec((tm, d), lambda i: (i, 0))
    const = lambda shape: pl.BlockSpec(shape, lambda i: (0, 0), pipeline_mode=pl.Buffered(1))
    perms = [_residue_major_perm(blk, dil) for _, dil in DL_PAIRS[1:]]
    return pl.pallas_call(
        _norm_kernel,
        grid=(m // tm,),
        in_specs=[row, pl.BlockSpec((None, 6, d), lambda i: (i // nsb, 0, 0)), const((1, d)),
                  const((blk, blk)), const((blk, blk))],
        out_specs=[row, row, row],
        out_shape=[jax.ShapeDtypeStruct((m, d), BF16)] * 3,
        compiler_params=_cparams(("parallel",)),
        name="norm",
    )(x.reshape(m, d), mod, g_mix.reshape(1, d), *perms)


def _project(h_ref, w_ref, tab_ref, scale, emit, row_chunk):
    for r in range(h_ref.shape[0] // row_chunk):
        rows = pl.ds(r * row_chunk, row_chunk)
        z = _dot(h_ref[rows, :], w_ref[...])
        if scale is not None:
            cs, sn = scale(tab_ref[0, rows, :], tab_ref[1, rows, :])
            z = jnp.concatenate(
                [zc * cs + pltpu.roll(zc, LANES // 2, 1) * sn
                 for zc in (z[:, cb * LANES:(cb + 1) * LANES] for cb in range(COL_TILE // LANES))],
                axis=1)
        emit(r * row_chunk, z.astype(BF16))


def _qkv_scale(kind):
    def scale(cs, sn):
        f = jnp.where(kind == 0, Q_SCALE, 1.0).astype(F32)
        return jnp.where(kind < 2, cs * f, 1.0), jnp.where(kind < 2, sn * f, 0.0)
    return scale


def _residue_major_emit(zb_ref, blk, row_chunk):
    dil = zb_ref.shape[0]
    n = blk // dil
    piece = min(n, row_chunk)

    def emit(row0, z):
        for p in range(row_chunk // piece):
            r0 = row0 + p * piece
            pb, ro = divmod(r0, blk)
            zb_ref[ro // n, pb * n + ro % n:pb * n + ro % n + piece, :] = z[p * piece:(p + 1) * piece, :]
    return emit


_N_MAIN = CT_DL + 3
_N_MAIN_TILES = _N_MAIN + (N_COL_TILES - CT_GA)


def _proj_main_kernel(h_ref, w_ref, tab_ref, hm_ref, zb_ref, zg_ref, *, blk, row_chunk):
    j = pl.program_id(1)

    def head_major(row0, z):
        for hh in range(COL_TILE // DA_PAIR):
            hm_ref[hh, row0:row0 + row_chunk, :] = z[:, hh * DA_PAIR:(hh + 1) * DA_PAIR]

    def gates(row0, z):
        zg_ref[row0:row0 + row_chunk, :] = z

    @pl.when(j < CT_KA)
    def _():
        _project(h_ref, w_ref, tab_ref, lambda cs, sn: (cs * Q_SCALE, sn * Q_SCALE), head_major, row_chunk)

    @pl.when((j >= CT_KA) & (j < CT_VA))
    def _():
        _project(h_ref, w_ref, tab_ref, lambda cs, sn: (cs, sn), head_major, row_chunk)

    @pl.when((j >= CT_VA) & (j < CT_DL))
    def _():
        _project(h_ref, w_ref, tab_ref, None, head_major, row_chunk)

    @pl.when((j >= CT_DL) & (j < _N_MAIN))
    def _():
        _project(h_ref, w_ref, tab_ref, _qkv_scale(j - CT_DL),
                 _residue_major_emit(zb_ref, blk, row_chunk), row_chunk)

    @pl.when(j >= _N_MAIN)
    def _():
        _project(h_ref, w_ref, tab_ref, None, gates, row_chunk)


def _proj_group_kernel(h_ref, w_ref, tab_ref, zb_ref, *, blk, row_chunk):
    _project(h_ref, w_ref, tab_ref, _qkv_scale(pl.program_id(1)),
             _residue_major_emit(zb_ref, blk, row_chunk), row_chunk)


def _rotary_lane_layout(w):
    d, n = w.shape
    blk = w.reshape(d, n // LANES, LANES)
    a, b, h = ROPE_HALF, ROPE_ROT, DL_HEAD_DIM
    swapped = jnp.concatenate([blk[..., :a], blk[..., h:h + a], blk[..., b:h], blk[..., a:b],
                               blk[..., h + a:]], axis=-1)
    per_tile = COL_TILE // LANES
    qk_tiles = [CT_QA, CT_QA + 1, CT_KA, CT_KA + 1] + [CT_DL + 3 * g + t for g in range(len(DL_PAIRS))
                                                      for t in (0, 1)]
    qk_blocks = {t * per_tile + c for t in qk_tiles for c in range(per_tile)}
    is_qk = jnp.asarray([i in qk_blocks for i in range(n // LANES)])
    return jnp.where(is_qk[None, :, None], swapped, blk).reshape(d, n)


def _first_head_lanes(lane):
    return (lane < ROPE_HALF) | ((lane >= ROPE_ROT) & (lane < DL_HEAD_DIM + ROPE_HALF))


def _rope_tables(seq):
    inv = 1.0 / (ROPE_THETA ** (jnp.arange(0, ROPE_ROT, 2, dtype=F32) / ROPE_ROT))
    ang = jnp.arange(seq, dtype=F32)[:, None] * inv[None, :]
    cos, sin = jnp.cos(ang), jnp.sin(ang)
    ones = jnp.ones((seq, DL_HEAD_DIM - ROPE_ROT), F32)
    zeros = jnp.zeros((seq, DL_HEAD_DIM - ROPE_ROT), F32)
    cs = jnp.concatenate([cos, cos, ones, cos, cos, ones], axis=1)
    sn = jnp.concatenate([-sin, -sin, zeros, sin, sin, zeros], axis=1)
    return jnp.stack([cs, sn])


def _residue_major_rows(tab, blk, dil):
    k, s, w = tab.shape
    return tab.reshape(k, s // blk, blk // dil, dil, w).transpose(0, 1, 3, 2, 4).reshape(k, s, w)


def _inproj(x, mod, g_mix, w_in):
    b, s, d = x.shape
    m = b * s
    blk = min(256, s)
    tm = min(2048, s)
    nsb = s // tm
    row_chunk = min(256, tm)
    for _, dil in DL_PAIRS:
        assert s % (dil * DL_BLOCK) == 0, "sequence must tile into dilated blocks without padding"
    hs = _norm(x, mod, g_mix, blk)
    w = _rotary_lane_layout(w_in).astype(BF16)
    tab = _rope_tables(s)
    h_spec = pl.BlockSpec((tm, d), lambda i, j: (i, 0))
    tab_spec = pl.BlockSpec((2, tm, LANES), lambda i, j: (0, i % nsb, 0))

    def zb_shape(dil):
        return jax.ShapeDtypeStruct((b, dil, s // dil, 3 * DL_W), BF16)

    def zb_spec(dil, first):
        return pl.BlockSpec((None, dil, tm // dil, COL_TILE),
                            lambda i, j: (i // nsb, 0, i % nsb, jnp.clip(j - first, 0, 2)))

    hm, zb0, zg = pl.pallas_call(
        functools.partial(_proj_main_kernel, blk=blk, row_chunk=row_chunk),
        grid=(b * nsb, _N_MAIN_TILES),
        in_specs=[h_spec,
                  pl.BlockSpec((d, COL_TILE), lambda i, j: (0, jnp.where(j < _N_MAIN, j, j + CT_GA - _N_MAIN))),
                  tab_spec],
        out_specs=[pl.BlockSpec((None, COL_TILE // DA_PAIR, tm, DA_PAIR),
                                lambda i, j: (i // nsb, jnp.minimum(j, CT_DL - 1), i % nsb, 0)),
                   zb_spec(1, CT_DL),
                   pl.BlockSpec((tm, COL_TILE), lambda i, j: (i, jnp.maximum(j - _N_MAIN, 0)))],
        out_shape=[jax.ShapeDtypeStruct((b, 3 * DA_HEADS, s, DA_PAIR), BF16), zb_shape(1),
                   jax.ShapeDtypeStruct((m, (N_COL_TILES - CT_GA) * COL_TILE), BF16)],
        compiler_params=_cparams(("parallel", "arbitrary")),
        name="proj_main",
    )(hs[0], w, tab)
    zbs = [zb0]
    for g in range(1, len(DL_PAIRS)):
        dil = DL_PAIRS[g][1]
        first = CT_DL + 3 * g
        zbs.append(pl.pallas_call(
            functools.partial(_proj_group_kernel, blk=blk, row_chunk=row_chunk),
            grid=(b * nsb, 3),
            in_specs=[h_spec, pl.BlockSpec((d, COL_TILE), lambda i, j, first=first: (0, first + j)), tab_spec],
            out_specs=zb_spec(dil, 0),
            out_shape=zb_shape(dil),
            compiler_params=_cparams(("parallel", "arbitrary")),
            name=f"proj_group{g}",
        )(hs[g], w, _residue_major_rows(tab, blk, dil)))
    return hm, zg, *zbs


def _diffattn_kernel(q_ref, k_ref, v_ref, lq1_ref, lk1_ref, lq2_ref, lk2_ref, gs_ref, o_ref,
                     v2_scr, *, tk, lambda_init):
    qi = pl.program_id(2)
    tq = q_ref.shape[0]
    seq = k_ref.shape[0]

    @pl.when(qi == 0)
    def _():
        v2_scr[:, :DA_PAIR] = v_ref[...]
        v2_scr[:, DA_PAIR:] = jnp.ones((seq, DA_PAIR), BF16)

    lane = lax.broadcasted_iota(jnp.int32, (tq, DA_PAIR), 1)
    q = q_ref[...]
    zero = jnp.zeros_like(q)
    first = _first_head_lanes(lane)
    qmaps = (jnp.where(first, q, zero), jnp.where(first, zero, q))
    m = [None, None]
    acc = [None, None]
    for kk in range(seq // tk):
        kb = k_ref[kk * tk:(kk + 1) * tk, :]
        vb = v2_scr[kk * tk:(kk + 1) * tk, :]
        for mp in range(2):
            s = _nt_dot(qmaps[mp], kb)
            m_cur = jnp.max(s, axis=1, keepdims=True)
            if kk == 0:
                m_new = jnp.broadcast_to(m_cur, (tq, LANES))
            else:
                m_new = jnp.maximum(m[mp], m_cur)
            p = jnp.exp2(s - jnp.tile(m_new, (1, tk // LANES)))
            pv = _dot(p.astype(BF16), vb)
            if kk == 0:
                acc[mp] = pv
            else:
                alpha = jnp.exp2(m[mp] - m_new)
                acc[mp] = acc[mp] * jnp.tile(alpha, (1, 2)) + pv
            m[mp] = m_new

    o1 = acc[0][:, :DA_PAIR] / acc[0][:, DA_PAIR:]
    o2 = acc[1][:, :DA_PAIR] / acc[1][:, DA_PAIR:]
    lam = (jnp.exp(jnp.sum(lq1_ref[...] * lk1_ref[...], axis=1, keepdims=True))
           - jnp.exp(jnp.sum(lq2_ref[...] * lk2_ref[...], axis=1, keepdims=True)) + lambda_init)
    o = o1 - lam * o2
    o_ref[...] = ((_rms(o) * gs_ref[...]) * (1.0 - lambda_init)).astype(BF16)


def _diffattn(hm, lam_q1, lam_k1, lam_q2, lam_k2, g_subln, lambda_init):
    b, nh3, seq, _ = hm.shape
    nh = nh3 // 3
    tq = seq if seq <= 2048 else 1024
    tk = min(256, seq)
    lam_spec = pl.BlockSpec((1, DA_HEAD_DIM), lambda bi, h, qi: (0, 0))
    return pl.pallas_call(
        functools.partial(_diffattn_kernel, tk=tk, lambda_init=lambda_init),
        grid=(b, nh, seq // tq),
        in_specs=[pl.BlockSpec((None, None, tq, DA_PAIR), lambda bi, h, qi: (bi, h, qi, 0)),
                  pl.BlockSpec((None, None, seq, DA_PAIR), lambda bi, h, qi: (bi, nh + h, 0, 0)),
                  pl.BlockSpec((None, None, seq, DA_PAIR), lambda bi, h, qi: (bi, 2 * nh + h, 0, 0)),
                  lam_spec, lam_spec, lam_spec, lam_spec,
                  pl.BlockSpec((1, DA_PAIR), lambda bi, h, qi: (0, 0))],
        out_specs=pl.BlockSpec((None, tq, DA_PAIR), lambda bi, h, qi: (bi, qi, h)),
        out_shape=jax.ShapeDtypeStruct((b, seq, nh * DA_PAIR), BF16),
        scratch_shapes=[pltpu.VMEM((seq, 2 * DA_PAIR), BF16)],
        compiler_params=_cparams(("parallel", "parallel", "arbitrary")),
        name="diffattn",
    )(hm, hm, hm, lam_q1.reshape(1, -1), lam_k1.reshape(1, -1), lam_q2.reshape(1, -1),
      lam_k2.reshape(1, -1), g_subln.reshape(1, -1))


def _dilated_kernel(q_ref, k_ref, v_ref, o_ref, lse_ref, *, sub, kw, radius):
    n = pl.program_id(2)
    n_cls, tq, _ = q_ref.shape
    t = k_ref.shape[1]
    lane = lax.broadcasted_iota(jnp.int32, (sub, LANES), 1)
    low = lane < DL_HEAD_DIM
    first = _first_head_lanes(lane)
    zero = jnp.zeros((sub, LANES), BF16)
    ones = jnp.ones((kw, LANES), BF16)
    n_pairs = DL_W // LANES
    for c in range(n_cls):
        o_rows, lse_rows = [], []
        for sb in range(tq // sub):
            q0 = n * tq + sb * sub
            ks = pl.multiple_of(jnp.clip(q0 - radius, 0, t - kw), DL_BLOCK)
            qpos = lax.broadcasted_iota(jnp.int32, (sub, kw), 0) + q0
            kpos = lax.broadcasted_iota(jnp.int32, (sub, kw), 1) + ks
            band = jnp.abs(kpos - qpos) <= radius
            band = jnp.concatenate([band, band], axis=0)
            all_scores = []
            for j in range(n_pairs):
                cols = slice(j * LANES, (j + 1) * LANES)
                qp = q_ref[c, sb * sub:(sb + 1) * sub, cols]
                qcat = jnp.concatenate([jnp.where(first, qp, zero), jnp.where(first, zero, qp)], axis=0)
                all_scores.append(_nt_dot(qcat, k_ref[c, pl.ds(ks, kw), cols]))
            ms, dens, o_cols = [], [], []
            for j in range(n_pairs):
                cols = slice(j * LANES, (j + 1) * LANES)
                vp = jnp.concatenate([v_ref[c, pl.ds(ks, kw), cols], ones], axis=1)
                s = jnp.where(band, all_scores[j], NEG)
                m = jnp.max(s, axis=1, keepdims=True)
                p = jnp.exp2(s - m)
                r = _dot(p.astype(BF16), vp)
                den = r[:, LANES:]
                o2 = r[:, :LANES] / den
                ms += [m[:sub], m[sub:]]
                dens += [den[:sub], den[sub:]]
                o_cols.append(jnp.where(low, o2[:sub], o2[sub:]).astype(BF16))
            m_tile = jnp.zeros((sub, LANES), F32)
            den_tile = jnp.ones((sub, LANES), F32)
            for hd in range(len(ms)):
                m_tile = jnp.where(lane == hd, ms[hd], m_tile)
                den_tile = jnp.where(lane == hd, dens[hd], den_tile)
            lse_tile = m_tile * LN2 + jnp.log(den_tile)
            o_rows.append(jnp.concatenate(o_cols, axis=1))
            lse_rows.append(lse_tile)
        o_ref[c] = jnp.concatenate(o_rows, axis=0)
        lse_ref[c] = jnp.concatenate(lse_rows, axis=0)


def _dilated(zb, g, window, dil):
    b, _, t, _ = zb.shape
    radius = window // (2 * dil)
    rows = min(2048, dil * t)
    tq = min(rows, t)
    n_cls = rows // tq
    sub = min(2 * radius, tq)
    kw = min(sub + 2 * radius, t)
    return pl.pallas_call(
        functools.partial(_dilated_kernel, sub=sub, kw=kw, radius=radius),
        grid=(b, dil // n_cls, t // tq),
        in_specs=[pl.BlockSpec((None, n_cls, tq, DL_W), lambda bi, c, n: (bi, c, n, 0)),
                  pl.BlockSpec((None, n_cls, t, DL_W), lambda bi, c, n: (bi, c, 0, 1)),
                  pl.BlockSpec((None, n_cls, t, DL_W), lambda bi, c, n: (bi, c, 0, 2))],
        out_specs=[pl.BlockSpec((None, n_cls, tq, DL_W), lambda bi, c, n: (bi, c, n, 0)),
                   pl.BlockSpec((None, n_cls, tq, LANES), lambda bi, c, n: (bi, c, n, 0))],
        out_shape=[jax.ShapeDtypeStruct((b, dil, t, DL_W), BF16),
                   jax.ShapeDtypeStruct((b, dil, t, LANES), F32)],
        compiler_params=_cparams(("parallel", "parallel", "arbitrary")),
        name=f"dilated{g}",
    )(zb, zb, zb)


def _natural_order(ref, scr):
    dil, n, w = ref.shape
    slabs = []
    for cb in range(w // LANES):
        cols = slice(cb * LANES, (cb + 1) * LANES)
        if dil == 1:
            slabs.append(ref[0, :, cols].astype(F32))
            continue
        for c in range(dil):
            scr[cb, pl.ds(c, n, stride=dil), :] = ref[c, :, cols].astype(F32)
        slabs.append(scr[cb])
    return slabs


def _tail_kernel(x_ref, mod_ref, oa_ref, o0_ref, o1_ref, o2_ref, l0_ref, l1_ref, l2_ref,
                 ga_ref, gb_ref, wa_ref, wb_ref, wo_ref, ex_ref, g_ref, wgu_ref, wd_ref, gf_ref, o_ref,
                 os1_scr, os2_scr, ls1_scr, ls2_scr, *, chunk, final_norm):
    l0, = _natural_order(l0_ref, None)
    l1, = _natural_order(l1_ref, ls1_scr)
    l2, = _natural_order(l2_ref, ls2_scr)
    o0 = _natural_order(o0_ref, None)
    o1 = _natural_order(o1_ref, os1_scr)
    o2 = _natural_order(o2_ref, os2_scr)
    mx = jnp.maximum(jnp.maximum(l0, l1), l2)
    e0, e1, e2 = jnp.exp(l0 - mx), jnp.exp(l1 - mx), jnp.exp(l2 - mx)
    inv = 1.0 / (e0 + e1 + e2)
    ob = None
    for e, o in ((e0, o0), (e1, o1), (e2, o2)):
        w = e * inv
        w_hi = w.astype(BF16)
        w_lo = (w - w_hi.astype(F32)).astype(BF16)
        wx = _dot(jnp.concatenate([w_hi, w_lo], axis=1), ex_ref[...])
        term = [wx[:, cb * LANES:(cb + 1) * LANES] * o[cb] for cb in range(len(o))]
        ob = term if ob is None else [a + t for a, t in zip(ob, term)]
    ob = jnp.concatenate(ob, axis=1)
    ya = _dot(oa_ref[...], wa_ref[...])
    yb = _dot(ob.astype(BF16), wb_ref[...])
    ga = ga_ref[...].astype(F32)
    gb = gb_ref[...].astype(F32)
    merged = ya / (1.0 + jnp.exp(-ga)) + yb / (1.0 + jnp.exp(-gb))
    y = _dot(merged.astype(BF16), wo_ref[...])
    x1 = x_ref[...] + mod_ref[2:3, :] * y
    h = ((_rms(x1) * g_ref[...]) * (1.0 + mod_ref[4:5, :]) + mod_ref[3:4, :]).astype(BF16)
    hidden = wd_ref.shape[0]
    acc = None
    for c in range(hidden // chunk):
        cols = slice(c * chunk, (c + 1) * chunk)
        gate = _dot(h, wgu_ref[:, cols])
        up = _dot(h, wgu_ref[:, hidden + c * chunk:hidden + (c + 1) * chunk])
        f = (gate / (1.0 + jnp.exp(-gate))) * up
        part = _dot(f.astype(BF16), wd_ref[cols, :])
        acc = part if acc is None else acc + part
    x2 = x1 + mod_ref[5:6, :] * acc
    o_ref[...] = _rms(x2) * gf_ref[...] if final_norm else x2


def _tail(x, mod, zg, oa, obs, lses, wa_bf, wb_bf, wo_bf, g_ffn, wgu_bf, wd_bf, g_final, final_norm):
    b, s, d = x.shape
    m = b * s
    tm = min(512, s)
    nsb = s // tm
    hidden = wd_bf.shape[0]
    chunk = hidden // 2
    assert chunk % LANES == 0
    row = lambda w: pl.BlockSpec((tm, w), lambda i: (i, 0))
    full = lambda shape: pl.BlockSpec(shape, lambda i: (0, 0), pipeline_mode=pl.Buffered(1))

    def residue_major(dil, w):
        return pl.BlockSpec((None, dil, tm // dil, w), lambda i: (i // nsb, 0, i % nsb, 0))

    dils = [dil for _, dil in DL_PAIRS]
    head = jnp.arange(LANES)[:, None]
    col = jnp.arange(DL_W)[None, :]
    expand = ((col // DL_HEAD_DIM) == head).astype(BF16)
    expand = jnp.concatenate([expand, expand], axis=0)
    return pl.pallas_call(
        functools.partial(_tail_kernel, chunk=chunk, final_norm=final_norm),
        grid=(m // tm,),
        in_specs=[row(d), pl.BlockSpec((None, 6, d), lambda i: (i // nsb, 0, 0)),
                  row(DA_HEADS * DA_PAIR)]
                 + [residue_major(dil, DL_W) for dil in dils]
                 + [residue_major(dil, LANES) for dil in dils]
                 + [pl.BlockSpec((tm, d), lambda i: (i, 0)), pl.BlockSpec((tm, d), lambda i: (i, 1)),
                    full(wa_bf.shape), full(wb_bf.shape), full(wo_bf.shape), full(expand.shape),
                    full((1, d)), full(wgu_bf.shape), full(wd_bf.shape), full((1, d))],
        out_specs=row(d),
        out_shape=jax.ShapeDtypeStruct((m, d), F32),
        scratch_shapes=[pltpu.VMEM((DL_W // LANES, tm, LANES), F32),
                        pltpu.VMEM((DL_W // LANES, tm, LANES), F32),
                        pltpu.VMEM((1, tm, LANES), F32), pltpu.VMEM((1, tm, LANES), F32)],
        compiler_params=_cparams(("parallel",)),
        name="tail",
    )(x.reshape(m, d), mod, oa.reshape(m, -1), *obs, *lses, zg, zg, wa_bf, wb_bf, wo_bf, expand,
      g_ffn.reshape(1, d), wgu_bf, wd_bf, g_final.reshape(1, d))


def _trunk(x, mod_all, weights):
    (g_mix, w_in, lam_q1, lam_k1, lam_q2, lam_k2, g_subln, w_a, w_b, w_out, g_ffn,
     w_gu, w_down, g_final) = weights
    b, s, d = x.shape
    depth = w_in.shape[0]
    for l in range(depth):
        lambda_init = 0.8 - 0.6 * math.exp(-0.3 * l)
        mod = mod_all[l]
        hm, zg, *zbs = _inproj(x, mod, g_mix[l], w_in[l])
        oa = _diffattn(hm, lam_q1[l], lam_k1[l], lam_q2[l], lam_k2[l], g_subln[l], lambda_init)
        obs, lses = [], []
        for g, (window, dil) in enumerate(DL_PAIRS):
            o, lse = _dilated(zbs[g], g, window, dil)
            obs.append(o)
            lses.append(lse)
        x = _tail(x, mod, zg, oa, obs, lses, w_a[l].astype(BF16), w_b[l].astype(BF16),
                  w_out[l].astype(BF16), g_ffn[l], w_gu[l].astype(BF16), w_down[l].astype(BF16),
                  g_final, l == depth - 1).reshape(b, s, d)
    return x


def kernel(x_prompt, x_sample, c_prompt, c_sample, w_ada, b_ada, g_mix, w_in, lam_q1, lam_k1,
           lam_q2, lam_k2, g_subln, w_a, w_b, w_out, g_ffn, w_gu, w_down, g_final):
    depth = w_in.shape[0]
    bp = c_prompt.shape[0]
    c_all = jnp.concatenate([c_prompt, c_sample], axis=0)
    mods = [_ada(c_all, w_ada[l], b_ada[l]).reshape(c_all.shape[0], 6, D_MODEL) for l in range(depth)]
    weights = (g_mix, w_in, lam_q1, lam_k1, lam_q2, lam_k2, g_subln, w_a, w_b, w_out, g_ffn,
               w_gu, w_down, g_final)
    y_prompt = _trunk(x_prompt, [m[:bp] for m in mods], weights)
    y_sample = _trunk(x_sample, [m[bp:] for m in mods], weights)
    return (y_prompt, y_sample)
```

```python
import functools
import math

import jax
import jax.numpy as jnp
from jax import lax
from jax.experimental import pallas as pl
from jax.experimental.pallas import tpu as pltpu

F32 = jnp.float32
BF16 = jnp.bfloat16

D_MODEL = 1024
DA_HEADS = 8
DA_HEAD_DIM = 64
DA_PAIR = 2 * DA_HEAD_DIM
DL_PAIRS = ((128, 1), (512, 4), (2048, 16))
DL_HEADS = 8
DL_HEAD_DIM = 64
DL_W = DL_HEADS * DL_HEAD_DIM
DL_BLOCK = 64
ROPE_THETA = 500000.0
ROPE_ROT = DL_HEAD_DIM // 4
ROPE_HALF = ROPE_ROT // 2
EPS = 1e-6
NEG = -1e30
LOG2E = math.log2(math.e)
LN2 = math.log(2.0)
Q_SCALE = DA_HEAD_DIM ** -0.5 * LOG2E

LANES = 128
COL_TILE = 512
N_IN = 2 * DA_HEADS * DA_PAIR + DA_HEADS * DA_PAIR + 9 * DL_W + 2 * D_MODEL
N_COL_TILES = N_IN // COL_TILE
CT_QA, CT_KA, CT_VA = 0, 2, 4
CT_DL = 6
CT_GA = 15
VMEM_LIMIT = 56 * 1024 * 1024

NORM_ROWS = 1024
PERM_ROWS = 256
PROJ_ROWS = 2048
PROJ_ROW_CHUNK = 256
DA_WHOLE_SEQ = 2048
DA_Q_ROWS = 2048
DA_KV_CHUNK = 256
DL_STEP_ROWS = 2048
TAIL_ROWS = 512


def _cparams(sem):
    return pltpu.CompilerParams(dimension_semantics=sem, vmem_limit_bytes=VMEM_LIMIT)


def _nt_dot(a, b):
    return lax.dot_general(a, b, (((1,), (1,)), ((), ())), preferred_element_type=F32)


def _dot(a, b):
    return jnp.dot(a, b, preferred_element_type=F32)


def _rms(x):
    return x * lax.rsqrt(jnp.mean(x * x, axis=-1, keepdims=True) + EPS)


def _ada_kernel(c_ref, w_ref, b_ref, o_ref):
    c = c_ref[...]
    a = c * (1.0 / (1.0 + jnp.exp(-c)))
    w = w_ref[...]
    a_hi = a.astype(BF16)
    a_lo = (a - a_hi.astype(F32)).astype(BF16)
    w_hi = w.astype(BF16)
    w_lo = (w - w_hi.astype(F32)).astype(BF16)
    o_ref[...] = _dot(a_hi, w_hi) + _dot(a_hi, w_lo) + _dot(a_lo, w_hi) + b_ref[...]


def _ada(c, w_ada, b_ada):
    nb, d = c.shape
    n = w_ada.shape[1]
    tn = COL_TILE
    return pl.pallas_call(
        _ada_kernel,
        grid=(n // tn,),
        in_specs=[pl.BlockSpec((nb, d), lambda j: (0, 0)),
                  pl.BlockSpec((d, tn), lambda j: (0, j)),
                  pl.BlockSpec((1, tn), lambda j: (0, j))],
        out_specs=pl.BlockSpec((nb, tn), lambda j: (0, j)),
        out_shape=jax.ShapeDtypeStruct((nb, n), F32),
        compiler_params=_cparams(("arbitrary",)),
        name="ada",
    )(c, w_ada, b_ada.reshape(1, n))


def _norm_kernel(x_ref, mod_ref, g_ref, p1_ref, p2_ref, h0_ref, h1_ref, h2_ref):
    blk = p1_ref.shape[0]
    for r in range(x_ref.shape[0] // blk):
        rows = pl.ds(r * blk, blk)
        y = _rms(x_ref[rows, :]) * g_ref[...]
        h = (y * (1.0 + mod_ref[1:2, :]) + mod_ref[0:1, :]).astype(BF16)
        h0_ref[rows, :] = h
        h1_ref[rows, :] = _dot(p1_ref[...], h).astype(BF16)
        h2_ref[rows, :] = _dot(p2_ref[...], h).astype(BF16)


def _residue_major_perm(rows, dil):
    dst = jnp.arange(rows)
    src = (dst % (rows // dil)) * dil + dst // (rows // dil)
    return (jnp.arange(rows)[None, :] == src[:, None]).astype(BF16)


def _norm(x, mod, g_mix, blk):
    b, s, d = x.shape
    m = b * s
    tm = min(NORM_ROWS, s)
    nsb = s // tm
    row = pl.BlockSpec((tm, d), lambda i: (i, 0))
    const = lambda shape: pl.BlockSpec(shape, lambda i: (0, 0), pipeline_mode=pl.Buffered(1))
    perms = [_residue_major_perm(blk, dil) for _, dil in DL_PAIRS[1:]]
    return pl.pallas_call(
        _norm_kernel,
        grid=(m // tm,),
        in_specs=[row, pl.BlockSpec((None, 6, d), lambda i: (i // nsb, 0, 0)), const((1, d)),
                  const((blk, blk)), const((blk, blk))],
        out_specs=[row, row, row],
        out_shape=[jax.ShapeDtypeStruct((m, d), BF16)] * 3,
        compiler_params=_cparams(("parallel",)),
        name="norm",
    )(x.reshape(m, d), mod, g_mix.reshape(1, d), *perms)


def _project(h_ref, w_ref, tab_ref, scale, emit, row_chunk):
    for r in range(h_ref.shape[0] // row_chunk):
        rows = pl.ds(r * row_chunk, row_chunk)
        z = _dot(h_ref[rows, :], w_ref[...])
        if scale is not None:
            cs, sn = scale(tab_ref[0, rows, :], tab_ref[1, rows, :])
            z = jnp.concatenate(
                [zc * cs + pltpu.roll(zc, LANES // 2, 1) * sn
                 for zc in (z[:, cb * LANES:(cb + 1) * LANES] for cb in range(COL_TILE // LANES))],
                axis=1)
        emit(r * row_chunk, z.astype(BF16))


def _qkv_scale(kind):
    def scale(cs, sn):
        f = jnp.where(kind == 0, Q_SCALE, 1.0).astype(F32)
        return jnp.where(kind < 2, cs * f, 1.0), jnp.where(kind < 2, sn * f, 0.0)
    return scale


def _residue_major_emit(zb_ref, blk, row_chunk):
    dil = zb_ref.shape[0]
    n = blk // dil
    piece = min(n, row_chunk)

    def emit(row0, z):
        for p in range(row_chunk // piece):
            r0 = row0 + p * piece
            pb, ro = divmod(r0, blk)
            zb_ref[ro // n, pb * n + ro % n:pb * n + ro % n + piece, :] = z[p * piece:(p + 1) * piece, :]
    return emit


_N_MAIN = CT_DL + 3
_N_MAIN_TILES = _N_MAIN + (N_COL_TILES - CT_GA)


def _proj_main_kernel(h_ref, w_ref, tab_ref, hm_ref, zb_ref, zg_ref, *, blk, row_chunk):
    j = pl.program_id(1)

    def head_major(row0, z):
        for hh in range(COL_TILE // DA_PAIR):
            hm_ref[hh, row0:row0 + row_chunk, :] = z[:, hh * DA_PAIR:(hh + 1) * DA_PAIR]

    def gates(row0, z):
        zg_ref[row0:row0 + row_chunk, :] = z

    @pl.when(j < CT_KA)
    def _():
        _project(h_ref, w_ref, tab_ref, lambda cs, sn: (cs * Q_SCALE, sn * Q_SCALE), head_major, row_chunk)

    @pl.when((j >= CT_KA) & (j < CT_VA))
    def _():
        _project(h_ref, w_ref, tab_ref, lambda cs, sn: (cs, sn), head_major, row_chunk)

    @pl.when((j >= CT_VA) & (j < CT_DL))
    def _():
        _project(h_ref, w_ref, tab_ref, None, head_major, row_chunk)

    @pl.when((j >= CT_DL) & (j < _N_MAIN))
    def _():
        _project(h_ref, w_ref, tab_ref, _qkv_scale(j - CT_DL),
                 _residue_major_emit(zb_ref, blk, row_chunk), row_chunk)

    @pl.when(j >= _N_MAIN)
    def _():
        _project(h_ref, w_ref, tab_ref, None, gates, row_chunk)


def _proj_group_kernel(h_ref, w_ref, tab_ref, zb_ref, *, blk, row_chunk):
    _project(h_ref, w_ref, tab_ref, _qkv_scale(pl.program_id(1)),
             _residue_major_emit(zb_ref, blk, row_chunk), row_chunk)


def _rotary_lane_layout(w):
    d, n = w.shape
    blk = w.reshape(d, n // LANES, LANES)
    a, b, h = ROPE_HALF, ROPE_ROT, DL_HEAD_DIM
    swapped = jnp.concatenate([blk[..., :a], blk[..., h:h + a], blk[..., b:h], blk[..., a:b],
                               blk[..., h + a:]], axis=-1)
    per_tile = COL_TILE // LANES
    qk_tiles = [CT_QA, CT_QA + 1, CT_KA, CT_KA + 1] + [CT_DL + 3 * g + t for g in range(len(DL_PAIRS))
                                                      for t in (0, 1)]
    qk_blocks = {t * per_tile + c for t in qk_tiles for c in range(per_tile)}
    is_qk = jnp.asarray([i in qk_blocks for i in range(n // LANES)])
    return jnp.where(is_qk[None, :, None], swapped, blk).reshape(d, n)


def _first_head_lanes(lane):
    return (lane < ROPE_HALF) | ((lane >= ROPE_ROT) & (lane < DL_HEAD_DIM + ROPE_HALF))


def _rope_tables(seq):
    inv = 1.0 / (ROPE_THETA ** (jnp.arange(0, ROPE_ROT, 2, dtype=F32) / ROPE_ROT))
    ang = jnp.arange(seq, dtype=F32)[:, None] * inv[None, :]
    cos, sin = jnp.cos(ang), jnp.sin(ang)
    ones = jnp.ones((seq, DL_HEAD_DIM - ROPE_ROT), F32)
    zeros = jnp.zeros((seq, DL_HEAD_DIM - ROPE_ROT), F32)
    cs = jnp.concatenate([cos, cos, ones, cos, cos, ones], axis=1)
    sn = jnp.concatenate([-sin, -sin, zeros, sin, sin, zeros], axis=1)
    return jnp.stack([cs, sn])


def _residue_major_rows(tab, blk, dil):
    k, s, w = tab.shape
    return tab.reshape(k, s // blk, blk // dil, dil, w).transpose(0, 1, 3, 2, 4).reshape(k, s, w)


def _inproj(x, mod, g_mix, w_in):
    b, s, d = x.shape
    m = b * s
    blk = min(PERM_ROWS, s)
    tm = min(PROJ_ROWS, s)
    nsb = s // tm
    row_chunk = min(PROJ_ROW_CHUNK, tm)
    for _, dil in DL_PAIRS:
        assert s % (dil * DL_BLOCK) == 0, "sequence must tile into dilated blocks without padding"
    hs = _norm(x, mod, g_mix, blk)
    w = _rotary_lane_layout(w_in).astype(BF16)
    tab = _rope_tables(s)
    h_spec = pl.BlockSpec((tm, d), lambda i, j: (i, 0))
    tab_spec = pl.BlockSpec((2, tm, LANES), lambda i, j: (0, i % nsb, 0))

    def zb_shape(dil):
        return jax.ShapeDtypeStruct((b, dil, s // dil, 3 * DL_W), BF16)

    def zb_spec(dil, first):
        return pl.BlockSpec((None, dil, tm // dil, COL_TILE),
                            lambda i, j: (i // nsb, 0, i % nsb, jnp.clip(j - first, 0, 2)))

    hm, zb0, zg = pl.pallas_call(
        functools.partial(_proj_main_kernel, blk=blk, row_chunk=row_chunk),
        grid=(b * nsb, _N_MAIN_TILES),
        in_specs=[h_spec,
                  pl.BlockSpec((d, COL_TILE), lambda i, j: (0, jnp.where(j < _N_MAIN, j, j + CT_GA - _N_MAIN))),
                  tab_spec],
        out_specs=[pl.BlockSpec((None, COL_TILE // DA_PAIR, tm, DA_PAIR),
                                lambda i, j: (i // nsb, jnp.minimum(j, CT_DL - 1), i % nsb, 0)),
                   zb_spec(1, CT_DL),
                   pl.BlockSpec((tm, COL_TILE), lambda i, j: (i, jnp.maximum(j - _N_MAIN, 0)))],
        out_shape=[jax.ShapeDtypeStruct((b, 3 * DA_HEADS, s, DA_PAIR), BF16), zb_shape(1),
                   jax.ShapeDtypeStruct((m, (N_COL_TILES - CT_GA) * COL_TILE), BF16)],
        compiler_params=_cparams(("parallel", "arbitrary")),
        name="proj_main",
    )(hs[0], w, tab)
    zbs = [zb0]
    for g in range(1, len(DL_PAIRS)):
        dil = DL_PAIRS[g][1]
        first = CT_DL + 3 * g
        zbs.append(pl.pallas_call(
            functools.partial(_proj_group_kernel, blk=blk, row_chunk=row_chunk),
            grid=(b * nsb, 3),
            in_specs=[h_spec, pl.BlockSpec((d, COL_TILE), lambda i, j, first=first: (0, first + j)), tab_spec],
            out_specs=zb_spec(dil, 0),
            out_shape=zb_shape(dil),
            compiler_params=_cparams(("parallel", "arbitrary")),
            name=f"proj_group{g}",
        )(hs[g], w, _residue_major_rows(tab, blk, dil)))
    return hm, zg, *zbs


def _diffattn_kernel(q_ref, k_ref, v_ref, lq1_ref, lk1_ref, lq2_ref, lk2_ref, gs_ref, o_ref,
                     v2_scr, *, tk, lambda_init):
    qi = pl.program_id(2)
    tq = q_ref.shape[0]
    seq = k_ref.shape[0]

    @pl.when(qi == 0)
    def _():
        v2_scr[:, :DA_PAIR] = v_ref[...]
        v2_scr[:, DA_PAIR:] = jnp.ones((seq, DA_PAIR), BF16)

    lane = lax.broadcasted_iota(jnp.int32, (tq, DA_PAIR), 1)
    q = q_ref[...]
    zero = jnp.zeros_like(q)
    first = _first_head_lanes(lane)
    qmaps = (jnp.where(first, q, zero), jnp.where(first, zero, q))
    m = [None, None]
    acc = [None, None]
    for kk in range(seq // tk):
        kb = k_ref[kk * tk:(kk + 1) * tk, :]
        vb = v2_scr[kk * tk:(kk + 1) * tk, :]
        for mp in range(2):
            s = _nt_dot(qmaps[mp], kb)
            m_cur = jnp.max(s, axis=1, keepdims=True)
            if kk == 0:
                m_new = jnp.broadcast_to(m_cur, (tq, LANES))
            else:
                m_new = jnp.maximum(m[mp], m_cur)
            p = jnp.exp2(s - jnp.tile(m_new, (1, tk // LANES)))
            pv = _dot(p.astype(BF16), vb)
            if kk == 0:
                acc[mp] = pv
            else:
                alpha = jnp.exp2(m[mp] - m_new)
                acc[mp] = acc[mp] * jnp.tile(alpha, (1, 2)) + pv
            m[mp] = m_new

    o1 = acc[0][:, :DA_PAIR] / acc[0][:, DA_PAIR:]
    o2 = acc[1][:, :DA_PAIR] / acc[1][:, DA_PAIR:]
    lam = (jnp.exp(jnp.sum(lq1_ref[...] * lk1_ref[...], axis=1, keepdims=True))
           - jnp.exp(jnp.sum(lq2_ref[...] * lk2_ref[...], axis=1, keepdims=True)) + lambda_init)
    o = o1 - lam * o2
    o_ref[...] = ((_rms(o) * gs_ref[...]) * (1.0 - lambda_init)).astype(BF16)


def _diffattn(hm, lam_q1, lam_k1, lam_q2, lam_k2, g_subln, lambda_init):
    b, nh3, seq, _ = hm.shape
    nh = nh3 // 3
    tq = seq if seq <= DA_WHOLE_SEQ else DA_Q_ROWS
    tk = min(DA_KV_CHUNK, seq)
    lam_spec = pl.BlockSpec((1, DA_HEAD_DIM), lambda bi, h, qi: (0, 0))
    return pl.pallas_call(
        functools.partial(_diffattn_kernel, tk=tk, lambda_init=lambda_init),
        grid=(b, nh, seq // tq),
        in_specs=[pl.BlockSpec((None, None, tq, DA_PAIR), lambda bi, h, qi: (bi, h, qi, 0)),
                  pl.BlockSpec((None, None, seq, DA_PAIR), lambda bi, h, qi: (bi, nh + h, 0, 0)),
                  pl.BlockSpec((None, None, seq, DA_PAIR), lambda bi, h, qi: (bi, 2 * nh + h, 0, 0)),
                  lam_spec, lam_spec, lam_spec, lam_spec,
                  pl.BlockSpec((1, DA_PAIR), lambda bi, h, qi: (0, 0))],
        out_specs=pl.BlockSpec((None, tq, DA_PAIR), lambda bi, h, qi: (bi, qi, h)),
        out_shape=jax.ShapeDtypeStruct((b, seq, nh * DA_PAIR), BF16),
        scratch_shapes=[pltpu.VMEM((seq, 2 * DA_PAIR), BF16)],
        compiler_params=_cparams(("parallel", "parallel", "arbitrary")),
        name="diffattn",
    )(hm, hm, hm, lam_q1.reshape(1, -1), lam_k1.reshape(1, -1), lam_q2.reshape(1, -1),
      lam_k2.reshape(1, -1), g_subln.reshape(1, -1))


def _dilated_kernel(q_ref, k_ref, v_ref, o_ref, lse_ref, *, sub, kw, radius):
    n = pl.program_id(2)
    n_cls, tq, _ = q_ref.shape
    t = k_ref.shape[1]
    lane = lax.broadcasted_iota(jnp.int32, (sub, LANES), 1)
    low = lane < DL_HEAD_DIM
    first = _first_head_lanes(lane)
    zero = jnp.zeros((sub, LANES), BF16)
    ones = jnp.ones((kw, LANES), BF16)
    n_pairs = DL_W // LANES
    for c in range(n_cls):
        o_rows, lse_rows = [], []
        for sb in range(tq // sub):
            q0 = n * tq + sb * sub
            ks = pl.multiple_of(jnp.clip(q0 - radius, 0, t - kw), DL_BLOCK)
            qpos = lax.broadcasted_iota(jnp.int32, (sub, kw), 0) + q0
            kpos = lax.broadcasted_iota(jnp.int32, (sub, kw), 1) + ks
            band = jnp.abs(kpos - qpos) <= radius
            band = jnp.concatenate([band, band], axis=0)
            all_scores = []
            for j in range(n_pairs):
                cols = slice(j * LANES, (j + 1) * LANES)
                qp = q_ref[c, sb * sub:(sb + 1) * sub, cols]
                qcat = jnp.concatenate([jnp.where(first, qp, zero), jnp.where(first, zero, qp)], axis=0)
                all_scores.append(_nt_dot(qcat, k_ref[c, pl.ds(ks, kw), cols]))
            ms, dens, o_cols = [], [], []
            for j in range(n_pairs):
                cols = slice(j * LANES, (j + 1) * LANES)
                vp = jnp.concatenate([v_ref[c, pl.ds(ks, kw), cols], ones], axis=1)
                s = jnp.where(band, all_scores[j], NEG)
                m = jnp.max(s, axis=1, keepdims=True)
                p = jnp.exp2(s - m)
                r = _dot(p.astype(BF16), vp)
                den = r[:, LANES:]
                o2 = r[:, :LANES] / den
                ms += [m[:sub], m[sub:]]
                dens += [den[:sub], den[sub:]]
                o_cols.append(jnp.where(low, o2[:sub], o2[sub:]).astype(BF16))
            m_tile = jnp.zeros((sub, LANES), F32)
            den_tile = jnp.ones((sub, LANES), F32)
            for hd in range(len(ms)):
                m_tile = jnp.where(lane == hd, ms[hd], m_tile)
                den_tile = jnp.where(lane == hd, dens[hd], den_tile)
            lse_tile = m_tile * LN2 + jnp.log(den_tile)
            o_rows.append(jnp.concatenate(o_cols, axis=1))
            lse_rows.append(lse_tile)
        o_ref[c] = jnp.concatenate(o_rows, axis=0)
        lse_ref[c] = jnp.concatenate(lse_rows, axis=0)


def _dilated(zb, g, window, dil):
    b, _, t, _ = zb.shape
    radius = window // (2 * dil)
    rows = min(DL_STEP_ROWS, dil * t)
    tq = min(rows, t)
    n_cls = rows // tq
    sub = min(2 * radius, tq)
    kw = min(sub + 2 * radius, t)
    return pl.pallas_call(
        functools.partial(_dilated_kernel, sub=sub, kw=kw, radius=radius),
        grid=(b, dil // n_cls, t // tq),
        in_specs=[pl.BlockSpec((None, n_cls, tq, DL_W), lambda bi, c, n: (bi, c, n, 0)),
                  pl.BlockSpec((None, n_cls, t, DL_W), lambda bi, c, n: (bi, c, 0, 1)),
                  pl.BlockSpec((None, n_cls, t, DL_W), lambda bi, c, n: (bi, c, 0, 2))],
        out_specs=[pl.BlockSpec((None, n_cls, tq, DL_W), lambda bi, c, n: (bi, c, n, 0)),
                   pl.BlockSpec((None, n_cls, tq, LANES), lambda bi, c, n: (bi, c, n, 0))],
        out_shape=[jax.ShapeDtypeStruct((b, dil, t, DL_W), BF16),
                   jax.ShapeDtypeStruct((b, dil, t, LANES), F32)],
        compiler_params=_cparams(("parallel", "parallel", "arbitrary")),
        name=f"dilated{g}",
    )(zb, zb, zb)


def _natural_order(ref, scr):
    dil, n, w = ref.shape
    slabs = []
    for cb in range(w // LANES):
        cols = slice(cb * LANES, (cb + 1) * LANES)
        if dil == 1:
            slabs.append(ref[0, :, cols].astype(F32))
            continue
        for c in range(dil):
            scr[cb, pl.ds(c, n, stride=dil), :] = ref[c, :, cols].astype(F32)
        slabs.append(scr[cb])
    return slabs


def _tail_kernel(x_ref, mod_ref, oa_ref, o0_ref, o1_ref, o2_ref, l0_ref, l1_ref, l2_ref,
                 ga_ref, gb_ref, wa_ref, wb_ref, wo_ref, ex_ref, g_ref, wgu_ref, wd_ref, gf_ref, o_ref,
                 os1_scr, os2_scr, ls1_scr, ls2_scr, *, chunk, final_norm):
    l0, = _natural_order(l0_ref, None)
    l1, = _natural_order(l1_ref, ls1_scr)
    l2, = _natural_order(l2_ref, ls2_scr)
    o0 = _natural_order(o0_ref, None)
    o1 = _natural_order(o1_ref, os1_scr)
    o2 = _natural_order(o2_ref, os2_scr)
    mx = jnp.maximum(jnp.maximum(l0, l1), l2)
    e0, e1, e2 = jnp.exp(l0 - mx), jnp.exp(l1 - mx), jnp.exp(l2 - mx)
    inv = 1.0 / (e0 + e1 + e2)
    ob = None
    for e, o in ((e0, o0), (e1, o1), (e2, o2)):
        w = e * inv
        w_hi = w.astype(BF16)
        w_lo = (w - w_hi.astype(F32)).astype(BF16)
        wx = _dot(jnp.concatenate([w_hi, w_lo], axis=1), ex_ref[...])
        term = [wx[:, cb * LANES:(cb + 1) * LANES] * o[cb] for cb in range(len(o))]
        ob = term if ob is None else [a + t for a, t in zip(ob, term)]
    ob = jnp.concatenate(ob, axis=1)
    ya = _dot(oa_ref[...], wa_ref[...])
    yb = _dot(ob.astype(BF16), wb_ref[...])
    ga = ga_ref[...].astype(F32)
    gb = gb_ref[...].astype(F32)
    merged = ya / (1.0 + jnp.exp(-ga)) + yb / (1.0 + jnp.exp(-gb))
    y = _dot(merged.astype(BF16), wo_ref[...])
    x1 = x_ref[...] + mod_ref[2:3, :] * y
    h = ((_rms(x1) * g_ref[...]) * (1.0 + mod_ref[4:5, :]) + mod_ref[3:4, :]).astype(BF16)
    hidden = wd_ref.shape[0]
    acc = None
    for c in range(hidden // chunk):
        cols = slice(c * chunk, (c + 1) * chunk)
        gate = _dot(h, wgu_ref[:, cols])
        up = _dot(h, wgu_ref[:, hidden + c * chunk:hidden + (c + 1) * chunk])
        f = (gate / (1.0 + jnp.exp(-gate))) * up
        part = _dot(f.astype(BF16), wd_ref[cols, :])
        acc = part if acc is None else acc + part
    x2 = x1 + mod_ref[5:6, :] * acc
    o_ref[...] = _rms(x2) * gf_ref[...] if final_norm else x2


def _tail(x, mod, zg, oa, obs, lses, wa_bf, wb_bf, wo_bf, g_ffn, wgu_bf, wd_bf, g_final, final_norm):
    b, s, d = x.shape
    m = b * s
    tm = min(TAIL_ROWS, s)
    nsb = s // tm
    hidden = wd_bf.shape[0]
    chunk = hidden // 2
    assert chunk % LANES == 0
    row = lambda w: pl.BlockSpec((tm, w), lambda i: (i, 0))
    full = lambda shape: pl.BlockSpec(shape, lambda i: (0, 0), pipeline_mode=pl.Buffered(1))

    def residue_major(dil, w):
        return pl.BlockSpec((None, dil, tm // dil, w), lambda i: (i // nsb, 0, i % nsb, 0))

    dils = [dil for _, dil in DL_PAIRS]
    head = jnp.arange(LANES)[:, None]
    col = jnp.arange(DL_W)[None, :]
    expand = ((col // DL_HEAD_DIM) == head).astype(BF16)
    expand = jnp.concatenate([expand, expand], axis=0)
    return pl.pallas_call(
        functools.partial(_tail_kernel, chunk=chunk, final_norm=final_norm),
        grid=(m // tm,),
        in_specs=[row(d), pl.BlockSpec((None, 6, d), lambda i: (i // nsb, 0, 0)),
                  row(DA_HEADS * DA_PAIR)]
                 + [residue_major(dil, DL_W) for dil in dils]
                 + [residue_major(dil, LANES) for dil in dils]
                 + [pl.BlockSpec((tm, d), lambda i: (i, 0)), pl.BlockSpec((tm, d), lambda i: (i, 1)),
                    full(wa_bf.shape), full(wb_bf.shape), full(wo_bf.shape), full(expand.shape),
                    full((1, d)), full(wgu_bf.shape), full(wd_bf.shape), full((1, d))],
        out_specs=row(d),
        out_shape=jax.ShapeDtypeStruct((m, d), F32),
        scratch_shapes=[pltpu.VMEM((DL_W // LANES, tm, LANES), F32),
                        pltpu.VMEM((DL_W // LANES, tm, LANES), F32),
                        pltpu.VMEM((1, tm, LANES), F32), pltpu.VMEM((1, tm, LANES), F32)],
        compiler_params=_cparams(("parallel",)),
        name="tail",
    )(x.reshape(m, d), mod, oa.reshape(m, -1), *obs, *lses, zg, zg, wa_bf, wb_bf, wo_bf, expand,
      g_ffn.reshape(1, d), wgu_bf, wd_bf, g_final.reshape(1, d))


def _trunk(x, mod_all, weights):
    (g_mix, w_in, lam_q1, lam_k1, lam_q2, lam_k2, g_subln, w_a, w_b, w_out, g_ffn,
     w_gu, w_down, g_final) = weights
    b, s, d = x.shape
    depth = w_in.shape[0]
    for l in range(depth):
        lambda_init = 0.8 - 0.6 * math.exp(-0.3 * l)
        mod = mod_all[l]
        hm, zg, *zbs = _inproj(x, mod, g_mix[l], w_in[l])
        oa = _diffattn(hm, lam_q1[l], lam_k1[l], lam_q2[l], lam_k2[l], g_subln[l], lambda_init)
        obs, lses = [], []
        for g, (window, dil) in enumerate(DL_PAIRS):
            o, lse = _dilated(zbs[g], g, window, dil)
            obs.append(o)
            lses.append(lse)
        x = _tail(x, mod, zg, oa, obs, lses, w_a[l].astype(BF16), w_b[l].astype(BF16),
                  w_out[l].astype(BF16), g_ffn[l], w_gu[l].astype(BF16), w_down[l].astype(BF16),
                  g_final, l == depth - 1).reshape(b, s, d)
    return x


def kernel(x_prompt, x_sample, c_prompt, c_sample, w_ada, b_ada, g_mix, w_in, lam_q1, lam_k1,
           lam_q2, lam_k2, g_subln, w_a, w_b, w_out, g_ffn, w_gu, w_down, g_final):
    depth = w_in.shape[0]
    bp = c_prompt.shape[0]
    c_all = jnp.concatenate([c_prompt, c_sample], axis=0)
    mods = [_ada(c_all, w_ada[l], b_ada[l]).reshape(c_all.shape[0], 6, D_MODEL) for l in range(depth)]
    weights = (g_mix, w_in, lam_q1, lam_k1, lam_q2, lam_k2, g_subln, w_a, w_b, w_out, g_ffn,
               w_gu, w_down, g_final)
    y_prompt = _trunk(x_prompt, [m[:bp] for m in mods], weights)
    y_sample = _trunk(x_sample, [m[bp:] for m in mods], weights)
    return (y_prompt, y_sample)
```

```python
import functools
import math

import jax
import jax.numpy as jnp
from jax import lax
from jax.experimental import pallas as pl
from jax.experimental.pallas import tpu as pltpu

F32 = jnp.float32
BF16 = jnp.bfloat16

D_MODEL = 1024
DA_HEADS = 8
DA_HEAD_DIM = 64
DA_PAIR = 2 * DA_HEAD_DIM
DL_PAIRS = ((128, 1), (512, 4), (2048, 16))
DL_HEADS = 8
DL_HEAD_DIM = 64
DL_W = DL_HEADS * DL_HEAD_DIM
DL_BLOCK = 64
ROPE_THETA = 500000.0
ROPE_ROT = DL_HEAD_DIM // 4
ROPE_HALF = ROPE_ROT // 2
EPS = 1e-6
NEG = -1e30
LOG2E = math.log2(math.e)
LN2 = math.log(2.0)
Q_SCALE = DA_HEAD_DIM ** -0.5 * LOG2E

LANES = 128
COL_TILE = 512
N_IN = 2 * DA_HEADS * DA_PAIR + DA_HEADS * DA_PAIR + 9 * DL_W + 2 * D_MODEL
N_COL_TILES = N_IN // COL_TILE
CT_QA, CT_KA, CT_VA = 0, 2, 4
CT_DL = 6
CT_GA = 15
VMEM_LIMIT = 56 * 1024 * 1024

NORM_ROWS = 1024
PERM_ROWS = 256
PROJ_ROWS = 2048
PROJ_ROW_CHUNK = 256
DA_WHOLE_SEQ = 2048
DA_Q_ROWS = 2048
DA_KV_CHUNK = 256
DL_STEP_ROWS = 2048
TAIL_ROWS = 512
FFN_CHUNK = 768


def _cparams(sem):
    return pltpu.CompilerParams(dimension_semantics=sem, vmem_limit_bytes=VMEM_LIMIT)


def _nt_dot(a, b):
    return lax.dot_general(a, b, (((1,), (1,)), ((), ())), preferred_element_type=F32)


def _dot(a, b):
    return jnp.dot(a, b, preferred_element_type=F32)


def _rms(x):
    return x * lax.rsqrt(jnp.mean(x * x, axis=-1, keepdims=True) + EPS)


def _ada_kernel(c_ref, w_ref, b_ref, o_ref):
    c = c_ref[...]
    a = c * (1.0 / (1.0 + jnp.exp(-c)))
    w = w_ref[...]
    a_hi = a.astype(BF16)
    a_lo = (a - a_hi.astype(F32)).astype(BF16)
    w_hi = w.astype(BF16)
    w_lo = (w - w_hi.astype(F32)).astype(BF16)
    o_ref[...] = _dot(a_hi, w_hi) + _dot(a_hi, w_lo) + _dot(a_lo, w_hi) + b_ref[...]


def _ada(c, w_ada, b_ada):
    nb, d = c.shape
    n = w_ada.shape[1]
    tn = COL_TILE
    return pl.pallas_call(
        _ada_kernel,
        grid=(n // tn,),
        in_specs=[pl.BlockSpec((nb, d), lambda j: (0, 0)),
                  pl.BlockSpec((d, tn), lambda j: (0, j)),
                  pl.BlockSpec((1, tn), lambda j: (0, j))],
        out_specs=pl.BlockSpec((nb, tn), lambda j: (0, j)),
        out_shape=jax.ShapeDtypeStruct((nb, n), F32),
        compiler_params=_cparams(("arbitrary",)),
        name="ada",
    )(c, w_ada, b_ada.reshape(1, n))


def _norm_kernel(x_ref, mod_ref, g_ref, p1_ref, p2_ref, h0_ref, h1_ref, h2_ref):
    blk = p1_ref.shape[0]
    for r in range(x_ref.shape[0] // blk):
        rows = pl.ds(r * blk, blk)
        y = _rms(x_ref[rows, :]) * g_ref[...]
        h = (y * (1.0 + mod_ref[1:2, :]) + mod_ref[0:1, :]).astype(BF16)
        h0_ref[rows, :] = h
        h1_ref[rows, :] = _dot(p1_ref[...], h).astype(BF16)
        h2_ref[rows, :] = _dot(p2_ref[...], h).astype(BF16)


def _residue_major_perm(rows, dil):
    dst = jnp.arange(rows)
    src = (dst % (rows // dil)) * dil + dst // (rows // dil)
    return (jnp.arange(rows)[None, :] == src[:, None]).astype(BF16)


def _norm(x, mod, g_mix, blk):
    b, s, d = x.shape
    m = b * s
    tm = min(NORM_ROWS, s)
    nsb = s // tm
    row = pl.BlockSpec((tm, d), lambda i: (i, 0))
    const = lambda shape: pl.BlockSpec(shape, lambda i: (0, 0), pipeline_mode=pl.Buffered(1))
    perms = [_residue_major_perm(blk, dil) for _, dil in DL_PAIRS[1:]]
    return pl.pallas_call(
        _norm_kernel,
        grid=(m // tm,),
        in_specs=[row, pl.BlockSpec((None, 6, d), lambda i: (i // nsb, 0, 0)), const((1, d)),
                  const((blk, blk)), const((blk, blk))],
        out_specs=[row, row, row],
        out_shape=[jax.ShapeDtypeStruct((m, d), BF16)] * 3,
        compiler_params=_cparams(("parallel",)),
        name="norm",
    )(x.reshape(m, d), mod, g_mix.reshape(1, d), *perms)


def _project(h_ref, w_ref, tab_ref, scale, emit, row_chunk):
    for r in range(h_ref.shape[0] // row_chunk):
        rows = pl.ds(r * row_chunk, row_chunk)
        z = _dot(h_ref[rows, :], w_ref[...])
        if scale is not None:
            cs, sn = scale(tab_ref[0, rows, :], tab_ref[1, rows, :])
            z = jnp.concatenate(
                [zc * cs + pltpu.roll(zc, LANES // 2, 1) * sn
                 for zc in (z[:, cb * LANES:(cb + 1) * LANES] for cb in range(COL_TILE // LANES))],
                axis=1)
        emit(r * row_chunk, z.astype(BF16))


def _qkv_scale(kind):
    def scale(cs, sn):
        f = jnp.where(kind == 0, Q_SCALE, 1.0).astype(F32)
        return jnp.where(kind < 2, cs * f, 1.0), jnp.where(kind < 2, sn * f, 0.0)
    return scale


def _residue_major_emit(zb_ref, blk, row_chunk):
    dil = zb_ref.shape[0]
    n = blk // dil
    piece = min(n, row_chunk)

    def emit(row0, z):
        for p in range(row_chunk // piece):
            r0 = row0 + p * piece
            pb, ro = divmod(r0, blk)
            zb_ref[ro // n, pb * n + ro % n:pb * n + ro % n + piece, :] = z[p * piece:(p + 1) * piece, :]
    return emit


_N_MAIN = CT_DL + 3
_N_MAIN_TILES = _N_MAIN + (N_COL_TILES - CT_GA)


def _proj_main_kernel(h_ref, w_ref, tab_ref, hm_ref, zb_ref, zg_ref, *, blk, row_chunk):
    j = pl.program_id(1)

    def head_major(row0, z):
        for hh in range(COL_TILE // DA_PAIR):
            hm_ref[hh, row0:row0 + row_chunk, :] = z[:, hh * DA_PAIR:(hh + 1) * DA_PAIR]

    def gates(row0, z):
        zg_ref[row0:row0 + row_chunk, :] = z

    @pl.when(j < CT_KA)
    def _():
        _project(h_ref, w_ref, tab_ref, lambda cs, sn: (cs * Q_SCALE, sn * Q_SCALE), head_major, row_chunk)

    @pl.when((j >= CT_KA) & (j < CT_VA))
    def _():
        _project(h_ref, w_ref, tab_ref, lambda cs, sn: (cs, sn), head_major, row_chunk)

    @pl.when((j >= CT_VA) & (j < CT_DL))
    def _():
        _project(h_ref, w_ref, tab_ref, None, head_major, row_chunk)

    @pl.when((j >= CT_DL) & (j < _N_MAIN))
    def _():
        _project(h_ref, w_ref, tab_ref, _qkv_scale(j - CT_DL),
                 _residue_major_emit(zb_ref, blk, row_chunk), row_chunk)

    @pl.when(j >= _N_MAIN)
    def _():
        _project(h_ref, w_ref, tab_ref, None, gates, row_chunk)


def _proj_group_kernel(h_ref, w_ref, tab_ref, zb_ref, *, blk, row_chunk):
    _project(h_ref, w_ref, tab_ref, _qkv_scale(pl.program_id(1)),
             _residue_major_emit(zb_ref, blk, row_chunk), row_chunk)


def _rotary_lane_layout(w):
    d, n = w.shape
    blk = w.reshape(d, n // LANES, LANES)
    a, b, h = ROPE_HALF, ROPE_ROT, DL_HEAD_DIM
    swapped = jnp.concatenate([blk[..., :a], blk[..., h:h + a], blk[..., b:h], blk[..., a:b],
                               blk[..., h + a:]], axis=-1)
    per_tile = COL_TILE // LANES
    qk_tiles = [CT_QA, CT_QA + 1, CT_KA, CT_KA + 1] + [CT_DL + 3 * g + t for g in range(len(DL_PAIRS))
                                                      for t in (0, 1)]
    qk_blocks = {t * per_tile + c for t in qk_tiles for c in range(per_tile)}
    is_qk = jnp.asarray([i in qk_blocks for i in range(n // LANES)])
    return jnp.where(is_qk[None, :, None], swapped, blk).reshape(d, n)


def _first_head_lanes(lane):
    return (lane < ROPE_HALF) | ((lane >= ROPE_ROT) & (lane < DL_HEAD_DIM + ROPE_HALF))


def _rope_tables(seq):
    inv = 1.0 / (ROPE_THETA ** (jnp.arange(0, ROPE_ROT, 2, dtype=F32) / ROPE_ROT))
    ang = jnp.arange(seq, dtype=F32)[:, None] * inv[None, :]
    cos, sin = jnp.cos(ang), jnp.sin(ang)
    ones = jnp.ones((seq, DL_HEAD_DIM - ROPE_ROT), F32)
    zeros = jnp.zeros((seq, DL_HEAD_DIM - ROPE_ROT), F32)
    cs = jnp.concatenate([cos, cos, ones, cos, cos, ones], axis=1)
    sn = jnp.concatenate([-sin, -sin, zeros, sin, sin, zeros], axis=1)
    return jnp.stack([cs, sn])


def _residue_major_rows(tab, blk, dil):
    k, s, w = tab.shape
    return tab.reshape(k, s // blk, blk // dil, dil, w).transpose(0, 1, 3, 2, 4).reshape(k, s, w)


def _inproj(x, mod, g_mix, w_in):
    b, s, d = x.shape
    m = b * s
    blk = min(PERM_ROWS, s)
    tm = min(PROJ_ROWS, s)
    nsb = s // tm
    row_chunk = min(PROJ_ROW_CHUNK, tm)
    for _, dil in DL_PAIRS:
        assert s % (dil * DL_BLOCK) == 0, "sequence must tile into dilated blocks without padding"
    hs = _norm(x, mod, g_mix, blk)
    w = _rotary_lane_layout(w_in).astype(BF16)
    tab = _rope_tables(s)
    h_spec = pl.BlockSpec((tm, d), lambda i, j: (i, 0))
    tab_spec = pl.BlockSpec((2, tm, LANES), lambda i, j: (0, i % nsb, 0))

    def zb_shape(dil):
        return jax.ShapeDtypeStruct((b, dil, s // dil, 3 * DL_W), BF16)

    def zb_spec(dil, first):
        return pl.BlockSpec((None, dil, tm // dil, COL_TILE),
                            lambda i, j: (i // nsb, 0, i % nsb, jnp.clip(j - first, 0, 2)))

    hm, zb0, zg = pl.pallas_call(
        functools.partial(_proj_main_kernel, blk=blk, row_chunk=row_chunk),
        grid=(b * nsb, _N_MAIN_TILES),
        in_specs=[h_spec,
                  pl.BlockSpec((d, COL_TILE), lambda i, j: (0, jnp.where(j < _N_MAIN, j, j + CT_GA - _N_MAIN))),
                  tab_spec],
        out_specs=[pl.BlockSpec((None, COL_TILE // DA_PAIR, tm, DA_PAIR),
                                lambda i, j: (i // nsb, jnp.minimum(j, CT_DL - 1), i % nsb, 0)),
                   zb_spec(1, CT_DL),
                   pl.BlockSpec((tm, COL_TILE), lambda i, j: (i, jnp.maximum(j - _N_MAIN, 0)))],
        out_shape=[jax.ShapeDtypeStruct((b, 3 * DA_HEADS, s, DA_PAIR), BF16), zb_shape(1),
                   jax.ShapeDtypeStruct((m, (N_COL_TILES - CT_GA) * COL_TILE), BF16)],
        compiler_params=_cparams(("parallel", "arbitrary")),
        name="proj_main",
    )(hs[0], w, tab)
    zbs = [zb0]
    for g in range(1, len(DL_PAIRS)):
        dil = DL_PAIRS[g][1]
        first = CT_DL + 3 * g
        zbs.append(pl.pallas_call(
            functools.partial(_proj_group_kernel, blk=blk, row_chunk=row_chunk),
            grid=(b * nsb, 3),
            in_specs=[h_spec, pl.BlockSpec((d, COL_TILE), lambda i, j, first=first: (0, first + j)), tab_spec],
            out_specs=zb_spec(dil, 0),
            out_shape=zb_shape(dil),
            compiler_params=_cparams(("parallel", "arbitrary")),
            name=f"proj_group{g}",
        )(hs[g], w, _residue_major_rows(tab, blk, dil)))
    return hm, zg, *zbs


def _diffattn_kernel(q_ref, k_ref, v_ref, lq1_ref, lk1_ref, lq2_ref, lk2_ref, gs_ref, o_ref,
                     v2_scr, *, tk, lambda_init):
    qi = pl.program_id(2)
    tq = q_ref.shape[0]
    seq = k_ref.shape[0]

    @pl.when(qi == 0)
    def _():
        v2_scr[:, :DA_PAIR] = v_ref[...]
        v2_scr[:, DA_PAIR:] = jnp.ones((seq, DA_PAIR), BF16)

    lane = lax.broadcasted_iota(jnp.int32, (tq, DA_PAIR), 1)
    q = q_ref[...]
    zero = jnp.zeros_like(q)
    first = _first_head_lanes(lane)
    qmaps = (jnp.where(first, q, zero), jnp.where(first, zero, q))
    m = [None, None]
    acc = [None, None]
    for kk in range(seq // tk):
        kb = k_ref[kk * tk:(kk + 1) * tk, :]
        vb = v2_scr[kk * tk:(kk + 1) * tk, :]
        for mp in range(2):
            s = _nt_dot(qmaps[mp], kb)
            m_cur = jnp.max(s, axis=1, keepdims=True)
            if kk == 0:
                m_new = jnp.broadcast_to(m_cur, (tq, LANES))
            else:
                m_new = jnp.maximum(m[mp], m_cur)
            p = jnp.exp2(s - jnp.tile(m_new, (1, tk // LANES)))
            pv = _dot(p.astype(BF16), vb)
            if kk == 0:
                acc[mp] = pv
            else:
                alpha = jnp.exp2(m[mp] - m_new)
                acc[mp] = acc[mp] * jnp.tile(alpha, (1, 2)) + pv
            m[mp] = m_new

    o1 = acc[0][:, :DA_PAIR] / acc[0][:, DA_PAIR:]
    o2 = acc[1][:, :DA_PAIR] / acc[1][:, DA_PAIR:]
    lam = (jnp.exp(jnp.sum(lq1_ref[...] * lk1_ref[...], axis=1, keepdims=True))
           - jnp.exp(jnp.sum(lq2_ref[...] * lk2_ref[...], axis=1, keepdims=True)) + lambda_init)
    o = o1 - lam * o2
    o_ref[...] = ((_rms(o) * gs_ref[...]) * (1.0 - lambda_init)).astype(BF16)


def _diffattn(hm, lam_q1, lam_k1, lam_q2, lam_k2, g_subln, lambda_init):
    b, nh3, seq, _ = hm.shape
    nh = nh3 // 3
    tq = seq if seq <= DA_WHOLE_SEQ else DA_Q_ROWS
    tk = min(DA_KV_CHUNK, seq)
    lam_spec = pl.BlockSpec((1, DA_HEAD_DIM), lambda bi, h, qi: (0, 0))
    return pl.pallas_call(
        functools.partial(_diffattn_kernel, tk=tk, lambda_init=lambda_init),
        grid=(b, nh, seq // tq),
        in_specs=[pl.BlockSpec((None, None, tq, DA_PAIR), lambda bi, h, qi: (bi, h, qi, 0)),
                  pl.BlockSpec((None, None, seq, DA_PAIR), lambda bi, h, qi: (bi, nh + h, 0, 0)),
                  pl.BlockSpec((None, None, seq, DA_PAIR), lambda bi, h, qi: (bi, 2 * nh + h, 0, 0)),
                  lam_spec, lam_spec, lam_spec, lam_spec,
                  pl.BlockSpec((1, DA_PAIR), lambda bi, h, qi: (0, 0))],
        out_specs=pl.BlockSpec((None, tq, DA_PAIR), lambda bi, h, qi: (bi, qi, h)),
        out_shape=jax.ShapeDtypeStruct((b, seq, nh * DA_PAIR), BF16),
        scratch_shapes=[pltpu.VMEM((seq, 2 * DA_PAIR), BF16)],
        compiler_params=_cparams(("parallel", "parallel", "arbitrary")),
        name="diffattn",
    )(hm, hm, hm, lam_q1.reshape(1, -1), lam_k1.reshape(1, -1), lam_q2.reshape(1, -1),
      lam_k2.reshape(1, -1), g_subln.reshape(1, -1))


def _dilated_kernel(q_ref, k_ref, v_ref, o_ref, lse_ref, *, sub, kw, radius):
    n = pl.program_id(2)
    n_cls, tq, _ = q_ref.shape
    t = k_ref.shape[1]
    lane = lax.broadcasted_iota(jnp.int32, (sub, LANES), 1)
    low = lane < DL_HEAD_DIM
    first = _first_head_lanes(lane)
    zero = jnp.zeros((sub, LANES), BF16)
    ones = jnp.ones((kw, LANES), BF16)
    n_pairs = DL_W // LANES
    for c in range(n_cls):
        o_rows, lse_rows = [], []
        for sb in range(tq // sub):
            q0 = n * tq + sb * sub
            ks = pl.multiple_of(jnp.clip(q0 - radius, 0, t - kw), DL_BLOCK)
            qpos = lax.broadcasted_iota(jnp.int32, (sub, kw), 0) + q0
            kpos = lax.broadcasted_iota(jnp.int32, (sub, kw), 1) + ks
            band = jnp.abs(kpos - qpos) <= radius
            band = jnp.concatenate([band, band], axis=0)
            all_scores = []
            for j in range(n_pairs):
                cols = slice(j * LANES, (j + 1) * LANES)
                qp = q_ref[c, sb * sub:(sb + 1) * sub, cols]
                qcat = jnp.concatenate([jnp.where(first, qp, zero), jnp.where(first, zero, qp)], axis=0)
                all_scores.append(_nt_dot(qcat, k_ref[c, pl.ds(ks, kw), cols]))
            ms, dens, o_cols = [], [], []
            for j in range(n_pairs):
                cols = slice(j * LANES, (j + 1) * LANES)
                vp = jnp.concatenate([v_ref[c, pl.ds(ks, kw), cols], ones], axis=1)
                s = jnp.where(band, all_scores[j], NEG)
                m = jnp.max(s, axis=1, keepdims=True)
                p = jnp.exp2(s - m)
                r = _dot(p.astype(BF16), vp)
                den = r[:, LANES:]
                o2 = r[:, :LANES] / den
                ms += [m[:sub], m[sub:]]
                dens += [den[:sub], den[sub:]]
                o_cols.append(jnp.where(low, o2[:sub], o2[sub:]).astype(BF16))
            m_tile = jnp.zeros((sub, LANES), F32)
            den_tile = jnp.ones((sub, LANES), F32)
            for hd in range(len(ms)):
                m_tile = jnp.where(lane == hd, ms[hd], m_tile)
                den_tile = jnp.where(lane == hd, dens[hd], den_tile)
            lse_tile = m_tile * LN2 + jnp.log(den_tile)
            o_rows.append(jnp.concatenate(o_cols, axis=1))
            lse_rows.append(lse_tile)
        o_ref[c] = jnp.concatenate(o_rows, axis=0)
        lse_ref[c] = jnp.concatenate(lse_rows, axis=0)


def _dilated(zb, g, window, dil):
    b, _, t, _ = zb.shape
    radius = window // (2 * dil)
    rows = min(DL_STEP_ROWS, dil * t)
    tq = min(rows, t)
    n_cls = rows // tq
    sub = min(2 * radius, tq)
    kw = min(sub + 2 * radius, t)
    return pl.pallas_call(
        functools.partial(_dilated_kernel, sub=sub, kw=kw, radius=radius),
        grid=(b, dil // n_cls, t // tq),
        in_specs=[pl.BlockSpec((None, n_cls, tq, DL_W), lambda bi, c, n: (bi, c, n, 0)),
                  pl.BlockSpec((None, n_cls, t, DL_W), lambda bi, c, n: (bi, c, 0, 1)),
                  pl.BlockSpec((None, n_cls, t, DL_W), lambda bi, c, n: (bi, c, 0, 2))],
        out_specs=[pl.BlockSpec((None, n_cls, tq, DL_W), lambda bi, c, n: (bi, c, n, 0)),
                   pl.BlockSpec((None, n_cls, tq, LANES), lambda bi, c, n: (bi, c, n, 0))],
        out_shape=[jax.ShapeDtypeStruct((b, dil, t, DL_W), BF16),
                   jax.ShapeDtypeStruct((b, dil, t, LANES), F32)],
        compiler_params=_cparams(("parallel", "parallel", "arbitrary")),
        name=f"dilated{g}",
    )(zb, zb, zb)


def _natural_order(ref, scr):
    dil, n, w = ref.shape
    slabs = []
    for cb in range(w // LANES):
        cols = slice(cb * LANES, (cb + 1) * LANES)
        if dil == 1:
            slabs.append(ref[0, :, cols].astype(F32))
            continue
        for c in range(dil):
            scr[cb, pl.ds(c, n, stride=dil), :] = ref[c, :, cols].astype(F32)
        slabs.append(scr[cb])
    return slabs


def _tail_kernel(x_ref, mod_ref, oa_ref, o0_ref, o1_ref, o2_ref, l0_ref, l1_ref, l2_ref,
                 ga_ref, gb_ref, wa_ref, wb_ref, wo_ref, ex_ref, g_ref, wgu_ref, wd_ref, gf_ref, o_ref,
                 os1_scr, os2_scr, ls1_scr, ls2_scr, *, chunk, final_norm):
    l0, = _natural_order(l0_ref, None)
    l1, = _natural_order(l1_ref, ls1_scr)
    l2, = _natural_order(l2_ref, ls2_scr)
    o0 = _natural_order(o0_ref, None)
    o1 = _natural_order(o1_ref, os1_scr)
    o2 = _natural_order(o2_ref, os2_scr)
    mx = jnp.maximum(jnp.maximum(l0, l1), l2)
    e0, e1, e2 = jnp.exp(l0 - mx), jnp.exp(l1 - mx), jnp.exp(l2 - mx)
    inv = 1.0 / (e0 + e1 + e2)
    ob = None
    for e, o in ((e0, o0), (e1, o1), (e2, o2)):
        w = e * inv
        w_hi = w.astype(BF16)
        w_lo = (w - w_hi.astype(F32)).astype(BF16)
        wx = _dot(jnp.concatenate([w_hi, w_lo], axis=1), ex_ref[...])
        term = [wx[:, cb * LANES:(cb + 1) * LANES] * o[cb] for cb in range(len(o))]
        ob = term if ob is None else [a + t for a, t in zip(ob, term)]
    ob = jnp.concatenate(ob, axis=1)
    ya = _dot(oa_ref[...], wa_ref[...])
    yb = _dot(ob.astype(BF16), wb_ref[...])
    ga = ga_ref[...].astype(F32)
    gb = gb_ref[...].astype(F32)
    merged = ya / (1.0 + jnp.exp(-ga)) + yb / (1.0 + jnp.exp(-gb))
    y = _dot(merged.astype(BF16), wo_ref[...])
    x1 = x_ref[...] + mod_ref[2:3, :] * y
    h = ((_rms(x1) * g_ref[...]) * (1.0 + mod_ref[4:5, :]) + mod_ref[3:4, :]).astype(BF16)
    hidden = wd_ref.shape[0]
    acc = None
    for c0 in range(0, hidden, chunk):
        c1 = min(c0 + chunk, hidden)
        cols = slice(c0, c1)
        gate = _dot(h, wgu_ref[:, cols])
        up = _dot(h, wgu_ref[:, hidden + c0:hidden + c1])
        f = (gate / (1.0 + jnp.exp(-gate))) * up
        part = _dot(f.astype(BF16), wd_ref[cols, :])
        acc = part if acc is None else acc + part
    x2 = x1 + mod_ref[5:6, :] * acc
    o_ref[...] = _rms(x2) * gf_ref[...] if final_norm else x2


def _tail(x, mod, zg, oa, obs, lses, wa_bf, wb_bf, wo_bf, g_ffn, wgu_bf, wd_bf, g_final, final_norm):
    b, s, d = x.shape
    m = b * s
    tm = min(TAIL_ROWS, s)
    nsb = s // tm
    hidden = wd_bf.shape[0]
    chunk = FFN_CHUNK
    assert hidden % LANES == 0
    row = lambda w: pl.BlockSpec((tm, w), lambda i: (i, 0))
    full = lambda shape: pl.BlockSpec(shape, lambda i: (0, 0), pipeline_mode=pl.Buffered(1))

    def residue_major(dil, w):
        return pl.BlockSpec((None, dil, tm // dil, w), lambda i: (i // nsb, 0, i % nsb, 0))

    dils = [dil for _, dil in DL_PAIRS]
    head = jnp.arange(LANES)[:, None]
    col = jnp.arange(DL_W)[None, :]
    expand = ((col // DL_HEAD_DIM) == head).astype(BF16)
    expand = jnp.concatenate([expand, expand], axis=0)
    return pl.pallas_call(
        functools.partial(_tail_kernel, chunk=chunk, final_norm=final_norm),
        grid=(m // tm,),
        in_specs=[row(d), pl.BlockSpec((None, 6, d), lambda i: (i // nsb, 0, 0)),
                  row(DA_HEADS * DA_PAIR)]
                 + [residue_major(dil, DL_W) for dil in dils]
                 + [residue_major(dil, LANES) for dil in dils]
                 + [pl.BlockSpec((tm, d), lambda i: (i, 0)), pl.BlockSpec((tm, d), lambda i: (i, 1)),
                    full(wa_bf.shape), full(wb_bf.shape), full(wo_bf.shape), full(expand.shape),
                    full((1, d)), full(wgu_bf.shape), full(wd_bf.shape), full((1, d))],
        out_specs=row(d),
        out_shape=jax.ShapeDtypeStruct((m, d), F32),
        scratch_shapes=[pltpu.VMEM((DL_W // LANES, tm, LANES), F32),
                        pltpu.VMEM((DL_W // LANES, tm, LANES), F32),
                        pltpu.VMEM((1, tm, LANES), F32), pltpu.VMEM((1, tm, LANES), F32)],
        compiler_params=_cparams(("parallel",)),
        name="tail",
    )(x.reshape(m, d), mod, oa.reshape(m, -1), *obs, *lses, zg, zg, wa_bf, wb_bf, wo_bf, expand,
      g_ffn.reshape(1, d), wgu_bf, wd_bf, g_final.reshape(1, d))


def _trunk(x, mod_all, weights):
    (g_mix, w_in, lam_q1, lam_k1, lam_q2, lam_k2, g_subln, w_a, w_b, w_out, g_ffn,
     w_gu, w_down, g_final) = weights
    b, s, d = x.shape
    depth = w_in.shape[0]
    for l in range(depth):
        lambda_init = 0.8 - 0.6 * math.exp(-0.3 * l)
        mod = mod_all[l]
        hm, zg, *zbs = _inproj(x, mod, g_mix[l], w_in[l])
        oa = _diffattn(hm, lam_q1[l], lam_k1[l], lam_q2[l], lam_k2[l], g_subln[l], lambda_init)
        obs, lses = [], []
        for g, (window, dil) in enumerate(DL_PAIRS):
            o, lse = _dilated(zbs[g], g, window, dil)
            obs.append(o)
            lses.append(lse)
        x = _tail(x, mod, zg, oa, obs, lses, w_a[l].astype(BF16), w_b[l].astype(BF16),
                  w_out[l].astype(BF16), g_ffn[l], w_gu[l].astype(BF16), w_down[l].astype(BF16),
                  g_final, l == depth - 1).reshape(b, s, d)
    return x


def kernel(x_prompt, x_sample, c_prompt, c_sample, w_ada, b_ada, g_mix, w_in, lam_q1, lam_k1,
           lam_q2, lam_k2, g_subln, w_a, w_b, w_out, g_ffn, w_gu, w_down, g_final):
    depth = w_in.shape[0]
    bp = c_prompt.shape[0]
    c_all = jnp.concatenate([c_prompt, c_sample], axis=0)
    mods = [_ada(c_all, w_ada[l], b_ada[l]).reshape(c_all.shape[0], 6, D_MODEL) for l in range(depth)]
    weights = (g_mix, w_in, lam_q1, lam_k1, lam_q2, lam_k2, g_subln, w_a, w_b, w_out, g_ffn,
               w_gu, w_down, g_final)
    y_prompt = _trunk(x_prompt, [m[:bp] for m in mods], weights)
    y_sample = _trunk(x_sample, [m[bp:] for m in mods], weights)
    return (y_prompt, y_sample)
```

```python
import functools
import math

import jax
import jax.numpy as jnp
from jax import lax
from jax.experimental import pallas as pl
from jax.experimental.pallas import tpu as pltpu

F32 = jnp.float32
BF16 = jnp.bfloat16

D_MODEL = 1024
DA_HEADS = 8
DA_HEAD_DIM = 64
DA_PAIR = 2 * DA_HEAD_DIM
DL_PAIRS = ((128, 1), (512, 4), (2048, 16))
DL_HEADS = 8
DL_HEAD_DIM = 64
DL_W = DL_HEADS * DL_HEAD_DIM
DL_BLOCK = 64
ROPE_THETA = 500000.0
ROPE_ROT = DL_HEAD_DIM // 4
ROPE_HALF = ROPE_ROT // 2
EPS = 1e-6
NEG = -1e30
LOG2E = math.log2(math.e)
LN2 = math.log(2.0)
Q_SCALE = DA_HEAD_DIM ** -0.5 * LOG2E

LANES = 128
COL_TILE = 512
N_IN = 2 * DA_HEADS * DA_PAIR + DA_HEADS * DA_PAIR + 9 * DL_W + 2 * D_MODEL
N_COL_TILES = N_IN // COL_TILE
CT_QA, CT_KA, CT_VA = 0, 2, 4
CT_DL = 6
CT_GA = 15
VMEM_LIMIT = 56 * 1024 * 1024

NORM_ROWS = 1024
PERM_ROWS = 256
PROJ_ROWS = 2048
PROJ_ROW_CHUNK = 256
DA_WHOLE_SEQ = 2048
DA_Q_ROWS = 2048
DA_KV_CHUNK = 256
DL_STEP_ROWS = 2048
TAIL_ROWS = 512
FFN_CHUNK = 768


def _cparams(sem):
    return pltpu.CompilerParams(dimension_semantics=sem, vmem_limit_bytes=VMEM_LIMIT)


def _nt_dot(a, b):
    return lax.dot_general(a, b, (((1,), (1,)), ((), ())), preferred_element_type=F32)


def _dot(a, b):
    return jnp.dot(a, b, preferred_element_type=F32)


def _rms(x):
    return x * lax.rsqrt(jnp.mean(x * x, axis=-1, keepdims=True) + EPS)


def _ada_kernel(c_ref, w_ref, b_ref, o_ref):
    c = c_ref[...]
    a = c * (1.0 / (1.0 + jnp.exp(-c)))
    w = w_ref[...]
    a_hi = a.astype(BF16)
    a_lo = (a - a_hi.astype(F32)).astype(BF16)
    w_hi = w.astype(BF16)
    w_lo = (w - w_hi.astype(F32)).astype(BF16)
    o_ref[...] = _dot(a_hi, w_hi) + _dot(a_hi, w_lo) + _dot(a_lo, w_hi) + b_ref[...]


def _ada(c, w_ada, b_ada):
    nb, d = c.shape
    n = w_ada.shape[1]
    tn = COL_TILE
    return pl.pallas_call(
        _ada_kernel,
        grid=(n // tn,),
        in_specs=[pl.BlockSpec((nb, d), lambda j: (0, 0)),
                  pl.BlockSpec((d, tn), lambda j: (0, j)),
                  pl.BlockSpec((1, tn), lambda j: (0, j))],
        out_specs=pl.BlockSpec((nb, tn), lambda j: (0, j)),
        out_shape=jax.ShapeDtypeStruct((nb, n), F32),
        compiler_params=_cparams(("arbitrary",)),
        name="ada",
    )(c, w_ada, b_ada.reshape(1, n))


def _norm_kernel(x_ref, mod_ref, g_ref, p1_ref, p2_ref, h0_ref, h1_ref, h2_ref):
    blk = p1_ref.shape[0]
    for r in range(x_ref.shape[0] // blk):
        rows = pl.ds(r * blk, blk)
        y = _rms(x_ref[rows, :]) * g_ref[...]
        h = (y * (1.0 + mod_ref[1:2, :]) + mod_ref[0:1, :]).astype(BF16)
        h0_ref[rows, :] = h
        h1_ref[rows, :] = _dot(p1_ref[...], h).astype(BF16)
        h2_ref[rows, :] = _dot(p2_ref[...], h).astype(BF16)


def _residue_major_perm(rows, dil):
    dst = jnp.arange(rows)
    src = (dst % (rows // dil)) * dil + dst // (rows // dil)
    return (jnp.arange(rows)[None, :] == src[:, None]).astype(BF16)


def _norm(x, mod, g_mix, blk):
    b, s, d = x.shape
    m = b * s
    tm = min(NORM_ROWS, s)
    nsb = s // tm
    row = pl.BlockSpec((tm, d), lambda i: (i, 0))
    const = lambda shape: pl.BlockSpec(shape, lambda i: (0, 0), pipeline_mode=pl.Buffered(1))
    perms = [_residue_major_perm(blk, dil) for _, dil in DL_PAIRS[1:]]
    return pl.pallas_call(
        _norm_kernel,
        grid=(m // tm,),
        in_specs=[row, pl.BlockSpec((None, 6, d), lambda i: (i // nsb, 0, 0)), const((1, d)),
                  const((blk, blk)), const((blk, blk))],
        out_specs=[row, row, row],
        out_shape=[jax.ShapeDtypeStruct((m, d), BF16)] * 3,
        compiler_params=_cparams(("parallel",)),
        name="norm",
    )(x.reshape(m, d), mod, g_mix.reshape(1, d), *perms)


def _project(h_ref, w_ref, tab_ref, scale, emit, row_chunk):
    j = pl.program_id(1)
    for r in range(h_ref.shape[0] // row_chunk):
        rows = pl.ds(r * row_chunk, row_chunk)
        z = _dot(h_ref[rows, :], w_ref[j])
        if scale is not None:
            cs, sn = scale(tab_ref[0, rows, :], tab_ref[1, rows, :])
            z = jnp.concatenate(
                [zc * cs + pltpu.roll(zc, LANES // 2, 1) * sn
                 for zc in (z[:, cb * LANES:(cb + 1) * LANES] for cb in range(COL_TILE // LANES))],
                axis=1)
        emit(r * row_chunk, z.astype(BF16))


def _qkv_scale(kind):
    def scale(cs, sn):
        f = jnp.where(kind == 0, Q_SCALE, 1.0).astype(F32)
        return jnp.where(kind < 2, cs * f, 1.0), jnp.where(kind < 2, sn * f, 0.0)
    return scale


def _residue_major_emit(zb_ref, blk, row_chunk):
    dil = zb_ref.shape[0]
    n = blk // dil
    piece = min(n, row_chunk)

    def emit(row0, z):
        for p in range(row_chunk // piece):
            r0 = row0 + p * piece
            pb, ro = divmod(r0, blk)
            zb_ref[ro // n, pb * n + ro % n:pb * n + ro % n + piece, :] = z[p * piece:(p + 1) * piece, :]
    return emit


_N_MAIN = CT_DL + 3
_N_MAIN_TILES = _N_MAIN + (N_COL_TILES - CT_GA)


def _proj_main_kernel(h_ref, w_ref, tab_ref, hm_ref, zb_ref, zg_ref, *, blk, row_chunk):
    j = pl.program_id(1)

    def head_major(row0, z):
        for hh in range(COL_TILE // DA_PAIR):
            hm_ref[hh, row0:row0 + row_chunk, :] = z[:, hh * DA_PAIR:(hh + 1) * DA_PAIR]

    def gates(row0, z):
        zg_ref[row0:row0 + row_chunk, :] = z

    @pl.when(j < CT_KA)
    def _():
        _project(h_ref, w_ref, tab_ref, lambda cs, sn: (cs * Q_SCALE, sn * Q_SCALE), head_major, row_chunk)

    @pl.when((j >= CT_KA) & (j < CT_VA))
    def _():
        _project(h_ref, w_ref, tab_ref, lambda cs, sn: (cs, sn), head_major, row_chunk)

    @pl.when((j >= CT_VA) & (j < CT_DL))
    def _():
        _project(h_ref, w_ref, tab_ref, None, head_major, row_chunk)

    @pl.when((j >= CT_DL) & (j < _N_MAIN))
    def _():
        _project(h_ref, w_ref, tab_ref, _qkv_scale(j - CT_DL),
                 _residue_major_emit(zb_ref, blk, row_chunk), row_chunk)

    @pl.when(j >= _N_MAIN)
    def _():
        _project(h_ref, w_ref, tab_ref, None, gates, row_chunk)


def _proj_group_kernel(h_ref, w_ref, tab_ref, zb_ref, *, blk, row_chunk):
    _project(h_ref, w_ref, tab_ref, _qkv_scale(pl.program_id(1)),
             _residue_major_emit(zb_ref, blk, row_chunk), row_chunk)


def _rotary_lane_layout(w):
    d, n = w.shape
    blk = w.reshape(d, n // LANES, LANES)
    a, b, h = ROPE_HALF, ROPE_ROT, DL_HEAD_DIM
    swapped = jnp.concatenate([blk[..., :a], blk[..., h:h + a], blk[..., b:h], blk[..., a:b],
                               blk[..., h + a:]], axis=-1)
    per_tile = COL_TILE // LANES
    qk_tiles = [CT_QA, CT_QA + 1, CT_KA, CT_KA + 1] + [CT_DL + 3 * g + t for g in range(len(DL_PAIRS))
                                                      for t in (0, 1)]
    qk_blocks = {t * per_tile + c for t in qk_tiles for c in range(per_tile)}
    is_qk = jnp.asarray([i in qk_blocks for i in range(n // LANES)])
    return jnp.where(is_qk[None, :, None], swapped, blk).reshape(d, n)


def _first_head_lanes(lane):
    return (lane < ROPE_HALF) | ((lane >= ROPE_ROT) & (lane < DL_HEAD_DIM + ROPE_HALF))


def _rope_tables(seq):
    inv = 1.0 / (ROPE_THETA ** (jnp.arange(0, ROPE_ROT, 2, dtype=F32) / ROPE_ROT))
    ang = jnp.arange(seq, dtype=F32)[:, None] * inv[None, :]
    cos, sin = jnp.cos(ang), jnp.sin(ang)
    ones = jnp.ones((seq, DL_HEAD_DIM - ROPE_ROT), F32)
    zeros = jnp.zeros((seq, DL_HEAD_DIM - ROPE_ROT), F32)
    cs = jnp.concatenate([cos, cos, ones, cos, cos, ones], axis=1)
    sn = jnp.concatenate([-sin, -sin, zeros, sin, sin, zeros], axis=1)
    return jnp.stack([cs, sn])


def _residue_major_rows(tab, blk, dil):
    k, s, w = tab.shape
    return tab.reshape(k, s // blk, blk // dil, dil, w).transpose(0, 1, 3, 2, 4).reshape(k, s, w)


def _inproj(x, mod, g_mix, w_in):
    b, s, d = x.shape
    m = b * s
    blk = min(PERM_ROWS, s)
    tm = min(PROJ_ROWS, s)
    nsb = s // tm
    row_chunk = min(PROJ_ROW_CHUNK, tm)
    for _, dil in DL_PAIRS:
        assert s % (dil * DL_BLOCK) == 0, "sequence must tile into dilated blocks without padding"
    hs = _norm(x, mod, g_mix, blk)
    w = _rotary_lane_layout(w_in).astype(BF16).reshape(d, N_COL_TILES, COL_TILE).transpose(1, 0, 2)
    w_main = jnp.concatenate([w[:_N_MAIN], w[CT_GA:]], axis=0)

    def w_spec(n_tiles):
        return pl.BlockSpec((n_tiles, d, COL_TILE), lambda i, j: (0, 0, 0), pipeline_mode=pl.Buffered(1))

    tab = _rope_tables(s)
    h_spec = pl.BlockSpec((tm, d), lambda i, j: (i, 0))
    tab_spec = pl.BlockSpec((2, tm, LANES), lambda i, j: (0, i % nsb, 0))

    def zb_shape(dil):
        return jax.ShapeDtypeStruct((b, dil, s // dil, 3 * DL_W), BF16)

    def zb_spec(dil, first):
        return pl.BlockSpec((None, dil, tm // dil, COL_TILE),
                            lambda i, j: (i // nsb, 0, i % nsb, jnp.clip(j - first, 0, 2)))

    hm, zb0, zg = pl.pallas_call(
        functools.partial(_proj_main_kernel, blk=blk, row_chunk=row_chunk),
        grid=(b * nsb, _N_MAIN_TILES),
        in_specs=[h_spec, w_spec(_N_MAIN_TILES), tab_spec],
        out_specs=[pl.BlockSpec((None, COL_TILE // DA_PAIR, tm, DA_PAIR),
                                lambda i, j: (i // nsb, jnp.minimum(j, CT_DL - 1), i % nsb, 0)),
                   zb_spec(1, CT_DL),
                   pl.BlockSpec((tm, COL_TILE), lambda i, j: (i, jnp.maximum(j - _N_MAIN, 0)))],
        out_shape=[jax.ShapeDtypeStruct((b, 3 * DA_HEADS, s, DA_PAIR), BF16), zb_shape(1),
                   jax.ShapeDtypeStruct((m, (N_COL_TILES - CT_GA) * COL_TILE), BF16)],
        compiler_params=_cparams(("parallel", "arbitrary")),
        name="proj_main",
    )(hs[0], w_main, tab)
    zbs = [zb0]
    for g in range(1, len(DL_PAIRS)):
        dil = DL_PAIRS[g][1]
        first = CT_DL + 3 * g
        zbs.append(pl.pallas_call(
            functools.partial(_proj_group_kernel, blk=blk, row_chunk=row_chunk),
            grid=(b * nsb, 3),
            in_specs=[h_spec, w_spec(3), tab_spec],
            out_specs=zb_spec(dil, 0),
            out_shape=zb_shape(dil),
            compiler_params=_cparams(("parallel", "arbitrary")),
            name=f"proj_group{g}",
        )(hs[g], w[first:first + 3], _residue_major_rows(tab, blk, dil)))
    return hm, zg, *zbs


def _diffattn_kernel(q_ref, k_ref, v_ref, lq1_ref, lk1_ref, lq2_ref, lk2_ref, gs_ref, o_ref,
                     v2_scr, *, tk, lambda_init):
    qi = pl.program_id(2)
    tq = q_ref.shape[0]
    seq = k_ref.shape[0]

    @pl.when(qi == 0)
    def _():
        v2_scr[:, :DA_PAIR] = v_ref[...]
        v2_scr[:, DA_PAIR:] = jnp.ones((seq, DA_PAIR), BF16)

    lane = lax.broadcasted_iota(jnp.int32, (tq, DA_PAIR), 1)
    q = q_ref[...]
    zero = jnp.zeros_like(q)
    first = _first_head_lanes(lane)
    qmaps = (jnp.where(first, q, zero), jnp.where(first, zero, q))
    m = [None, None]
    acc = [None, None]
    for kk in range(seq // tk):
        kb = k_ref[kk * tk:(kk + 1) * tk, :]
        vb = v2_scr[kk * tk:(kk + 1) * tk, :]
        for mp in range(2):
            s = _nt_dot(qmaps[mp], kb)
            m_cur = jnp.max(s, axis=1, keepdims=True)
            if kk == 0:
                m_new = jnp.broadcast_to(m_cur, (tq, LANES))
            else:
                m_new = jnp.maximum(m[mp], m_cur)
            p = jnp.exp2(s - jnp.tile(m_new, (1, tk // LANES)))
            pv = _dot(p.astype(BF16), vb)
            if kk == 0:
                acc[mp] = pv
            else:
                alpha = jnp.exp2(m[mp] - m_new)
                acc[mp] = acc[mp] * jnp.tile(alpha, (1, 2)) + pv
            m[mp] = m_new

    o1 = acc[0][:, :DA_PAIR] / acc[0][:, DA_PAIR:]
    o2 = acc[1][:, :DA_PAIR] / acc[1][:, DA_PAIR:]
    lam = (jnp.exp(jnp.sum(lq1_ref[...] * lk1_ref[...], axis=1, keepdims=True))
           - jnp.exp(jnp.sum(lq2_ref[...] * lk2_ref[...], axis=1, keepdims=True)) + lambda_init)
    o = o1 - lam * o2
    o_ref[...] = ((_rms(o) * gs_ref[...]) * (1.0 - lambda_init)).astype(BF16)


def _diffattn(hm, lam_q1, lam_k1, lam_q2, lam_k2, g_subln, lambda_init):
    b, nh3, seq, _ = hm.shape
    nh = nh3 // 3
    tq = seq if seq <= DA_WHOLE_SEQ else DA_Q_ROWS
    tk = min(DA_KV_CHUNK, seq)
    lam_spec = pl.BlockSpec((1, DA_HEAD_DIM), lambda bi, h, qi: (0, 0))
    return pl.pallas_call(
        functools.partial(_diffattn_kernel, tk=tk, lambda_init=lambda_init),
        grid=(b, nh, seq // tq),
        in_specs=[pl.BlockSpec((None, None, tq, DA_PAIR), lambda bi, h, qi: (bi, h, qi, 0)),
                  pl.BlockSpec((None, None, seq, DA_PAIR), lambda bi, h, qi: (bi, nh + h, 0, 0)),
                  pl.BlockSpec((None, None, seq, DA_PAIR), lambda bi, h, qi: (bi, 2 * nh + h, 0, 0)),
                  lam_spec, lam_spec, lam_spec, lam_spec,
                  pl.BlockSpec((1, DA_PAIR), lambda bi, h, qi: (0, 0))],
        out_specs=pl.BlockSpec((None, tq, DA_PAIR), lambda bi, h, qi: (bi, qi, h)),
        out_shape=jax.ShapeDtypeStruct((b, seq, nh * DA_PAIR), BF16),
        scratch_shapes=[pltpu.VMEM((seq, 2 * DA_PAIR), BF16)],
        compiler_params=_cparams(("parallel", "parallel", "arbitrary")),
        name="diffattn",
    )(hm, hm, hm, lam_q1.reshape(1, -1), lam_k1.reshape(1, -1), lam_q2.reshape(1, -1),
      lam_k2.reshape(1, -1), g_subln.reshape(1, -1))


def _dilated_kernel(q_ref, k_ref, v_ref, o_ref, lse_ref, *, sub, kw, radius):
    n = pl.program_id(2)
    n_cls, tq, _ = q_ref.shape
    t = k_ref.shape[1]
    lane = lax.broadcasted_iota(jnp.int32, (sub, LANES), 1)
    low = lane < DL_HEAD_DIM
    first = _first_head_lanes(lane)
    zero = jnp.zeros((sub, LANES), BF16)
    ones = jnp.ones((kw, LANES), BF16)
    n_pairs = DL_W // LANES
    for c in range(n_cls):
        o_rows, lse_rows = [], []
        for sb in range(tq // sub):
            q0 = n * tq + sb * sub
            ks = pl.multiple_of(jnp.clip(q0 - radius, 0, t - kw), DL_BLOCK)
            qpos = lax.broadcasted_iota(jnp.int32, (sub, kw), 0) + q0
            kpos = lax.broadcasted_iota(jnp.int32, (sub, kw), 1) + ks
            band = jnp.abs(kpos - qpos) <= radius
            band = jnp.concatenate([band, band], axis=0)
            all_scores = []
            for j in range(n_pairs):
                cols = slice(j * LANES, (j + 1) * LANES)
                qp = q_ref[c, sb * sub:(sb + 1) * sub, cols]
                qcat = jnp.concatenate([jnp.where(first, qp, zero), jnp.where(first, zero, qp)], axis=0)
                all_scores.append(_nt_dot(qcat, k_ref[c, pl.ds(ks, kw), cols]))
            ms, dens, o_cols = [], [], []
            for j in range(n_pairs):
                cols = slice(j * LANES, (j + 1) * LANES)
                vp = jnp.concatenate([v_ref[c, pl.ds(ks, kw), cols], ones], axis=1)
                s = jnp.where(band, all_scores[j], NEG)
                m = jnp.max(s, axis=1, keepdims=True)
                p = jnp.exp2(s - m)
                r = _dot(p.astype(BF16), vp)
                den = r[:, LANES:]
                o2 = r[:, :LANES] / den
                ms += [m[:sub], m[sub:]]
                dens += [den[:sub], den[sub:]]
                o_cols.append(jnp.where(low, o2[:sub], o2[sub:]).astype(BF16))
            m_tile = jnp.zeros((sub, LANES), F32)
            den_tile = jnp.ones((sub, LANES), F32)
            for hd in range(len(ms)):
                m_tile = jnp.where(lane == hd, ms[hd], m_tile)
                den_tile = jnp.where(lane == hd, dens[hd], den_tile)
            lse_tile = m_tile * LN2 + jnp.log(den_tile)
            o_rows.append(jnp.concatenate(o_cols, axis=1))
            lse_rows.append(lse_tile)
        o_ref[c] = jnp.concatenate(o_rows, axis=0)
        lse_ref[c] = jnp.concatenate(lse_rows, axis=0)


def _dilated(zb, g, window, dil):
    b, _, t, _ = zb.shape
    radius = window // (2 * dil)
    rows = min(DL_STEP_ROWS, dil * t)
    tq = min(rows, t)
    n_cls = rows // tq
    sub = min(2 * radius, tq)
    kw = min(sub + 2 * radius, t)
    return pl.pallas_call(
        functools.partial(_dilated_kernel, sub=sub, kw=kw, radius=radius),
        grid=(b, dil // n_cls, t // tq),
        in_specs=[pl.BlockSpec((None, n_cls, tq, DL_W), lambda bi, c, n: (bi, c, n, 0)),
                  pl.BlockSpec((None, n_cls, t, DL_W), lambda bi, c, n: (bi, c, 0, 1)),
                  pl.BlockSpec((None, n_cls, t, DL_W), lambda bi, c, n: (bi, c, 0, 2))],
        out_specs=[pl.BlockSpec((None, n_cls, tq, DL_W), lambda bi, c, n: (bi, c, n, 0)),
                   pl.BlockSpec((None, n_cls, tq, LANES), lambda bi, c, n: (bi, c, n, 0))],
        out_shape=[jax.ShapeDtypeStruct((b, dil, t, DL_W), BF16),
                   jax.ShapeDtypeStruct((b, dil, t, LANES), F32)],
        compiler_params=_cparams(("parallel", "parallel", "arbitrary")),
        name=f"dilated{g}",
    )(zb, zb, zb)


def _natural_order(ref, scr):
    dil, n, w = ref.shape
    slabs = []
    for cb in range(w // LANES):
        cols = slice(cb * LANES, (cb + 1) * LANES)
        if dil == 1:
            slabs.append(ref[0, :, cols].astype(F32))
            continue
        for c in range(dil):
            scr[cb, pl.ds(c, n, stride=dil), :] = ref[c, :, cols].astype(F32)
        slabs.append(scr[cb])
    return slabs


def _tail_kernel(x_ref, mod_ref, oa_ref, o0_ref, o1_ref, o2_ref, l0_ref, l1_ref, l2_ref,
                 ga_ref, gb_ref, wa_ref, wb_ref, wo_ref, ex_ref, g_ref, wgu_ref, wd_ref, gf_ref, o_ref,
                 os1_scr, os2_scr, ls1_scr, ls2_scr, *, chunk, final_norm):
    l0, = _natural_order(l0_ref, None)
    l1, = _natural_order(l1_ref, ls1_scr)
    l2, = _natural_order(l2_ref, ls2_scr)
    o0 = _natural_order(o0_ref, None)
    o1 = _natural_order(o1_ref, os1_scr)
    o2 = _natural_order(o2_ref, os2_scr)
    mx = jnp.maximum(jnp.maximum(l0, l1), l2)
    e0, e1, e2 = jnp.exp(l0 - mx), jnp.exp(l1 - mx), jnp.exp(l2 - mx)
    inv = 1.0 / (e0 + e1 + e2)
    ob = None
    for e, o in ((e0, o0), (e1, o1), (e2, o2)):
        w = e * inv
        w_hi = w.astype(BF16)
        w_lo = (w - w_hi.astype(F32)).astype(BF16)
        wx = _dot(jnp.concatenate([w_hi, w_lo], axis=1), ex_ref[...])
        term = [wx[:, cb * LANES:(cb + 1) * LANES] * o[cb] for cb in range(len(o))]
        ob = term if ob is None else [a + t for a, t in zip(ob, term)]
    ob = jnp.concatenate(ob, axis=1)
    ya = _dot(oa_ref[...], wa_ref[...])
    yb = _dot(ob.astype(BF16), wb_ref[...])
    ga = ga_ref[...].astype(F32)
    gb = gb_ref[...].astype(F32)
    merged = ya / (1.0 + jnp.exp(-ga)) + yb / (1.0 + jnp.exp(-gb))
    y = _dot(merged.astype(BF16), wo_ref[...])
    x1 = x_ref[...] + mod_ref[2:3, :] * y
    h = ((_rms(x1) * g_ref[...]) * (1.0 + mod_ref[4:5, :]) + mod_ref[3:4, :]).astype(BF16)
    hidden = wd_ref.shape[0]
    acc = None
    for c0 in range(0, hidden, chunk):
        c1 = min(c0 + chunk, hidden)
        cols = slice(c0, c1)
        gate = _dot(h, wgu_ref[:, cols])
        up = _dot(h, wgu_ref[:, hidden + c0:hidden + c1])
        f = (gate / (1.0 + jnp.exp(-gate))) * up
        part = _dot(f.astype(BF16), wd_ref[cols, :])
        acc = part if acc is None else acc + part
    x2 = x1 + mod_ref[5:6, :] * acc
    o_ref[...] = _rms(x2) * gf_ref[...] if final_norm else x2


def _tail(x, mod, zg, oa, obs, lses, wa_bf, wb_bf, wo_bf, g_ffn, wgu_bf, wd_bf, g_final, final_norm):
    b, s, d = x.shape
    m = b * s
    tm = min(TAIL_ROWS, s)
    nsb = s // tm
    hidden = wd_bf.shape[0]
    chunk = FFN_CHUNK
    assert hidden % LANES == 0
    row = lambda w: pl.BlockSpec((tm, w), lambda i: (i, 0))
    full = lambda shape: pl.BlockSpec(shape, lambda i: (0, 0), pipeline_mode=pl.Buffered(1))

    def residue_major(dil, w):
        return pl.BlockSpec((None, dil, tm // dil, w), lambda i: (i // nsb, 0, i % nsb, 0))

    dils = [dil for _, dil in DL_PAIRS]
    head = jnp.arange(LANES)[:, None]
    col = jnp.arange(DL_W)[None, :]
    expand = ((col // DL_HEAD_DIM) == head).astype(BF16)
    expand = jnp.concatenate([expand, expand], axis=0)
    return pl.pallas_call(
        functools.partial(_tail_kernel, chunk=chunk, final_norm=final_norm),
        grid=(m // tm,),
        in_specs=[row(d), pl.BlockSpec((None, 6, d), lambda i: (i // nsb, 0, 0)),
                  row(DA_HEADS * DA_PAIR)]
                 + [residue_major(dil, DL_W) for dil in dils]
                 + [residue_major(dil, LANES) for dil in dils]
                 + [pl.BlockSpec((tm, d), lambda i: (i, 0)), pl.BlockSpec((tm, d), lambda i: (i, 1)),
                    full(wa_bf.shape), full(wb_bf.shape), full(wo_bf.shape), full(expand.shape),
                    full((1, d)), full(wgu_bf.shape), full(wd_bf.shape), full((1, d))],
        out_specs=row(d),
        out_shape=jax.ShapeDtypeStruct((m, d), F32),
        scratch_shapes=[pltpu.VMEM((DL_W // LANES, tm, LANES), F32),
                        pltpu.VMEM((DL_W // LANES, tm, LANES), F32),
                        pltpu.VMEM((1, tm, LANES), F32), pltpu.VMEM((1, tm, LANES), F32)],
        compiler_params=_cparams(("parallel",)),
        name="tail",
    )(x.reshape(m, d), mod, oa.reshape(m, -1), *obs, *lses, zg, zg, wa_bf, wb_bf, wo_bf, expand,
      g_ffn.reshape(1, d), wgu_bf, wd_bf, g_final.reshape(1, d))


def _trunk(x, mod_all, weights):
    (g_mix, w_in, lam_q1, lam_k1, lam_q2, lam_k2, g_subln, w_a, w_b, w_out, g_ffn,
     w_gu, w_down, g_final) = weights
    b, s, d = x.shape
    depth = w_in.shape[0]
    for l in range(depth):
        lambda_init = 0.8 - 0.6 * math.exp(-0.3 * l)
        mod = mod_all[l]
        hm, zg, *zbs = _inproj(x, mod, g_mix[l], w_in[l])
        oa = _diffattn(hm, lam_q1[l], lam_k1[l], lam_q2[l], lam_k2[l], g_subln[l], lambda_init)
        obs, lses = [], []
        for g, (window, dil) in enumerate(DL_PAIRS):
            o, lse = _dilated(zbs[g], g, window, dil)
            obs.append(o)
            lses.append(lse)
        x = _tail(x, mod, zg, oa, obs, lses, w_a[l].astype(BF16), w_b[l].astype(BF16),
                  w_out[l].astype(BF16), g_ffn[l], w_gu[l].astype(BF16), w_down[l].astype(BF16),
                  g_final, l == depth - 1).reshape(b, s, d)
    return x


def kernel(x_prompt, x_sample, c_prompt, c_sample, w_ada, b_ada, g_mix, w_in, lam_q1, lam_k1,
           lam_q2, lam_k2, g_subln, w_a, w_b, w_out, g_ffn, w_gu, w_down, g_final):
    depth = w_in.shape[0]
    bp = c_prompt.shape[0]
    c_all = jnp.concatenate([c_prompt, c_sample], axis=0)
    mods = [_ada(c_all, w_ada[l], b_ada[l]).reshape(c_all.shape[0], 6, D_MODEL) for l in range(depth)]
    weights = (g_mix, w_in, lam_q1, lam_k1, lam_q2, lam_k2, g_subln, w_a, w_b, w_out, g_ffn,
               w_gu, w_down, g_final)
    y_prompt = _trunk(x_prompt, [m[:bp] for m in mods], weights)
    y_sample = _trunk(x_sample, [m[bp:] for m in mods], weights)
    return (y_prompt, y_sample)
```

```python
import functools
import math

import jax
import jax.numpy as jnp
from jax import lax
from jax.experimental import pallas as pl
from jax.experimental.pallas import tpu as pltpu

F32 = jnp.float32
BF16 = jnp.bfloat16

D_MODEL = 1024
DA_HEADS = 8
DA_HEAD_DIM = 64
DA_PAIR = 2 * DA_HEAD_DIM
DL_PAIRS = ((128, 1), (512, 4), (2048, 16))
DL_HEADS = 8
DL_HEAD_DIM = 64
DL_W = DL_HEADS * DL_HEAD_DIM
DL_BLOCK = 64
ROPE_THETA = 500000.0
ROPE_ROT = DL_HEAD_DIM // 4
ROPE_HALF = ROPE_ROT // 2
EPS = 1e-6
NEG = -1e30
LOG2E = math.log2(math.e)
LN2 = math.log(2.0)
Q_SCALE = DA_HEAD_DIM ** -0.5 * LOG2E

LANES = 128
COL_TILE = 512
N_IN = 2 * DA_HEADS * DA_PAIR + DA_HEADS * DA_PAIR + 9 * DL_W + 2 * D_MODEL
N_COL_TILES = N_IN // COL_TILE
CT_QA, CT_KA, CT_VA = 0, 2, 4
CT_DL = 6
CT_GA = 15
VMEM_LIMIT = 56 * 1024 * 1024

NORM_ROWS = 1024
PERM_ROWS = 256
PROJ_ROWS = 2048
PROJ_ROW_CHUNK = 256
DA_WHOLE_SEQ = 2048
DA_Q_ROWS = 2048
DA_KV_CHUNK = 256
DL_STEP_ROWS = 2048
TAIL_ROWS = 512
FFN_CHUNK = 768


def _cparams(sem):
    return pltpu.CompilerParams(dimension_semantics=sem, vmem_limit_bytes=VMEM_LIMIT)


def _nt_dot(a, b):
    return lax.dot_general(a, b, (((1,), (1,)), ((), ())), preferred_element_type=F32)


def _dot(a, b):
    return jnp.dot(a, b, preferred_element_type=F32)


def _rms(x):
    return x * lax.rsqrt(jnp.mean(x * x, axis=-1, keepdims=True) + EPS)


def _ada_kernel(c_ref, w_ref, b_ref, o_ref):
    c = c_ref[...]
    a = c * (1.0 / (1.0 + jnp.exp(-c)))
    w = w_ref[...]
    a_hi = a.astype(BF16)
    a_lo = (a - a_hi.astype(F32)).astype(BF16)
    w_hi = w.astype(BF16)
    w_lo = (w - w_hi.astype(F32)).astype(BF16)
    o_ref[...] = _dot(a_hi, w_hi) + _dot(a_hi, w_lo) + _dot(a_lo, w_hi) + b_ref[...]


def _ada(c, w_ada, b_ada):
    nb, d = c.shape
    n = w_ada.shape[1]
    tn = COL_TILE
    return pl.pallas_call(
        _ada_kernel,
        grid=(n // tn,),
        in_specs=[pl.BlockSpec((nb, d), lambda j: (0, 0)),
                  pl.BlockSpec((d, tn), lambda j: (0, j)),
                  pl.BlockSpec((1, tn), lambda j: (0, j))],
        out_specs=pl.BlockSpec((nb, tn), lambda j: (0, j)),
        out_shape=jax.ShapeDtypeStruct((nb, n), F32),
        compiler_params=_cparams(("arbitrary",)),
        name="ada",
    )(c, w_ada, b_ada.reshape(1, n))


def _norm_kernel(x_ref, mod_ref, g_ref, p1_ref, p2_ref, h0_ref, h1_ref, h2_ref):
    blk = p1_ref.shape[0]
    for r in range(x_ref.shape[0] // blk):
        rows = pl.ds(r * blk, blk)
        y = _rms(x_ref[rows, :]) * g_ref[...]
        h = (y * (1.0 + mod_ref[1:2, :]) + mod_ref[0:1, :]).astype(BF16)
        h0_ref[rows, :] = h
        h1_ref[rows, :] = _dot(p1_ref[...], h).astype(BF16)
        h2_ref[rows, :] = _dot(p2_ref[...], h).astype(BF16)


def _residue_major_perm(rows, dil):
    dst = jnp.arange(rows)
    src = (dst % (rows // dil)) * dil + dst // (rows // dil)
    return (jnp.arange(rows)[None, :] == src[:, None]).astype(BF16)


def _norm(x, mod, g_mix, blk):
    b, s, d = x.shape
    m = b * s
    tm = min(NORM_ROWS, s)
    nsb = s // tm
    row = pl.BlockSpec((tm, d), lambda i: (i, 0))
    const = lambda shape: pl.BlockSpec(shape, lambda i: (0, 0), pipeline_mode=pl.Buffered(1))
    perms = [_residue_major_perm(blk, dil) for _, dil in DL_PAIRS[1:]]
    return pl.pallas_call(
        _norm_kernel,
        grid=(m // tm,),
        in_specs=[row, pl.BlockSpec((None, 6, d), lambda i: (i // nsb, 0, 0)), const((1, d)),
                  const((blk, blk)), const((blk, blk))],
        out_specs=[row, row, row],
        out_shape=[jax.ShapeDtypeStruct((m, d), BF16)] * 3,
        compiler_params=_cparams(("parallel",)),
        name="norm",
    )(x.reshape(m, d), mod, g_mix.reshape(1, d), *perms)


def _project(h_ref, w_ref, tab_ref, scale, emit, row_chunk):
    for r in range(h_ref.shape[0] // row_chunk):
        rows = pl.ds(r * row_chunk, row_chunk)
        z = _dot(h_ref[rows, :], w_ref[...])
        if scale is not None:
            cs, sn = scale(tab_ref[0, rows, :], tab_ref[1, rows, :])
            z = jnp.concatenate(
                [zc * cs + pltpu.roll(zc, LANES // 2, 1) * sn
                 for zc in (z[:, cb * LANES:(cb + 1) * LANES] for cb in range(COL_TILE // LANES))],
                axis=1)
        emit(r * row_chunk, z.astype(BF16))


def _qkv_scale(kind):
    def scale(cs, sn):
        f = jnp.where(kind == 0, Q_SCALE, 1.0).astype(F32)
        return jnp.where(kind < 2, cs * f, 1.0), jnp.where(kind < 2, sn * f, 0.0)
    return scale


def _residue_major_emit(zb_ref, blk, row_chunk, col0=0):
    dil = zb_ref.shape[0]
    n = blk // dil
    piece = min(n, row_chunk)

    def emit(row0, z):
        for p in range(row_chunk // piece):
            r0 = row0 + p * piece
            pb, ro = divmod(r0, blk)
            zb_ref[ro // n, pb * n + ro % n:pb * n + ro % n + piece, col0:col0 + COL_TILE] = (
                z[p * piece:(p + 1) * piece, :])
    return emit


_N_MAIN = CT_DL + 3
_N_MAIN_TILES = _N_MAIN + (N_COL_TILES - CT_GA)


def _proj_main_kernel(h_ref, w_ref, tab_ref, hm_ref, zb_ref, zg_ref, *, blk, row_chunk):
    j = pl.program_id(1)

    def head_major(row0, z):
        for hh in range(COL_TILE // DA_PAIR):
            hm_ref[hh, row0:row0 + row_chunk, :] = z[:, hh * DA_PAIR:(hh + 1) * DA_PAIR]

    def gates(row0, z):
        zg_ref[row0:row0 + row_chunk, :] = z

    @pl.when(j < CT_KA)
    def _():
        _project(h_ref, w_ref, tab_ref, lambda cs, sn: (cs * Q_SCALE, sn * Q_SCALE), head_major, row_chunk)

    @pl.when((j >= CT_KA) & (j < CT_VA))
    def _():
        _project(h_ref, w_ref, tab_ref, lambda cs, sn: (cs, sn), head_major, row_chunk)

    @pl.when((j >= CT_VA) & (j < CT_DL))
    def _():
        _project(h_ref, w_ref, tab_ref, None, head_major, row_chunk)

    @pl.when((j >= CT_DL) & (j < _N_MAIN))
    def _():
        _project(h_ref, w_ref, tab_ref, _qkv_scale(j - CT_DL),
                 _residue_major_emit(zb_ref, blk, row_chunk), row_chunk)

    @pl.when(j >= _N_MAIN)
    def _():
        _project(h_ref, w_ref, tab_ref, None, gates, row_chunk)


def _proj_group_kernel(h_ref, w_ref, tab_ref, zb_ref, *, blk, row_chunk):
    scales = (lambda cs, sn: (cs * Q_SCALE, sn * Q_SCALE), lambda cs, sn: (cs, sn), None)
    for kind, scale in enumerate(scales):
        c0 = kind * COL_TILE
        _project(h_ref, w_ref.at[:, c0:c0 + COL_TILE], tab_ref, scale,
                 _residue_major_emit(zb_ref, blk, row_chunk, c0), row_chunk)


def _rotary_lane_layout(w):
    d, n = w.shape
    blk = w.reshape(d, n // LANES, LANES)
    a, b, h = ROPE_HALF, ROPE_ROT, DL_HEAD_DIM
    swapped = jnp.concatenate([blk[..., :a], blk[..., h:h + a], blk[..., b:h], blk[..., a:b],
                               blk[..., h + a:]], axis=-1)
    per_tile = COL_TILE // LANES
    qk_tiles = [CT_QA, CT_QA + 1, CT_KA, CT_KA + 1] + [CT_DL + 3 * g + t for g in range(len(DL_PAIRS))
                                                      for t in (0, 1)]
    qk_blocks = {t * per_tile + c for t in qk_tiles for c in range(per_tile)}
    is_qk = jnp.asarray([i in qk_blocks for i in range(n // LANES)])
    return jnp.where(is_qk[None, :, None], swapped, blk).reshape(d, n)


def _first_head_lanes(lane):
    return (lane < ROPE_HALF) | ((lane >= ROPE_ROT) & (lane < DL_HEAD_DIM + ROPE_HALF))


def _rope_tables(seq):
    inv = 1.0 / (ROPE_THETA ** (jnp.arange(0, ROPE_ROT, 2, dtype=F32) / ROPE_ROT))
    ang = jnp.arange(seq, dtype=F32)[:, None] * inv[None, :]
    cos, sin = jnp.cos(ang), jnp.sin(ang)
    ones = jnp.ones((seq, DL_HEAD_DIM - ROPE_ROT), F32)
    zeros = jnp.zeros((seq, DL_HEAD_DIM - ROPE_ROT), F32)
    cs = jnp.concatenate([cos, cos, ones, cos, cos, ones], axis=1)
    sn = jnp.concatenate([-sin, -sin, zeros, sin, sin, zeros], axis=1)
    return jnp.stack([cs, sn])


def _residue_major_rows(tab, blk, dil):
    k, s, w = tab.shape
    return tab.reshape(k, s // blk, blk // dil, dil, w).transpose(0, 1, 3, 2, 4).reshape(k, s, w)


def _inproj(x, mod, g_mix, w_in):
    b, s, d = x.shape
    m = b * s
    blk = min(PERM_ROWS, s)
    tm = min(PROJ_ROWS, s)
    nsb = s // tm
    row_chunk = min(PROJ_ROW_CHUNK, tm)
    for _, dil in DL_PAIRS:
        assert s % (dil * DL_BLOCK) == 0, "sequence must tile into dilated blocks without padding"
    hs = _norm(x, mod, g_mix, blk)
    w = _rotary_lane_layout(w_in).astype(BF16)
    tab = _rope_tables(s)
    h_spec = pl.BlockSpec((tm, d), lambda i, j: (i, 0))
    tab_spec = pl.BlockSpec((2, tm, LANES), lambda i, j: (0, i % nsb, 0))

    def zb_shape(dil):
        return jax.ShapeDtypeStruct((b, dil, s // dil, 3 * DL_W), BF16)

    def zb_spec(dil, first):
        return pl.BlockSpec((None, dil, tm // dil, COL_TILE),
                            lambda i, j: (i // nsb, 0, i % nsb, jnp.clip(j - first, 0, 2)))

    hm, zb0, zg = pl.pallas_call(
        functools.partial(_proj_main_kernel, blk=blk, row_chunk=row_chunk),
        grid=(b * nsb, _N_MAIN_TILES),
        in_specs=[h_spec,
                  pl.BlockSpec((d, COL_TILE), lambda i, j: (0, jnp.where(j < _N_MAIN, j, j + CT_GA - _N_MAIN))),
                  tab_spec],
        out_specs=[pl.BlockSpec((None, COL_TILE // DA_PAIR, tm, DA_PAIR),
                                lambda i, j: (i // nsb, jnp.minimum(j, CT_DL - 1), i % nsb, 0)),
                   zb_spec(1, CT_DL),
                   pl.BlockSpec((tm, COL_TILE), lambda i, j: (i, jnp.maximum(j - _N_MAIN, 0)))],
        out_shape=[jax.ShapeDtypeStruct((b, 3 * DA_HEADS, s, DA_PAIR), BF16), zb_shape(1),
                   jax.ShapeDtypeStruct((m, (N_COL_TILES - CT_GA) * COL_TILE), BF16)],
        compiler_params=_cparams(("parallel", "arbitrary")),
        name="proj_main",
    )(hs[0], w, tab)
    zbs = [zb0]
    for g in range(1, len(DL_PAIRS)):
        dil = DL_PAIRS[g][1]
        first = CT_DL + 3 * g
        zbs.append(pl.pallas_call(
            functools.partial(_proj_group_kernel, blk=blk, row_chunk=row_chunk),
            grid=(b * nsb,),
            in_specs=[pl.BlockSpec((tm, d), lambda i: (i, 0)),
                      pl.BlockSpec((d, 3 * COL_TILE), lambda i, first=first: (0, first // 3)),
                      pl.BlockSpec((2, tm, LANES), lambda i: (0, i % nsb, 0))],
            out_specs=pl.BlockSpec((None, dil, tm // dil, 3 * COL_TILE), lambda i: (i // nsb, 0, i % nsb, 0)),
            out_shape=zb_shape(dil),
            compiler_params=_cparams(("parallel",)),
            name=f"proj_group{g}",
        )(hs[g], w, _residue_major_rows(tab, blk, dil)))
    return hm, zg, *zbs


def _diffattn_kernel(q_ref, k_ref, v_ref, lq1_ref, lk1_ref, lq2_ref, lk2_ref, gs_ref, o_ref,
                     v2_scr, *, tk, lambda_init):
    qi = pl.program_id(2)
    tq = q_ref.shape[0]
    seq = k_ref.shape[0]

    @pl.when(qi == 0)
    def _():
        v2_scr[:, :DA_PAIR] = v_ref[...]
        v2_scr[:, DA_PAIR:] = jnp.ones((seq, DA_PAIR), BF16)

    lane = lax.broadcasted_iota(jnp.int32, (tq, DA_PAIR), 1)
    q = q_ref[...]
    zero = jnp.zeros_like(q)
    first = _first_head_lanes(lane)
    qmaps = (jnp.where(first, q, zero), jnp.where(first, zero, q))
    m = [None, None]
    acc = [None, None]
    for kk in range(seq // tk):
        kb = k_ref[kk * tk:(kk + 1) * tk, :]
        vb = v2_scr[kk * tk:(kk + 1) * tk, :]
        for mp in range(2):
            s = _nt_dot(qmaps[mp], kb)
            m_cur = jnp.max(s, axis=1, keepdims=True)
            if kk == 0:
                m_new = jnp.broadcast_to(m_cur, (tq, LANES))
            else:
                m_new = jnp.maximum(m[mp], m_cur)
            p = jnp.exp2(s - jnp.tile(m_new, (1, tk // LANES)))
            pv = _dot(p.astype(BF16), vb)
            if kk == 0:
                acc[mp] = pv
            else:
                alpha = jnp.exp2(m[mp] - m_new)
                acc[mp] = acc[mp] * jnp.tile(alpha, (1, 2)) + pv
            m[mp] = m_new

    o1 = acc[0][:, :DA_PAIR] / acc[0][:, DA_PAIR:]
    o2 = acc[1][:, :DA_PAIR] / acc[1][:, DA_PAIR:]
    lam = (jnp.exp(jnp.sum(lq1_ref[...] * lk1_ref[...], axis=1, keepdims=True))
           - jnp.exp(jnp.sum(lq2_ref[...] * lk2_ref[...], axis=1, keepdims=True)) + lambda_init)
    o = o1 - lam * o2
    o_ref[...] = ((_rms(o) * gs_ref[...]) * (1.0 - lambda_init)).astype(BF16)


def _diffattn(hm, lam_q1, lam_k1, lam_q2, lam_k2, g_subln, lambda_init):
    b, nh3, seq, _ = hm.shape
    nh = nh3 // 3
    tq = seq if seq <= DA_WHOLE_SEQ else DA_Q_ROWS
    tk = min(DA_KV_CHUNK, seq)
    lam_spec = pl.BlockSpec((1, DA_HEAD_DIM), lambda bi, h, qi: (0, 0))
    return pl.pallas_call(
        functools.partial(_diffattn_kernel, tk=tk, lambda_init=lambda_init),
        grid=(b, nh, seq // tq),
        in_specs=[pl.BlockSpec((None, None, tq, DA_PAIR), lambda bi, h, qi: (bi, h, qi, 0)),
                  pl.BlockSpec((None, None, seq, DA_PAIR), lambda bi, h, qi: (bi, nh + h, 0, 0)),
                  pl.BlockSpec((None, None, seq, DA_PAIR), lambda bi, h, qi: (bi, 2 * nh + h, 0, 0)),
                  lam_spec, lam_spec, lam_spec, lam_spec,
                  pl.BlockSpec((1, DA_PAIR), lambda bi, h, qi: (0, 0))],
        out_specs=pl.BlockSpec((None, tq, DA_PAIR), lambda bi, h, qi: (bi, qi, h)),
        out_shape=jax.ShapeDtypeStruct((b, seq, nh * DA_PAIR), BF16),
        scratch_shapes=[pltpu.VMEM((seq, 2 * DA_PAIR), BF16)],
        compiler_params=_cparams(("parallel", "parallel", "arbitrary")),
        name="diffattn",
    )(hm, hm, hm, lam_q1.reshape(1, -1), lam_k1.reshape(1, -1), lam_q2.reshape(1, -1),
      lam_k2.reshape(1, -1), g_subln.reshape(1, -1))


def _dilated_kernel(q_ref, k_ref, v_ref, o_ref, lse_ref, *, sub, kw, radius):
    n = pl.program_id(2)
    n_cls, tq, _ = q_ref.shape
    t = k_ref.shape[1]
    lane = lax.broadcasted_iota(jnp.int32, (sub, LANES), 1)
    low = lane < DL_HEAD_DIM
    first = _first_head_lanes(lane)
    zero = jnp.zeros((sub, LANES), BF16)
    ones = jnp.ones((kw, LANES), BF16)
    n_pairs = DL_W // LANES
    for c in range(n_cls):
        o_rows, lse_rows = [], []
        for sb in range(tq // sub):
            q0 = n * tq + sb * sub
            ks = pl.multiple_of(jnp.clip(q0 - radius, 0, t - kw), DL_BLOCK)
            qpos = lax.broadcasted_iota(jnp.int32, (sub, kw), 0) + q0
            kpos = lax.broadcasted_iota(jnp.int32, (sub, kw), 1) + ks
            band = jnp.abs(kpos - qpos) <= radius
            band = jnp.concatenate([band, band], axis=0)
            all_scores = []
            for j in range(n_pairs):
                cols = slice(j * LANES, (j + 1) * LANES)
                qp = q_ref[c, sb * sub:(sb + 1) * sub, cols]
                qcat = jnp.concatenate([jnp.where(first, qp, zero), jnp.where(first, zero, qp)], axis=0)
                all_scores.append(_nt_dot(qcat, k_ref[c, pl.ds(ks, kw), cols]))
            ms, dens, o_cols = [], [], []
            for j in range(n_pairs):
                cols = slice(j * LANES, (j + 1) * LANES)
                vp = jnp.concatenate([v_ref[c, pl.ds(ks, kw), cols], ones], axis=1)
                s = jnp.where(band, all_scores[j], NEG)
                m = jnp.max(s, axis=1, keepdims=True)
                p = jnp.exp2(s - m)
                r = _dot(p.astype(BF16), vp)
                den = r[:, LANES:]
                o2 = r[:, :LANES] / den
                ms += [m[:sub], m[sub:]]
                dens += [den[:sub], den[sub:]]
                o_cols.append(jnp.where(low, o2[:sub], o2[sub:]).astype(BF16))
            m_tile = jnp.zeros((sub, LANES), F32)
            den_tile = jnp.ones((sub, LANES), F32)
            for hd in range(len(ms)):
                m_tile = jnp.where(lane == hd, ms[hd], m_tile)
                den_tile = jnp.where(lane == hd, dens[hd], den_tile)
            lse_tile = m_tile * LN2 + jnp.log(den_tile)
            o_rows.append(jnp.concatenate(o_cols, axis=1))
            lse_rows.append(lse_tile)
        o_ref[c] = jnp.concatenate(o_rows, axis=0)
        lse_ref[c] = jnp.concatenate(lse_rows, axis=0)


def _dilated(zb, g, window, dil):
    b, _, t, _ = zb.shape
    radius = window // (2 * dil)
    rows = min(DL_STEP_ROWS, dil * t)
    tq = min(rows, t)
    n_cls = rows // tq
    sub = min(2 * radius, tq)
    kw = min(sub + 2 * radius, t)
    return pl.pallas_call(
        functools.partial(_dilated_kernel, sub=sub, kw=kw, radius=radius),
        grid=(b, dil // n_cls, t // tq),
        in_specs=[pl.BlockSpec((None, n_cls, tq, DL_W), lambda bi, c, n: (bi, c, n, 0)),
                  pl.BlockSpec((None, n_cls, t, DL_W), lambda bi, c, n: (bi, c, 0, 1)),
                  pl.BlockSpec((None, n_cls, t, DL_W), lambda bi, c, n: (bi, c, 0, 2))],
        out_specs=[pl.BlockSpec((None, n_cls, tq, DL_W), lambda bi, c, n: (bi, c, n, 0)),
                   pl.BlockSpec((None, n_cls, tq, LANES), lambda bi, c, n: (bi, c, n, 0))],
        out_shape=[jax.ShapeDtypeStruct((b, dil, t, DL_W), BF16),
                   jax.ShapeDtypeStruct((b, dil, t, LANES), F32)],
        compiler_params=_cparams(("parallel", "parallel", "arbitrary")),
        name=f"dilated{g}",
    )(zb, zb, zb)


def _natural_order(ref, scr):
    dil, n, w = ref.shape
    slabs = []
    for cb in range(w // LANES):
        cols = slice(cb * LANES, (cb + 1) * LANES)
        if dil == 1:
            slabs.append(ref[0, :, cols].astype(F32))
            continue
        for c in range(dil):
            scr[cb, pl.ds(c, n, stride=dil), :] = ref[c, :, cols].astype(F32)
        slabs.append(scr[cb])
    return slabs


def _tail_kernel(x_ref, mod_ref, oa_ref, o0_ref, o1_ref, o2_ref, l0_ref, l1_ref, l2_ref,
                 ga_ref, gb_ref, wa_ref, wb_ref, wo_ref, ex_ref, g_ref, wgu_ref, wd_ref, gf_ref, o_ref,
                 os1_scr, os2_scr, ls1_scr, ls2_scr, *, chunk, final_norm):
    l0, = _natural_order(l0_ref, None)
    l1, = _natural_order(l1_ref, ls1_scr)
    l2, = _natural_order(l2_ref, ls2_scr)
    o0 = _natural_order(o0_ref, None)
    o1 = _natural_order(o1_ref, os1_scr)
    o2 = _natural_order(o2_ref, os2_scr)
    mx = jnp.maximum(jnp.maximum(l0, l1), l2)
    e0, e1, e2 = jnp.exp(l0 - mx), jnp.exp(l1 - mx), jnp.exp(l2 - mx)
    inv = 1.0 / (e0 + e1 + e2)
    ob = None
    for e, o in ((e0, o0), (e1, o1), (e2, o2)):
        w = e * inv
        w_hi = w.astype(BF16)
        w_lo = (w - w_hi.astype(F32)).astype(BF16)
        wx = _dot(jnp.concatenate([w_hi, w_lo], axis=1), ex_ref[...])
        term = [wx[:, cb * LANES:(cb + 1) * LANES] * o[cb] for cb in range(len(o))]
        ob = term if ob is None else [a + t for a, t in zip(ob, term)]
    ob = jnp.concatenate(ob, axis=1)
    ya = _dot(oa_ref[...], wa_ref[...])
    yb = _dot(ob.astype(BF16), wb_ref[...])
    ga = ga_ref[...].astype(F32)
    gb = gb_ref[...].astype(F32)
    merged = ya / (1.0 + jnp.exp(-ga)) + yb / (1.0 + jnp.exp(-gb))
    y = _dot(merged.astype(BF16), wo_ref[...])
    x1 = x_ref[...] + mod_ref[2:3, :] * y
    h = ((_rms(x1) * g_ref[...]) * (1.0 + mod_ref[4:5, :]) + mod_ref[3:4, :]).astype(BF16)
    hidden = wd_ref.shape[0]
    acc = None
    for c0 in range(0, hidden, chunk):
        c1 = min(c0 + chunk, hidden)
        cols = slice(c0, c1)
        gate = _dot(h, wgu_ref[:, cols])
        up = _dot(h, wgu_ref[:, hidden + c0:hidden + c1])
        f = (gate / (1.0 + jnp.exp(-gate))) * up
        part = _dot(f.astype(BF16), wd_ref[cols, :])
        acc = part if acc is None else acc + part
    x2 = x1 + mod_ref[5:6, :] * acc
    o_ref[...] = _rms(x2) * gf_ref[...] if final_norm else x2


def _tail(x, mod, zg, oa, obs, lses, wa_bf, wb_bf, wo_bf, g_ffn, wgu_bf, wd_bf, g_final, final_norm):
    b, s, d = x.shape
    m = b * s
    tm = min(TAIL_ROWS, s)
    nsb = s // tm
    hidden = wd_bf.shape[0]
    chunk = FFN_CHUNK
    assert hidden % LANES == 0
    row = lambda w: pl.BlockSpec((tm, w), lambda i: (i, 0))
    full = lambda shape: pl.BlockSpec(shape, lambda i: (0, 0), pipeline_mode=pl.Buffered(1))

    def residue_major(dil, w):
        return pl.BlockSpec((None, dil, tm // dil, w), lambda i: (i // nsb, 0, i % nsb, 0))

    dils = [dil for _, dil in DL_PAIRS]
    head = jnp.arange(LANES)[:, None]
    col = jnp.arange(DL_W)[None, :]
    expand = ((col // DL_HEAD_DIM) == head).astype(BF16)
    expand = jnp.concatenate([expand, expand], axis=0)
    return pl.pallas_call(
        functools.partial(_tail_kernel, chunk=chunk, final_norm=final_norm),
        grid=(m // tm,),
        in_specs=[row(d), pl.BlockSpec((None, 6, d), lambda i: (i // nsb, 0, 0)),
                  row(DA_HEADS * DA_PAIR)]
                 + [residue_major(dil, DL_W) for dil in dils]
                 + [residue_major(dil, LANES) for dil in dils]
                 + [pl.BlockSpec((tm, d), lambda i: (i, 0)), pl.BlockSpec((tm, d), lambda i: (i, 1)),
                    full(wa_bf.shape), full(wb_bf.shape), full(wo_bf.shape), full(expand.shape),
                    full((1, d)), full(wgu_bf.shape), full(wd_bf.shape), full((1, d))],
        out_specs=row(d),
        out_shape=jax.ShapeDtypeStruct((m, d), F32),
        scratch_shapes=[pltpu.VMEM((DL_W // LANES, tm, LANES), F32),
                        pltpu.VMEM((DL_W // LANES, tm, LANES), F32),
                        pltpu.VMEM((1, tm, LANES), F32), pltpu.VMEM((1, tm, LANES), F32)],
        compiler_params=_cparams(("parallel",)),
        name="tail",
    )(x.reshape(m, d), mod, oa.reshape(m, -1), *obs, *lses, zg, zg, wa_bf, wb_bf, wo_bf, expand,
      g_ffn.reshape(1, d), wgu_bf, wd_bf, g_final.reshape(1, d))


def _trunk(x, mod_all, weights):
    (g_mix, w_in, lam_q1, lam_k1, lam_q2, lam_k2, g_subln, w_a, w_b, w_out, g_ffn,
     w_gu, w_down, g_final) = weights
    b, s, d = x.shape
    depth = w_in.shape[0]
    for l in range(depth):
        lambda_init = 0.8 - 0.6 * math.exp(-0.3 * l)
        mod = mod_all[l]
        hm, zg, *zbs = _inproj(x, mod, g_mix[l], w_in[l])
        oa = _diffattn(hm, lam_q1[l], lam_k1[l], lam_q2[l], lam_k2[l], g_subln[l], lambda_init)
        obs, lses = [], []
        for g, (window, dil) in enumerate(DL_PAIRS):
            o, lse = _dilated(zbs[g], g, window, dil)
            obs.append(o)
            lses.append(lse)
        x = _tail(x, mod, zg, oa, obs, lses, w_a[l].astype(BF16), w_b[l].astype(BF16),
                  w_out[l].astype(BF16), g_ffn[l], w_gu[l].astype(BF16), w_down[l].astype(BF16),
                  g_final, l == depth - 1).reshape(b, s, d)
    return x


def kernel(x_prompt, x_sample, c_prompt, c_sample, w_ada, b_ada, g_mix, w_in, lam_q1, lam_k1,
           lam_q2, lam_k2, g_subln, w_a, w_b, w_out, g_ffn, w_gu, w_down, g_final):
    depth = w_in.shape[0]
    bp = c_prompt.shape[0]
    c_all = jnp.concatenate([c_prompt, c_sample], axis=0)
    mods = [_ada(c_all, w_ada[l], b_ada[l]).reshape(c_all.shape[0], 6, D_MODEL) for l in range(depth)]
    weights = (g_mix, w_in, lam_q1, lam_k1, lam_q2, lam_k2, g_subln, w_a, w_b, w_out, g_ffn,
               w_gu, w_down, g_final)
    y_prompt = _trunk(x_prompt, [m[:bp] for m in mods], weights)
    y_sample = _trunk(x_sample, [m[bp:] for m in mods], weights)
    return (y_prompt, y_sample)
```

```python
import functools
import math

import jax
import jax.numpy as jnp
from jax import lax
from jax.experimental import pallas as pl
from jax.experimental.pallas import tpu as pltpu

F32 = jnp.float32
BF16 = jnp.bfloat16

D_MODEL = 1024
DA_HEADS = 8
DA_HEAD_DIM = 64
DA_PAIR = 2 * DA_HEAD_DIM
DL_PAIRS = ((128, 1), (512, 4), (2048, 16))
DL_HEADS = 8
DL_HEAD_DIM = 64
DL_W = DL_HEADS * DL_HEAD_DIM
DL_BLOCK = 64
ROPE_THETA = 500000.0
ROPE_ROT = DL_HEAD_DIM // 4
ROPE_HALF = ROPE_ROT // 2
EPS = 1e-6
NEG = -1e30
LOG2E = math.log2(math.e)
LN2 = math.log(2.0)
Q_SCALE = DA_HEAD_DIM ** -0.5 * LOG2E

LANES = 128
COL_TILE = 512
N_IN = 2 * DA_HEADS * DA_PAIR + DA_HEADS * DA_PAIR + 9 * DL_W + 2 * D_MODEL
N_COL_TILES = N_IN // COL_TILE
CT_QA, CT_KA, CT_VA = 0, 2, 4
CT_DL = 6
CT_GA = 15
VMEM_LIMIT = 56 * 1024 * 1024

NORM_ROWS = 1024
PERM_ROWS = 256
PROJ_ROWS = 2048
PROJ_ROW_CHUNK = 256
DA_WHOLE_SEQ = 2048
DA_Q_ROWS = 2048
DA_KV_CHUNK = 256
DL_STEP_ROWS = 2048
TAIL_ROWS = 512
FFN_CHUNK = 768


def _cparams(sem):
    return pltpu.CompilerParams(dimension_semantics=sem, vmem_limit_bytes=VMEM_LIMIT)


def _nt_dot(a, b):
    return lax.dot_general(a, b, (((1,), (1,)), ((), ())), preferred_element_type=F32)


def _dot(a, b):
    return jnp.dot(a, b, preferred_element_type=F32)


def _rms(x):
    return x * lax.rsqrt(jnp.mean(x * x, axis=-1, keepdims=True) + EPS)


def _ada_kernel(c_ref, w_ref, b_ref, o_ref):
    c = c_ref[...]
    a = c * (1.0 / (1.0 + jnp.exp(-c)))
    w = w_ref[...]
    a_hi = a.astype(BF16)
    a_lo = (a - a_hi.astype(F32)).astype(BF16)
    w_hi = w.astype(BF16)
    w_lo = (w - w_hi.astype(F32)).astype(BF16)
    o_ref[...] = _dot(a_hi, w_hi) + _dot(a_hi, w_lo) + _dot(a_lo, w_hi) + b_ref[...]


def _ada(c, w_ada, b_ada):
    nb, d = c.shape
    n = w_ada.shape[1]
    tn = COL_TILE
    return pl.pallas_call(
        _ada_kernel,
        grid=(n // tn,),
        in_specs=[pl.BlockSpec((nb, d), lambda j: (0, 0)),
                  pl.BlockSpec((d, tn), lambda j: (0, j)),
                  pl.BlockSpec((1, tn), lambda j: (0, j))],
        out_specs=pl.BlockSpec((nb, tn), lambda j: (0, j)),
        out_shape=jax.ShapeDtypeStruct((nb, n), F32),
        compiler_params=_cparams(("arbitrary",)),
        name="ada",
    )(c, w_ada, b_ada.reshape(1, n))


def _norm_kernel(x_ref, mod_ref, g_ref, p1_ref, p2_ref, h0_ref, h1_ref, h2_ref):
    blk = p1_ref.shape[0]
    for r in range(x_ref.shape[0] // blk):
        rows = pl.ds(r * blk, blk)
        y = _rms(x_ref[rows, :]) * g_ref[...]
        h = (y * (1.0 + mod_ref[1:2, :]) + mod_ref[0:1, :]).astype(BF16)
        h0_ref[rows, :] = h
        h1_ref[rows, :] = _dot(p1_ref[...], h).astype(BF16)
        h2_ref[rows, :] = _dot(p2_ref[...], h).astype(BF16)


def _residue_major_perm(rows, dil):
    dst = jnp.arange(rows)
    src = (dst % (rows // dil)) * dil + dst // (rows // dil)
    return (jnp.arange(rows)[None, :] == src[:, None]).astype(BF16)


def _norm(x, mod, g_mix, blk):
    b, s, d = x.shape
    m = b * s
    tm = min(NORM_ROWS, s)
    nsb = s // tm
    row = pl.BlockSpec((tm, d), lambda i: (i, 0))
    const = lambda shape: pl.BlockSpec(shape, lambda i: (0, 0), pipeline_mode=pl.Buffered(1))
    perms = [_residue_major_perm(blk, dil) for _, dil in DL_PAIRS[1:]]
    return pl.pallas_call(
        _norm_kernel,
        grid=(m // tm,),
        in_specs=[row, pl.BlockSpec((None, 6, d), lambda i: (i // nsb, 0, 0)), const((1, d)),
                  const((blk, blk)), const((blk, blk))],
        out_specs=[row, row, row],
        out_shape=[jax.ShapeDtypeStruct((m, d), BF16)] * 3,
        compiler_params=_cparams(("parallel",)),
        name="norm",
    )(x.reshape(m, d), mod, g_mix.reshape(1, d), *perms)


def _project(h_ref, w_ref, tab_ref, scale, emit, row_chunk):
    for r in range(h_ref.shape[0] // row_chunk):
        rows = pl.ds(r * row_chunk, row_chunk)
        z = _dot(h_ref[rows, :], w_ref[...])
        if scale is not None:
            cs, sn = scale(tab_ref[0, rows, :], tab_ref[1, rows, :])
            z = jnp.concatenate(
                [zc * cs + pltpu.roll(zc, LANES // 2, 1) * sn
                 for zc in (z[:, cb * LANES:(cb + 1) * LANES] for cb in range(COL_TILE // LANES))],
                axis=1)
        emit(r * row_chunk, z.astype(BF16))


def _residue_major_emit(zb_ref, blk, row_chunk, col0=0):
    dil = zb_ref.shape[0]
    n = blk // dil
    piece = min(n, row_chunk)

    def emit(row0, z):
        for p in range(row_chunk // piece):
            r0 = row0 + p * piece
            pb, ro = divmod(r0, blk)
            zb_ref[ro // n, pb * n + ro % n:pb * n + ro % n + piece, col0:col0 + COL_TILE] = (
                z[p * piece:(p + 1) * piece, :])
    return emit


_N_MAIN = CT_DL
_N_MAIN_TILES = _N_MAIN + (N_COL_TILES - CT_GA)


def _proj_main_kernel(h_ref, w_ref, tab_ref, hm_ref, zg_ref, *, row_chunk):
    j = pl.program_id(1)

    def head_major(row0, z):
        for hh in range(COL_TILE // DA_PAIR):
            hm_ref[hh, row0:row0 + row_chunk, :] = z[:, hh * DA_PAIR:(hh + 1) * DA_PAIR]

    def gates(row0, z):
        zg_ref[row0:row0 + row_chunk, :] = z

    @pl.when(j < CT_KA)
    def _():
        _project(h_ref, w_ref, tab_ref, lambda cs, sn: (cs * Q_SCALE, sn * Q_SCALE), head_major, row_chunk)

    @pl.when((j >= CT_KA) & (j < CT_VA))
    def _():
        _project(h_ref, w_ref, tab_ref, lambda cs, sn: (cs, sn), head_major, row_chunk)

    @pl.when((j >= CT_VA) & (j < CT_DL))
    def _():
        _project(h_ref, w_ref, tab_ref, None, head_major, row_chunk)

    @pl.when(j >= _N_MAIN)
    def _():
        _project(h_ref, w_ref, tab_ref, None, gates, row_chunk)


def _proj_group_kernel(h_ref, w_ref, tab_ref, zb_ref, *, blk, row_chunk):
    scales = (lambda cs, sn: (cs * Q_SCALE, sn * Q_SCALE), lambda cs, sn: (cs, sn), None)
    for kind, scale in enumerate(scales):
        c0 = kind * COL_TILE
        _project(h_ref, w_ref.at[:, c0:c0 + COL_TILE], tab_ref, scale,
                 _residue_major_emit(zb_ref, blk, row_chunk, c0), row_chunk)


def _rotary_lane_layout(w):
    d, n = w.shape
    blk = w.reshape(d, n // LANES, LANES)
    a, b, h = ROPE_HALF, ROPE_ROT, DL_HEAD_DIM
    swapped = jnp.concatenate([blk[..., :a], blk[..., h:h + a], blk[..., b:h], blk[..., a:b],
                               blk[..., h + a:]], axis=-1)
    per_tile = COL_TILE // LANES
    qk_tiles = [CT_QA, CT_QA + 1, CT_KA, CT_KA + 1] + [CT_DL + 3 * g + t for g in range(len(DL_PAIRS))
                                                      for t in (0, 1)]
    qk_blocks = {t * per_tile + c for t in qk_tiles for c in range(per_tile)}
    is_qk = jnp.asarray([i in qk_blocks for i in range(n // LANES)])
    return jnp.where(is_qk[None, :, None], swapped, blk).reshape(d, n)


def _first_head_lanes(lane):
    return (lane < ROPE_HALF) | ((lane >= ROPE_ROT) & (lane < DL_HEAD_DIM + ROPE_HALF))


def _rope_tables(seq):
    inv = 1.0 / (ROPE_THETA ** (jnp.arange(0, ROPE_ROT, 2, dtype=F32) / ROPE_ROT))
    ang = jnp.arange(seq, dtype=F32)[:, None] * inv[None, :]
    cos, sin = jnp.cos(ang), jnp.sin(ang)
    ones = jnp.ones((seq, DL_HEAD_DIM - ROPE_ROT), F32)
    zeros = jnp.zeros((seq, DL_HEAD_DIM - ROPE_ROT), F32)
    cs = jnp.concatenate([cos, cos, ones, cos, cos, ones], axis=1)
    sn = jnp.concatenate([-sin, -sin, zeros, sin, sin, zeros], axis=1)
    return jnp.stack([cs, sn])


def _residue_major_rows(tab, blk, dil):
    k, s, w = tab.shape
    return tab.reshape(k, s // blk, blk // dil, dil, w).transpose(0, 1, 3, 2, 4).reshape(k, s, w)


def _inproj(x, mod, g_mix, w_in):
    b, s, d = x.shape
    m = b * s
    blk = min(PERM_ROWS, s)
    tm = min(PROJ_ROWS, s)
    nsb = s // tm
    row_chunk = min(PROJ_ROW_CHUNK, tm)
    for _, dil in DL_PAIRS:
        assert s % (dil * DL_BLOCK) == 0, "sequence must tile into dilated blocks without padding"
    hs = _norm(x, mod, g_mix, blk)
    w = _rotary_lane_layout(w_in).astype(BF16)
    tab = _rope_tables(s)
    h_spec = pl.BlockSpec((tm, d), lambda i, j: (i, 0))
    tab_spec = pl.BlockSpec((2, tm, LANES), lambda i, j: (0, i % nsb, 0))

    def zb_shape(dil):
        return jax.ShapeDtypeStruct((b, dil, s // dil, 3 * DL_W), BF16)

    hm, zg = pl.pallas_call(
        functools.partial(_proj_main_kernel, row_chunk=row_chunk),
        grid=(b * nsb, _N_MAIN_TILES),
        in_specs=[h_spec,
                  pl.BlockSpec((d, COL_TILE), lambda i, j: (0, jnp.where(j < _N_MAIN, j, j + CT_GA - _N_MAIN))),
                  tab_spec],
        out_specs=[pl.BlockSpec((None, COL_TILE // DA_PAIR, tm, DA_PAIR),
                                lambda i, j: (i // nsb, jnp.minimum(j, CT_DL - 1), i % nsb, 0)),
                   pl.BlockSpec((tm, COL_TILE), lambda i, j: (i, jnp.maximum(j - _N_MAIN, 0)))],
        out_shape=[jax.ShapeDtypeStruct((b, 3 * DA_HEADS, s, DA_PAIR), BF16),
                   jax.ShapeDtypeStruct((m, (N_COL_TILES - CT_GA) * COL_TILE), BF16)],
        compiler_params=_cparams(("parallel", "arbitrary")),
        name="proj_main",
    )(hs[0], w, tab)
    zbs = []
    for g in range(len(DL_PAIRS)):
        dil = DL_PAIRS[g][1]
        first = CT_DL + 3 * g
        zbs.append(pl.pallas_call(
            functools.partial(_proj_group_kernel, blk=blk, row_chunk=row_chunk),
            grid=(b * nsb,),
            in_specs=[pl.BlockSpec((tm, d), lambda i: (i, 0)),
                      pl.BlockSpec((d, 3 * COL_TILE), lambda i, first=first: (0, first // 3)),
                      pl.BlockSpec((2, tm, LANES), lambda i: (0, i % nsb, 0))],
            out_specs=pl.BlockSpec((None, dil, tm // dil, 3 * COL_TILE), lambda i: (i // nsb, 0, i % nsb, 0)),
            out_shape=zb_shape(dil),
            compiler_params=_cparams(("parallel",)),
            name=f"proj_group{g}",
        )(hs[g], w, _residue_major_rows(tab, blk, dil)))
    return hm, zg, *zbs


def _diffattn_kernel(q_ref, k_ref, v_ref, lq1_ref, lk1_ref, lq2_ref, lk2_ref, gs_ref, o_ref,
                     v2_scr, *, tk, lambda_init):
    qi = pl.program_id(2)
    tq = q_ref.shape[0]
    seq = k_ref.shape[0]

    @pl.when(qi == 0)
    def _():
        v2_scr[:, :DA_PAIR] = v_ref[...]
        v2_scr[:, DA_PAIR:] = jnp.ones((seq, DA_PAIR), BF16)

    lane = lax.broadcasted_iota(jnp.int32, (tq, DA_PAIR), 1)
    q = q_ref[...]
    zero = jnp.zeros_like(q)
    first = _first_head_lanes(lane)
    qmaps = (jnp.where(first, q, zero), jnp.where(first, zero, q))
    m = [None, None]
    acc = [None, None]
    for kk in range(seq // tk):
        kb = k_ref[kk * tk:(kk + 1) * tk, :]
        vb = v2_scr[kk * tk:(kk + 1) * tk, :]
        for mp in range(2):
            s = _nt_dot(qmaps[mp], kb)
            m_cur = jnp.max(s, axis=1, keepdims=True)
            if kk == 0:
                m_new = jnp.broadcast_to(m_cur, (tq, LANES))
            else:
                m_new = jnp.maximum(m[mp], m_cur)
            p = jnp.exp2(s - jnp.tile(m_new, (1, tk // LANES)))
            pv = _dot(p.astype(BF16), vb)
            if kk == 0:
                acc[mp] = pv
            else:
                alpha = jnp.exp2(m[mp] - m_new)
                acc[mp] = acc[mp] * jnp.tile(alpha, (1, 2)) + pv
            m[mp] = m_new

    o1 = acc[0][:, :DA_PAIR] / acc[0][:, DA_PAIR:]
    o2 = acc[1][:, :DA_PAIR] / acc[1][:, DA_PAIR:]
    lam = (jnp.exp(jnp.sum(lq1_ref[...] * lk1_ref[...], axis=1, keepdims=True))
           - jnp.exp(jnp.sum(lq2_ref[...] * lk2_ref[...], axis=1, keepdims=True)) + lambda_init)
    o = o1 - lam * o2
    o_ref[...] = ((_rms(o) * gs_ref[...]) * (1.0 - lambda_init)).astype(BF16)


def _diffattn(hm, lam_q1, lam_k1, lam_q2, lam_k2, g_subln, lambda_init):
    b, nh3, seq, _ = hm.shape
    nh = nh3 // 3
    tq = seq if seq <= DA_WHOLE_SEQ else DA_Q_ROWS
    tk = min(DA_KV_CHUNK, seq)
    lam_spec = pl.BlockSpec((1, DA_HEAD_DIM), lambda bi, h, qi: (0, 0))
    return pl.pallas_call(
        functools.partial(_diffattn_kernel, tk=tk, lambda_init=lambda_init),
        grid=(b, nh, seq // tq),
        in_specs=[pl.BlockSpec((None, None, tq, DA_PAIR), lambda bi, h, qi: (bi, h, qi, 0)),
                  pl.BlockSpec((None, None, seq, DA_PAIR), lambda bi, h, qi: (bi, nh + h, 0, 0)),
                  pl.BlockSpec((None, None, seq, DA_PAIR), lambda bi, h, qi: (bi, 2 * nh + h, 0, 0)),
                  lam_spec, lam_spec, lam_spec, lam_spec,
                  pl.BlockSpec((1, DA_PAIR), lambda bi, h, qi: (0, 0))],
        out_specs=pl.BlockSpec((None, tq, DA_PAIR), lambda bi, h, qi: (bi, qi, h)),
        out_shape=jax.ShapeDtypeStruct((b, seq, nh * DA_PAIR), BF16),
        scratch_shapes=[pltpu.VMEM((seq, 2 * DA_PAIR), BF16)],
        compiler_params=_cparams(("parallel", "parallel", "arbitrary")),
        name="diffattn",
    )(hm, hm, hm, lam_q1.reshape(1, -1), lam_k1.reshape(1, -1), lam_q2.reshape(1, -1),
      lam_k2.reshape(1, -1), g_subln.reshape(1, -1))


def _dilated_kernel(q_ref, k_ref, v_ref, o_ref, lse_ref, *, sub, kw, radius):
    n = pl.program_id(2)
    n_cls, tq, _ = q_ref.shape
    t = k_ref.shape[1]
    lane = lax.broadcasted_iota(jnp.int32, (sub, LANES), 1)
    low = lane < DL_HEAD_DIM
    first = _first_head_lanes(lane)
    zero = jnp.zeros((sub, LANES), BF16)
    ones = jnp.ones((kw, LANES), BF16)
    n_pairs = DL_W // LANES
    for c in range(n_cls):
        o_rows, lse_rows = [], []
        for sb in range(tq // sub):
            q0 = n * tq + sb * sub
            ks = pl.multiple_of(jnp.clip(q0 - radius, 0, t - kw), DL_BLOCK)
            qpos = lax.broadcasted_iota(jnp.int32, (sub, kw), 0) + q0
            kpos = lax.broadcasted_iota(jnp.int32, (sub, kw), 1) + ks
            band = jnp.abs(kpos - qpos) <= radius
            band = jnp.concatenate([band, band], axis=0)
            all_scores = []
            for j in range(n_pairs):
                cols = slice(j * LANES, (j + 1) * LANES)
                qp = q_ref[c, sb * sub:(sb + 1) * sub, cols]
                qcat = jnp.concatenate([jnp.where(first, qp, zero), jnp.where(first, zero, qp)], axis=0)
                all_scores.append(_nt_dot(qcat, k_ref[c, pl.ds(ks, kw), cols]))
            ms, dens, o_cols = [], [], []
            for j in range(n_pairs):
                cols = slice(j * LANES, (j + 1) * LANES)
                vp = jnp.concatenate([v_ref[c, pl.ds(ks, kw), cols], ones], axis=1)
                s = jnp.where(band, all_scores[j], NEG)
                m = jnp.max(s, axis=1, keepdims=True)
                p = jnp.exp2(s - m)
                r = _dot(p.astype(BF16), vp)
                den = r[:, LANES:]
                o2 = r[:, :LANES] / den
                ms += [m[:sub], m[sub:]]
                dens += [den[:sub], den[sub:]]
                o_cols.append(jnp.where(low, o2[:sub], o2[sub:]).astype(BF16))
            m_tile = jnp.zeros((sub, LANES), F32)
            den_tile = jnp.ones((sub, LANES), F32)
            for hd in range(len(ms)):
                m_tile = jnp.where(lane == hd, ms[hd], m_tile)
                den_tile = jnp.where(lane == hd, dens[hd], den_tile)
            lse_tile = m_tile * LN2 + jnp.log(den_tile)
            o_rows.append(jnp.concatenate(o_cols, axis=1))
            lse_rows.append(lse_tile)
        o_ref[c] = jnp.concatenate(o_rows, axis=0)
        lse_ref[c] = jnp.concatenate(lse_rows, axis=0)


def _dilated(zb, g, window, dil):
    b, _, t, _ = zb.shape
    radius = window // (2 * dil)
    rows = min(DL_STEP_ROWS, dil * t)
    tq = min(rows, t)
    n_cls = rows // tq
    sub = min(2 * radius, tq)
    kw = min(sub + 2 * radius, t)
    return pl.pallas_call(
        functools.partial(_dilated_kernel, sub=sub, kw=kw, radius=radius),
        grid=(b, dil // n_cls, t // tq),
        in_specs=[pl.BlockSpec((None, n_cls, tq, DL_W), lambda bi, c, n: (bi, c, n, 0)),
                  pl.BlockSpec((None, n_cls, t, DL_W), lambda bi, c, n: (bi, c, 0, 1)),
                  pl.BlockSpec((None, n_cls, t, DL_W), lambda bi, c, n: (bi, c, 0, 2))],
        out_specs=[pl.BlockSpec((None, n_cls, tq, DL_W), lambda bi, c, n: (bi, c, n, 0)),
                   pl.BlockSpec((None, n_cls, tq, LANES), lambda bi, c, n: (bi, c, n, 0))],
        out_shape=[jax.ShapeDtypeStruct((b, dil, t, DL_W), BF16),
                   jax.ShapeDtypeStruct((b, dil, t, LANES), F32)],
        compiler_params=_cparams(("parallel", "parallel", "arbitrary")),
        name=f"dilated{g}",
    )(zb, zb, zb)


def _natural_order(ref, scr):
    dil, n, w = ref.shape
    slabs = []
    for cb in range(w // LANES):
        cols = slice(cb * LANES, (cb + 1) * LANES)
        if dil == 1:
            slabs.append(ref[0, :, cols].astype(F32))
            continue
        for c in range(dil):
            scr[cb, pl.ds(c, n, stride=dil), :] = ref[c, :, cols].astype(F32)
        slabs.append(scr[cb])
    return slabs


def _tail_kernel(x_ref, mod_ref, oa_ref, o0_ref, o1_ref, o2_ref, l0_ref, l1_ref, l2_ref,
                 ga_ref, gb_ref, wa_ref, wb_ref, wo_ref, ex_ref, g_ref, wgu_ref, wd_ref, gf_ref, o_ref,
                 os1_scr, os2_scr, ls1_scr, ls2_scr, *, chunk, final_norm):
    l0, = _natural_order(l0_ref, None)
    l1, = _natural_order(l1_ref, ls1_scr)
    l2, = _natural_order(l2_ref, ls2_scr)
    o0 = _natural_order(o0_ref, None)
    o1 = _natural_order(o1_ref, os1_scr)
    o2 = _natural_order(o2_ref, os2_scr)
    mx = jnp.maximum(jnp.maximum(l0, l1), l2)
    e0, e1, e2 = jnp.exp(l0 - mx), jnp.exp(l1 - mx), jnp.exp(l2 - mx)
    inv = 1.0 / (e0 + e1 + e2)
    ob = None
    for e, o in ((e0, o0), (e1, o1), (e2, o2)):
        w = e * inv
        w_hi = w.astype(BF16)
        w_lo = (w - w_hi.astype(F32)).astype(BF16)
        wx = _dot(jnp.concatenate([w_hi, w_lo], axis=1), ex_ref[...])
        term = [wx[:, cb * LANES:(cb + 1) * LANES] * o[cb] for cb in range(len(o))]
        ob = term if ob is None else [a + t for a, t in zip(ob, term)]
    ob = jnp.concatenate(ob, axis=1)
    ya = _dot(oa_ref[...], wa_ref[...])
    yb = _dot(ob.astype(BF16), wb_ref[...])
    ga = ga_ref[...].astype(F32)
    gb = gb_ref[...].astype(F32)
    merged = ya / (1.0 + jnp.exp(-ga)) + yb / (1.0 + jnp.exp(-gb))
    y = _dot(merged.astype(BF16), wo_ref[...])
    x1 = x_ref[...] + mod_ref[2:3, :] * y
    h = ((_rms(x1) * g_ref[...]) * (1.0 + mod_ref[4:5, :]) + mod_ref[3:4, :]).astype(BF16)
    hidden = wd_ref.shape[0]
    acc = None
    for c0 in range(0, hidden, chunk):
        c1 = min(c0 + chunk, hidden)
        cols = slice(c0, c1)
        gate = _dot(h, wgu_ref[:, cols])
        up = _dot(h, wgu_ref[:, hidden + c0:hidden + c1])
        f = (gate / (1.0 + jnp.exp(-gate))) * up
        part = _dot(f.astype(BF16), wd_ref[cols, :])
        acc = part if acc is None else acc + part
    x2 = x1 + mod_ref[5:6, :] * acc
    o_ref[...] = _rms(x2) * gf_ref[...] if final_norm else x2


def _tail(x, mod, zg, oa, obs, lses, wa_bf, wb_bf, wo_bf, g_ffn, wgu_bf, wd_bf, g_final, final_norm):
    b, s, d = x.shape
    m = b * s
    tm = min(TAIL_ROWS, s)
    nsb = s // tm
    hidden = wd_bf.shape[0]
    chunk = FFN_CHUNK
    assert hidden % LANES == 0
    row = lambda w: pl.BlockSpec((tm, w), lambda i: (i, 0))
    full = lambda shape: pl.BlockSpec(shape, lambda i: (0, 0), pipeline_mode=pl.Buffered(1))

    def residue_major(dil, w):
        return pl.BlockSpec((None, dil, tm // dil, w), lambda i: (i // nsb, 0, i % nsb, 0))

    dils = [dil for _, dil in DL_PAIRS]
    head = jnp.arange(LANES)[:, None]
    col = jnp.arange(DL_W)[None, :]
    expand = ((col // DL_HEAD_DIM) == head).astype(BF16)
    expand = jnp.concatenate([expand, expand], axis=0)
    return pl.pallas_call(
        functools.partial(_tail_kernel, chunk=chunk, final_norm=final_norm),
        grid=(m // tm,),
        in_specs=[row(d), pl.BlockSpec((None, 6, d), lambda i: (i // nsb, 0, 0)),
                  row(DA_HEADS * DA_PAIR)]
                 + [residue_major(dil, DL_W) for dil in dils]
                 + [residue_major(dil, LANES) for dil in dils]
                 + [pl.BlockSpec((tm, d), lambda i: (i, 0)), pl.BlockSpec((tm, d), lambda i: (i, 1)),
                    full(wa_bf.shape), full(wb_bf.shape), full(wo_bf.shape), full(expand.shape),
                    full((1, d)), full(wgu_bf.shape), full(wd_bf.shape), full((1, d))],
        out_specs=row(d),
        out_shape=jax.ShapeDtypeStruct((m, d), F32),
        scratch_shapes=[pltpu.VMEM((DL_W // LANES, tm, LANES), F32),
                        pltpu.VMEM((DL_W // LANES, tm, LANES), F32),
                        pltpu.VMEM((1, tm, LANES), F32), pltpu.VMEM((1, tm, LANES), F32)],
        compiler_params=_cparams(("parallel",)),
        name="tail",
    )(x.reshape(m, d), mod, oa.reshape(m, -1), *obs, *lses, zg, zg, wa_bf, wb_bf, wo_bf, expand,
      g_ffn.reshape(1, d), wgu_bf, wd_bf, g_final.reshape(1, d))


def _trunk(x, mod_all, weights):
    (g_mix, w_in, lam_q1, lam_k1, lam_q2, lam_k2, g_subln, w_a, w_b, w_out, g_ffn,
     w_gu, w_down, g_final) = weights
    b, s, d = x.shape
    depth = w_in.shape[0]
    for l in range(depth):
        lambda_init = 0.8 - 0.6 * math.exp(-0.3 * l)
        mod = mod_all[l]
        hm, zg, *zbs = _inproj(x, mod, g_mix[l], w_in[l])
        oa = _diffattn(hm, lam_q1[l], lam_k1[l], lam_q2[l], lam_k2[l], g_subln[l], lambda_init)
        obs, lses = [], []
        for g, (window, dil) in enumerate(DL_PAIRS):
            o, lse = _dilated(zbs[g], g, window, dil)
            obs.append(o)
            lses.append(lse)
        x = _tail(x, mod, zg, oa, obs, lses, w_a[l].astype(BF16), w_b[l].astype(BF16),
                  w_out[l].astype(BF16), g_ffn[l], w_gu[l].astype(BF16), w_down[l].astype(BF16),
                  g_final, l == depth - 1).reshape(b, s, d)
    return x


def kernel(x_prompt, x_sample, c_prompt, c_sample, w_ada, b_ada, g_mix, w_in, lam_q1, lam_k1,
           lam_q2, lam_k2, g_subln, w_a, w_b, w_out, g_ffn, w_gu, w_down, g_final):
    depth = w_in.shape[0]
    bp = c_prompt.shape[0]
    c_all = jnp.concatenate([c_prompt, c_sample], axis=0)
    mods = [_ada(c_all, w_ada[l], b_ada[l]).reshape(c_all.shape[0], 6, D_MODEL) for l in range(depth)]
    weights = (g_mix, w_in, lam_q1, lam_k1, lam_q2, lam_k2, g_subln, w_a, w_b, w_out, g_ffn,
               w_gu, w_down, g_final)
    y_prompt = _trunk(x_prompt, [m[:bp] for m in mods], weights)
    y_sample = _trunk(x_sample, [m[bp:] for m in mods], weights)
    return (y_prompt, y_sample)
```
